```python
import math
import jax
import jax.numpy as jnp
from jax import lax
import numpy as np

D_MODEL = 1024
BATCH = 4
SEQ = 8192
DEPTH = 2

EPS = 1e-6
N_BRANCHES = 3
D_SSM = 3 * D_MODEL // 4
SSM_GROUP = 16
SSM_GROUPS = D_SSM // SSM_GROUP
SSM_STATE = 64
DT_MIN = 0.001
DT_MAX = 0.1
ATTN_HEAD_DIM = 64
ATTN_HEADS_PER_GROUP = 4
ATTN_CONFIGS = ((128, 1), (512, 4), (2048, 16))
N_ATTN_HEADS = ATTN_HEADS_PER_GROUP * len(ATTN_CONFIGS)
D_ATTN = N_ATTN_HEADS * ATTN_HEAD_DIM
ATTN_BLOCK = 128
NUM_BUCKETS = 32
REL_MAX_DISTANCE = 2048
NEG_INF = -1e30
MEM_LEN = 256
MEM_HEADS = 4
MEM_HEAD_DIM = D_MODEL // 8
D_MEM = MEM_HEADS * MEM_HEAD_DIM
D_IN = 2 * D_SSM + 4 * D_ATTN + 2 * D_MEM + N_BRANCHES * D_MODEL

kernel_name = "hybrid_s5_dilated_attn_memxattn_gated"


def rms_norm(x, g):
    x32 = x.astype(jnp.float32)
    y = x32 * lax.rsqrt(jnp.mean(x32 * x32, axis=-1, keepdims=True) + EPS)
    return (y * g.astype(jnp.float32)).astype(x.dtype)


def rel_bucket(dist):
    n = jnp.maximum(dist, 0)
    max_exact = NUM_BUCKETS // 2
    n_f = jnp.maximum(n, 1).astype(jnp.float32)
    large = max_exact + (jnp.log(n_f / max_exact) / math.log(REL_MAX_DISTANCE / max_exact)
                         * (NUM_BUCKETS - max_exact)).astype(jnp.int32)
    large = jnp.minimum(large, NUM_BUCKETS - 1)
    return jnp.where(n < max_exact, n, large)


def s5_ssm(u, lam_re, lam_im, log_dt, b_re, b_im, c_re, c_im, d):
    B, L, _ = u.shape
    f32 = jnp.float32
    u32 = u.astype(f32).reshape(B, L, SSM_GROUPS, SSM_GROUP)
    lre, lim = lam_re.astype(f32), lam_im.astype(f32)
    dt = jnp.exp(log_dt.astype(f32))[:, None]
    mag = jnp.exp(lre * dt)
    abar_re, abar_im = mag * jnp.cos(lim * dt), mag * jnp.sin(lim * dt)
    den = lre * lre + lim * lim
    nr, ni = abar_re - 1.0, abar_im
    f_re = (nr * lre + ni * lim) / den
    f_im = (ni * lre - nr * lim) / den
    br, bi = b_re.astype(f32), b_im.astype(f32)
    bbar_re = f_re[..., None] * br - f_im[..., None] * bi
    bbar_im = f_re[..., None] * bi + f_im[..., None] * br
    bu_re = jnp.einsum('blgh,gph->blgp', u32, bbar_re)
    bu_im = jnp.einsum('blgh,gph->blgp', u32, bbar_im)
    a_re = jnp.broadcast_to(abar_re, (L, SSM_GROUPS, SSM_STATE))
    a_im = jnp.broadcast_to(abar_im, (L, SSM_GROUPS, SSM_STATE))

    def combine(e1, e2):
        a1r, a1i, b1r, b1i = e1
        a2r, a2i, b2r, b2i = e2
        return (a1r * a2r - a1i * a2i,
                a1r * a2i + a1i * a2r,
                a2r * b1r - a2i * b1i + b2r,
                a2r * b1i + a2i * b1r + b2i)

    def scan_one(br_seq, bi_seq):
        _, _, xr, xi = lax.associative_scan(combine, (a_re, a_im, br_seq, bi_seq), axis=0)
        return xr, xi

    xr, xi = jax.vmap(scan_one)(bu_re, bu_im)
    y = (jnp.einsum('blgp,ghp->blgh', xr, c_re.astype(f32))
         - jnp.einsum('blgp,ghp->blgh', xi, c_im.astype(f32))
         + d.astype(f32).reshape(SSM_GROUPS, SSM_GROUP) * u32)
    return y.reshape(B, L, D_SSM).astype(u.dtype)


def dilated_window_attention(q, k, v, bias_tab, window, dilation):
    B, L, H, hd = q.shape
    r = dilation
    span = window // dilation
    unit = r * ATTN_BLOCK
    Lp = -(-L // unit) * unit
    M = Lp // r
    nb = M // ATTN_BLOCK
    pad = ((0, 0), (0, Lp - L), (0, 0), (0, 0))

    def to_blocks(a):
        a = jnp.pad(a, pad).reshape(B, M, r, H, hd).transpose(0, 2, 1, 3, 4)
        return a.reshape(B, r, nb, ATTN_BLOCK, H, hd)

    def with_prev(a):
        prev = jnp.pad(a[:, :, :-1], ((0, 0), (0, 0), (1, 0), (0, 0), (0, 0), (0, 0)))
        return jnp.concatenate([prev, a], axis=3)

    qb = to_blocks(q)
    kw = with_prev(to_blocks(k))
    vw = with_prev(to_blocks(v))
    s = jnp.einsum('brnqhd,brnkhd->brnhqk', qb, kw).astype(jnp.float32) * (hd ** -0.5)
    qi = jnp.arange(ATTN_BLOCK)[:, None]
    kj = jnp.arange(2 * ATTN_BLOCK)[None, :]
    delta = ATTN_BLOCK + qi - kj
    band = (delta >= 0) & (delta <= span)
    has_prev = (jnp.arange(nb) > 0)[:, None, None] | (kj >= ATTN_BLOCK)[None]
    valid = band[None] & has_prev
    bias = bias_tab[rel_bucket(jnp.maximum(delta, 0) * r)]
    s = s + bias.transpose(2, 0, 1).astype(jnp.float32)
    s = jnp.where(valid[:, None], s, NEG_INF)
    m = jnp.max(s, axis=-1, keepdims=True)
    p = jnp.exp(s - m)
    l = jnp.sum(p, axis=-1, keepdims=True)
    o = jnp.einsum('brnhqk,brnkhd->brnqhd', (p / l).astype(v.dtype), vw)
    lse = (m + jnp.log(l))[..., 0]
    o = o.reshape(B, r, M, H, hd).transpose(0, 2, 1, 3, 4).reshape(B, Lp, H, hd)[:, :L]
    lse = lse.transpose(0, 1, 2, 4, 3).reshape(B, r, M, H).transpose(0, 2, 1, 3).reshape(B, Lp, H)[:, :L]
    return o, lse


def setup_inputs(seed: int = 0) -> dict:
    key = jax.random.key(seed)
    ks = jax.random.split(key, 32)
    f32 = jnp.float32

    def nrm(k, shape, scale):
        return jax.random.normal(k, shape, f32) * scale

    G, P, Hg = SSM_GROUPS, SSM_STATE, SSM_GROUP
    return {
        "x": nrm(ks[0], (BATCH, SEQ, D_MODEL), 1.0),
        "mem": nrm(ks[1], (BATCH, MEM_LEN, D_MODEL), 1.0),
        "norm_g": 1.0 + nrm(ks[2], (DEPTH, D_MODEL), 0.1),
        "mem_norm_g": 1.0 + nrm(ks[3], (DEPTH, D_MODEL), 0.1),
        "w_in": nrm(ks[4], (DEPTH, D_MODEL, D_IN), D_MODEL ** -0.5),
        "b_gate": nrm(ks[5], (DEPTH, N_BRANCHES * D_MODEL), 0.1),
        "ssm_lambda_re": -0.5 + nrm(ks[6], (DEPTH, G, P), 0.01),
        "ssm_lambda_im": math.pi * jnp.arange(P, dtype=f32) + nrm(ks[7], (DEPTH, G, P), 0.01),
        "ssm_log_dt": jax.random.uniform(ks[8], (DEPTH, G), f32, math.log(DT_MIN), math.log(DT_MAX)),
        "ssm_b_re": nrm(ks[9], (DEPTH, G, P, Hg), (0.5 / Hg) ** 0.5),
        "ssm_b_im": nrm(ks[10], (DEPTH, G, P, Hg), (0.5 / Hg) ** 0.5),
        "ssm_c_re": nrm(ks[11], (DEPTH, G, Hg, P), (0.5 / P) ** 0.5),
        "ssm_c_im": nrm(ks[12], (DEPTH, G, Hg, P), (0.5 / P) ** 0.5),
        "ssm_d": nrm(ks[13], (DEPTH, D_SSM), 0.5),
        "w_glu": nrm(ks[14], (DEPTH, D_SSM, D_SSM), D_SSM ** -0.5),
        "b_glu": nrm(ks[15], (DEPTH, D_SSM), 0.1),
        "w_mem_kv": nrm(ks[16], (DEPTH, D_MODEL, 2 * D_MEM), D_MODEL ** -0.5),
        "w_br_ssm": nrm(ks[17], (DEPTH, D_SSM, D_MODEL), D_SSM ** -0.5),
        "w_br_attn": nrm(ks[18], (DEPTH, D_ATTN, D_MODEL), D_ATTN ** -0.5),
        "w_br_mem": nrm(ks[19], (DEPTH, D_MEM, D_MODEL), D_MEM ** -0.5),
        "w_out": nrm(ks[20], (DEPTH, D_MODEL, D_MODEL), D_MODEL ** -0.5),
        "rel_bias": nrm(ks[21], (NUM_BUCKETS, N_ATTN_HEADS), 0.5),
        "final_norm_g": 1.0 + nrm(ks[22], (D_MODEL,), 0.1),
    }


def reference(x, mem, norm_g, mem_norm_g, w_in, b_gate, ssm_lambda_re, ssm_lambda_im, ssm_log_dt,
              ssm_b_re, ssm_b_im, ssm_c_re, ssm_c_im, ssm_d, w_glu, b_glu, w_mem_kv, w_br_ssm,
              w_br_attn, w_br_mem, w_out, rel_bias, final_norm_g):
    B, L, _ = x.shape
    sizes = (D_SSM, D_SSM, D_ATTN, D_ATTN, D_ATTN, D_ATTN, D_MEM, D_MEM, N_BRANCHES * D_MODEL)
    split_at = np.cumsum(sizes)[:-1].tolist()
    for layer in range(DEPTH):
        h = rms_norm(x, norm_g[layer])
        proj = h @ w_in[layer]
        u_ssm, z_ssm, q, k, v, z_attn, q_mem, z_mem, gate_logits = jnp.split(proj, split_at, axis=-1)

        y = s5_ssm(u_ssm, ssm_lambda_re[layer], ssm_lambda_im[layer], ssm_log_dt[layer],
                   ssm_b_re[layer], ssm_b_im[layer], ssm_c_re[layer], ssm_c_im[layer], ssm_d[layer])
        y = jax.nn.gelu(y)
        y = y * jax.nn.sigmoid(y @ w_glu[layer] + b_glu[layer])
        o_ssm = y * jax.nn.silu(z_ssm)

        qh = q.reshape(B, L, N_ATTN_HEADS, ATTN_HEAD_DIM)
        kh = k.reshape(B, L, N_ATTN_HEADS, ATTN_HEAD_DIM)
        vh = v.reshape(B, L, N_ATTN_HEADS, ATTN_HEAD_DIM)
        outs, lses = [], []
        for g, (window, dilation) in enumerate(ATTN_CONFIGS):
            sl = slice(g * ATTN_HEADS_PER_GROUP, (g + 1) * ATTN_HEADS_PER_GROUP)
            o_g, lse_g = dilated_window_attention(qh[:, :, sl], kh[:, :, sl], vh[:, :, sl],
                                                  rel_bias[:, sl], window, dilation)
            outs.append(o_g)
            lses.append(lse_g)
        alpha = jax.nn.softmax(jnp.stack(lses, axis=0), axis=0)
        o_attn = jnp.concatenate([o_g * alpha[g][..., None].astype(o_g.dtype) for g, o_g in enumerate(outs)],
                                 axis=2).reshape(B, L, D_ATTN)
        o_attn = o_attn * jax.nn.silu(z_attn)

        kv_mem = rms_norm(mem, mem_norm_g[layer]) @ w_mem_kv[layer]
        k_mem, v_mem = jnp.split(kv_mem, 2, axis=-1)
        k_mem = k_mem.reshape(B, MEM_LEN, MEM_HEADS, MEM_HEAD_DIM)
        v_mem = v_mem.reshape(B, MEM_LEN, MEM_HEADS, MEM_HEAD_DIM)
        qm = q_mem.reshape(B, L, MEM_HEADS, MEM_HEAD_DIM)
        s_mem = jnp.einsum('blhd,bmhd->bhlm', qm, k_mem).astype(jnp.float32) * (MEM_HEAD_DIM ** -0.5)
        p_mem = jax.nn.softmax(s_mem, axis=-1).astype(v_mem.dtype)
        o_mem = jnp.einsum('bhlm,bmhd->blhd', p_mem, v_mem).reshape(B, L, D_MEM)
        o_mem = o_mem * jax.nn.silu(z_mem)

        gates = jax.nn.sigmoid((gate_logits + b_gate[layer]).astype(jnp.float32))
        gates = gates.reshape(B, L, N_BRANCHES, D_MODEL).astype(x.dtype)
        merged = (gates[:, :, 0] * (o_ssm @ w_br_ssm[layer])
                  + gates[:, :, 1] * (o_attn @ w_br_attn[layer])
                  + gates[:, :, 2] * (o_mem @ w_br_mem[layer]))
        x = x + merged @ w_out[layer]
    return rms_norm(x, final_norm_g)
```

```python
import functools
import math

import jax
import jax.numpy as jnp
import numpy as np
from jax import lax
from jax.experimental import pallas as pl
from jax.experimental.pallas import tpu as pltpu

F32 = jnp.float32
BF16 = jnp.bfloat16

D_MODEL = 1024
DEPTH = 2
EPS = 1e-6
N_BRANCHES = 3
D_SSM = 768
SSM_GROUP = 16
SSM_GROUPS = 48
SSM_STATE = 64
ATTN_HEAD_DIM = 64
ATTN_HEADS_PER_GROUP = 4
ATTN_CONFIGS = ((128, 1), (512, 4), (2048, 16))
N_ATTN_HEADS = 12
D_ATTN = 768
ATTN_BLOCK = 128
NUM_BUCKETS = 32
REL_MAX_DISTANCE = 2048
NEG_INF = -1e30
MEM_HEADS = 4
MEM_HEAD_DIM = 128
D_MEM = 512
D_GROUP = ATTN_HEADS_PER_GROUP * ATTN_HEAD_DIM

LANES = 128
SUBLANES = 8
SSM_CHUNK = SUBLANES
SSM_TILE_GROUPS = LANES // SSM_GROUP
SSM_TILES = D_SSM // LANES
SSM_TILE_W = SSM_CHUNK * LANES
SSM_HALF = SSM_TILE_GROUPS * SSM_STATE
VMEM_LIMIT = 56 * 1024 * 1024

TM_INPROJ = 512
TM_MERGE = 256
TQ_ATTN = 512


def _rms(x, g):
    return x * lax.rsqrt(jnp.mean(x * x, axis=-1, keepdims=True) + EPS) * g


def _const_spec(shape):
    n = len(shape)
    return pl.BlockSpec(shape, lambda *_: (0,) * n, pipeline_mode=pl.Buffered(1))


def _memkv_kernel(mem_ref, g_ref, w_ref, k_ref, v_ref):
    h = _rms(mem_ref[0], g_ref[...]).astype(BF16)
    k_ref[0] = jnp.dot(h, w_ref[:, :D_MEM], preferred_element_type=F32).astype(BF16)
    v_ref[0] = jnp.dot(h, w_ref[:, D_MEM:], preferred_element_type=F32).astype(BF16)


def _mem_kv(mem, g, w_bf16):
    B, ML, _ = mem.shape
    return pl.pallas_call(
        _memkv_kernel,
        grid=(B,),
        in_specs=[pl.BlockSpec((1, ML, D_MODEL), lambda b: (b, 0, 0)),
                  _const_spec((1, D_MODEL)),
                  _const_spec((D_MODEL, 2 * D_MEM))],
        out_specs=[pl.BlockSpec((1, ML, D_MEM), lambda b: (b, 0, 0)),
                   pl.BlockSpec((1, ML, D_MEM), lambda b: (b, 0, 0))],
        out_shape=[jax.ShapeDtypeStruct((B, ML, D_MEM), BF16)] * 2,
        name="mem_kv",
    )(mem, g, w_bf16)


_IN_COLS = (D_SSM, D_ATTN, D_ATTN, D_ATTN, D_MEM)


def _inproj_kernel(x_ref, g_ref, w_ref, u_ref, q_ref, k_ref, v_ref, qm_ref):
    h = _rms(x_ref[...], g_ref[...]).astype(BF16)
    outs = (u_ref, q_ref, k_ref, v_ref, qm_ref)
    scales = (None, ATTN_HEAD_DIM ** -0.5, None, None, None)
    off = 0
    for o_ref, width, sc in zip(outs, _IN_COLS, scales):
        p = jnp.dot(h, w_ref[:, off:off + width], preferred_element_type=F32)
        if sc is not None:
            p = p * sc
        o_ref[...] = p.astype(BF16)
        off += width


def _in_proj(x2, g, w1):
    T = x2.shape[0]
    tm = TM_INPROJ
    return pl.pallas_call(
        _inproj_kernel,
        grid=(T // tm,),
        in_specs=[pl.BlockSpec((tm, D_MODEL), lambda i: (i, 0)),
                  _const_spec((1, D_MODEL)),
                  _const_spec((D_MODEL, sum(_IN_COLS)))],
        out_specs=[pl.BlockSpec((tm, w), lambda i: (i, 0)) for w in _IN_COLS],
        out_shape=[jax.ShapeDtypeStruct((T, w), BF16) for w in _IN_COLS],
        compiler_params=pltpu.CompilerParams(vmem_limit_bytes=VMEM_LIMIT),
        name="in_proj",
    )(x2, g, w1)


def _cmul(ar, ai, br, bi):
    return ar * br - ai * bi, ar * bi + ai * br


def _ssm_prep(lre, lim, log_dt, b_re, b_im, c_re, c_im, d):
    hp = lax.Precision.HIGHEST
    G, P, H, C = SSM_GROUPS, SSM_STATE, SSM_GROUP, SSM_CHUNK
    dt = jnp.exp(log_dt)[:, None]
    mag = jnp.exp(lre * dt)
    ar, ai = mag * jnp.cos(lim * dt), mag * jnp.sin(lim * dt)
    den = lre * lre + lim * lim
    nr, ni = ar - 1.0, ai
    fr = (nr * lre + ni * lim) / den
    fi = (ni * lre - nr * lim) / den
    bbr = fr[..., None] * b_re - fi[..., None] * b_im
    bbi = fr[..., None] * b_im + fi[..., None] * b_re
    prs, pis = [jnp.ones_like(ar)], [jnp.zeros_like(ai)]
    for _ in range(C):
        r_, i_ = _cmul(prs[-1], pis[-1], ar, ai)
        prs.append(r_)
        pis.append(i_)
    PR, PI = jnp.stack(prs), jnp.stack(pis)
    wr = PR[:C, :, :, None] * bbr - PI[:C, :, :, None] * bbi
    wi = PR[:C, :, :, None] * bbi + PI[:C, :, :, None] * bbr
    kk = (jnp.einsum('ghp,tgpk->tghk', c_re, wr, precision=hp)
          - jnp.einsum('ghp,tgpk->tghk', c_im, wi, precision=hp))
    kk = kk.at[0].add(jnp.eye(H, dtype=F32)[None] * d.reshape(G, H)[:, :, None])
    s_idx = jnp.arange(C)[:, None]
    t_idx = jnp.arange(C)[None, :]
    lag = t_idx - s_idx
    kst = jnp.where((lag >= 0)[:, :, None, None, None], kk[jnp.clip(lag, 0, C - 1)], 0.0)
    eye = jnp.eye(SSM_TILE_GROUPS, dtype=F32)
    J, GL = SSM_TILES, SSM_TILE_GROUPS
    kst = kst.reshape(C, C, J, GL, H, H)
    m_mat = jnp.einsum('ab,stjbhk->jsaktbh', eye, kst).reshape(J, SSM_TILE_W, SSM_TILE_W)
    sw = jnp.stack([wr[::-1], wi[::-1]]).reshape(2, C, J, GL, P, H)
    s_mat = jnp.einsum('ab,rsjbpk->jsakrbp', eye, sw).reshape(J, SSM_TILE_W, 2 * SSM_HALF)
    cr = c_re[None] * PR[1:, :, None, :] - c_im[None] * PI[1:, :, None, :]
    ci = c_re[None] * PI[1:, :, None, :] + c_im[None] * PR[1:, :, None, :]
    rw = jnp.stack([cr, -ci]).reshape(2, C, J, GL, H, P)
    r_mat = jnp.einsum('ab,rtjbhp->jraptbh', eye, rw).reshape(J, 2 * SSM_HALF, SSM_TILE_W)
    alr, ali = PR[C], PI[C]
    qrs, qis = [jnp.ones_like(alr)], [jnp.zeros_like(ali)]
    for _ in range(SUBLANES):
        r_, i_ = _cmul(qrs[-1], qis[-1], alr, ali)
        qrs.append(r_)
        qis.append(i_)

    def lay(zr, zi):
        zr = zr.reshape(zr.shape[:-2] + (J, GL * P))
        zi = zi.reshape(zi.shape[:-2] + (J, GL * P))
        return jnp.concatenate([zr, zi], axis=-1)

    rows = jnp.arange(SUBLANES)[:, None, None]
    tabs = []
    for dsh in (1, 2, 4):
        full = jnp.broadcast_to(lay(qrs[dsh], qis[dsh])[None], (SUBLANES, J, 2 * SSM_HALF))
        tabs.append(jnp.where(rows >= dsh, full, 0.0))
    tabs.append(jnp.stack([lay(qrs[i], qis[i]) for i in range(SUBLANES)]))
    tabs.append(jnp.broadcast_to(lay(qrs[SUBLANES], qis[SUBLANES])[None], (SUBLANES, J, 2 * SSM_HALF)))
    tab = jnp.stack(tabs).transpose(2, 0, 1, 3)
    return m_mat.astype(BF16), s_mat.astype(BF16), r_mat.astype(BF16), tab.astype(F32)


def _ssm_kernel(u_ref, m_ref, s_ref, r_ref, tab_ref, y_ref, upd_ref, xin_ref):
    hw = SSM_HALF
    u = u_ref[...]
    upd_ref[...] = jnp.dot(u, s_ref[0], preferred_element_type=F32)
    n_blocks = u.shape[0] // SUBLANES
    row = lax.broadcasted_iota(jnp.int32, (SUBLANES, hw), 0)

    def body(i, carry):
        er, ei = carry
        r0 = pl.multiple_of(i * SUBLANES, SUBLANES)
        vr = upd_ref[pl.ds(r0, SUBLANES), 0:hw]
        vi = upd_ref[pl.ds(r0, SUBLANES), hw:2 * hw]
        for lvl, dsh in enumerate((1, 2, 4)):
            cr, ci = tab_ref[0, lvl, :, 0:hw], tab_ref[0, lvl, :, hw:2 * hw]
            sr, si = pltpu.roll(vr, dsh, 0), pltpu.roll(vi, dsh, 0)
            vr, vi = vr + (cr * sr - ci * si), vi + (cr * si + ci * sr)
        sr = jnp.where(row == 0, 0.0, pltpu.roll(vr, 1, 0))
        si = jnp.where(row == 0, 0.0, pltpu.roll(vi, 1, 0))
        pr, pi_ = tab_ref[0, 3, :, 0:hw], tab_ref[0, 3, :, hw:2 * hw]
        xin_ref[pl.ds(r0, SUBLANES), 0:hw] = sr + (pr * er - pi_ * ei)
        xin_ref[pl.ds(r0, SUBLANES), hw:2 * hw] = si + (pr * ei + pi_ * er)
        a8r, a8i = tab_ref[0, 4, :, 0:hw], tab_ref[0, 4, :, hw:2 * hw]
        lr = jnp.broadcast_to(vr[SUBLANES - 1:SUBLANES, :], (SUBLANES, hw))
        li = jnp.broadcast_to(vi[SUBLANES - 1:SUBLANES, :], (SUBLANES, hw))
        return a8r * er - a8i * ei + lr, a8r * ei + a8i * er + li

    zero = jnp.zeros((SUBLANES, hw), F32)
    lax.fori_loop(0, n_blocks, body, (zero, zero))
    y = jnp.dot(u, m_ref[0], preferred_element_type=F32)
    y = y + jnp.dot(xin_ref[...].astype(BF16), r_ref[0], preferred_element_type=F32)
    y_ref[...] = y


def _ssm(u2, m_mat, s_mat, r_mat, tab, batch):
    rows = u2.shape[0] // batch
    w = SSM_TILE_W
    mat_spec = pl.BlockSpec((1, w, w), lambda j, b: (j, 0, 0))
    return pl.pallas_call(
        _ssm_kernel,
        grid=(SSM_TILES, batch),
        in_specs=[pl.BlockSpec((rows, w), lambda j, b: (b, j)),
                  mat_spec, mat_spec, mat_spec,
                  pl.BlockSpec((1, 5, SUBLANES, w), lambda j, b: (j, 0, 0, 0))],
        out_specs=pl.BlockSpec((rows, w), lambda j, b: (b, j)),
        out_shape=jax.ShapeDtypeStruct(u2.shape, F32),
        scratch_shapes=[pltpu.VMEM((rows, w), F32), pltpu.VMEM((rows, w), F32)],
        compiler_params=pltpu.CompilerParams(vmem_limit_bytes=VMEM_LIMIT),
        name="ssm",
    )(u2, m_mat, s_mat, r_mat, tab)


def _rel_bucket(dist):
    n = jnp.maximum(dist, 0)
    max_exact = NUM_BUCKETS // 2
    n_f = jnp.maximum(n, 1).astype(F32)
    large = max_exact + (jnp.log(n_f / max_exact) / math.log(REL_MAX_DISTANCE / max_exact)
                         * (NUM_BUCKETS - max_exact)).astype(jnp.int32)
    large = jnp.minimum(large, NUM_BUCKETS - 1)
    return jnp.where(n < max_exact, n, large)


def _bias_mask(rel_bias_g, window, dilation):
    span = window // dilation
    qi = jnp.arange(ATTN_BLOCK)[:, None]
    kj = jnp.arange(2 * ATTN_BLOCK)[None, :]
    delta = ATTN_BLOCK + qi - kj
    band = (delta >= 0) & (delta <= span)
    bias = rel_bias_g[_rel_bucket(jnp.maximum(delta, 0) * dilation)]
    return jnp.where(band[None], bias.transpose(2, 0, 1).astype(F32), NEG_INF)


def _attn_kernel(q_ref, kc_ref, kp_ref, vc_ref, vp_ref, bm_ref, o_ref, lse_ref):
    blk = ATTN_BLOCK
    first_valid_col = jnp.where(pl.program_id(1) == 0, blk, 0)
    lane = lax.broadcasted_iota(jnp.int32, (blk, LANES), 1)
    low = lane < ATTN_HEAD_DIM
    col = lax.broadcasted_iota(jnp.int32, (blk, 2 * blk), 1)
    lane4 = lax.broadcasted_iota(jnp.int32, (blk, ATTN_HEADS_PER_GROUP), 1)
    n_sub = q_ref.shape[1] // blk
    dn = (((1,), (1,)), ((), ()))
    for n in range(n_sub):
        rows = slice(n * blk, (n + 1) * blk)
        lse4 = jnp.zeros((blk, ATTN_HEADS_PER_GROUP), F32)
        for pair in range(ATTN_HEADS_PER_GROUP // 2):
            cols = slice(pair * LANES, (pair + 1) * LANES)
            q32 = q_ref[0, rows, cols].astype(F32)
            if n == 0:
                kk = jnp.concatenate([kp_ref[0, :, cols], kc_ref[0, rows, cols]], axis=0)
                vv = jnp.concatenate([vp_ref[0, :, cols], vc_ref[0, rows, cols]], axis=0)
            else:
                kk = kc_ref[0, (n - 1) * blk:(n + 1) * blk, cols]
                vv = vc_ref[0, (n - 1) * blk:(n + 1) * blk, cols]
            outs = []
            for sub in range(2):
                hh = 2 * pair + sub
                qh = jnp.where(low if sub == 0 else ~low, q32, 0.0).astype(BF16)
                s = lax.dot_general(qh, kk, dn, preferred_element_type=F32) + bm_ref[hh]
                if n == 0:
                    s = jnp.where(col >= first_valid_col, s, NEG_INF)
                m = jnp.max(s, axis=-1, keepdims=True)
                p = jnp.exp(s - m)
                l = jnp.sum(p, axis=-1, keepdims=True)
                o = jnp.dot(p.astype(BF16), vv, preferred_element_type=F32) * (1.0 / l)
                outs.append(o)
                lse4 = jnp.where(lane4 == hh, m + jnp.log(l), lse4)
            o_ref[0, rows, cols] = jnp.where(low, outs[0], outs[1]).astype(BF16)
        lse_ref[0, rows, :] = lse4


def _attention(q, k, v, bias_mask, col_block):
    ns, m_len, _ = q.shape
    tq = min(TQ_ATTN, m_len)
    per = tq // ATTN_BLOCK
    cur = pl.BlockSpec((1, tq, D_GROUP), lambda s, i: (s, i, col_block))
    prev = pl.BlockSpec((1, ATTN_BLOCK, D_GROUP), lambda s, i: (s, jnp.maximum(i * per - 1, 0), col_block))
    return pl.pallas_call(
        _attn_kernel,
        grid=(ns, m_len // tq),
        in_specs=[cur, cur, prev, cur, prev,
                  _const_spec((ATTN_HEADS_PER_GROUP, ATTN_BLOCK, 2 * ATTN_BLOCK))],
        out_specs=[pl.BlockSpec((1, tq, D_GROUP), lambda s, i: (s, i, 0)),
                   pl.BlockSpec((1, tq, ATTN_HEADS_PER_GROUP), lambda s, i: (s, i, 0))],
        out_shape=[jax.ShapeDtypeStruct((ns, m_len, D_GROUP), BF16),
                   jax.ShapeDtypeStruct((ns, m_len, ATTN_HEADS_PER_GROUP), F32)],
        name="attn",
    )(q, k, k, v, v, bias_mask)


_Z_COLS = (D_SSM, D_ATTN, D_MEM, N_BRANCHES * D_MODEL)


def _merge_kernel(final, x_ref, y_ref, o0_ref, o1_ref, o2_ref, l0_ref, l1_ref, l2_ref, qm_ref,
                  km_ref, vm_ref, g_ref, wz_ref, bg_ref, wglu_ref, bglu_ref, wbs_ref, wba_ref,
                  wbm_ref, wout_ref, fg_ref, out_ref):
    x = x_ref[...]
    tm = x.shape[0]
    h = _rms(x, g_ref[...]).astype(BF16)

    def zdot(lo, hi):
        return jnp.dot(h, wz_ref[:, lo:hi], preferred_element_type=F32)

    yg = jax.nn.gelu(y_ref[...])
    t = jnp.dot(yg.astype(BF16), wglu_ref[...], preferred_element_type=F32) + bglu_ref[...]
    o_ssm = yg * jax.nn.sigmoid(t) * jax.nn.silu(zdot(0, D_SSM))
    p_ssm = jnp.dot(o_ssm.astype(BF16), wbs_ref[...], preferred_element_type=F32)

    ls = (l0_ref[...], l1_ref[...], l2_ref[...])
    mx = jnp.maximum(jnp.maximum(ls[0], ls[1]), ls[2])
    es = [jnp.exp(l - mx) for l in ls]
    inv = 1.0 / (es[0] + es[1] + es[2])
    head_of_lane = lax.broadcasted_iota(jnp.int32, (tm, D_GROUP), 1) // ATTN_HEAD_DIM
    parts = []
    for e, o_ref in zip(es, (o0_ref, o1_ref, o2_ref)):
        alpha = e * inv
        wide = jnp.zeros((tm, D_GROUP), F32)
        for j in range(ATTN_HEADS_PER_GROUP):
            wide = jnp.where(head_of_lane == j, alpha[:, j:j + 1], wide)
        parts.append(o_ref[...].astype(F32) * wide)
    o_attn = jnp.concatenate(parts, axis=-1) * jax.nn.silu(zdot(D_SSM, D_SSM + D_ATTN))
    p_attn = jnp.dot(o_attn.astype(BF16), wba_ref[...], preferred_element_type=F32)

    dn = (((1,), (1,)), ((), ()))
    heads = []
    for hd in range(MEM_HEADS):
        cols = slice(hd * MEM_HEAD_DIM, (hd + 1) * MEM_HEAD_DIM)
        s = lax.dot_general(qm_ref[:, cols], km_ref[0, :, cols], dn, preferred_element_type=F32)
        s = s * (MEM_HEAD_DIM ** -0.5)
        m = jnp.max(s, axis=-1, keepdims=True)
        p = jnp.exp(s - m)
        l = jnp.sum(p, axis=-1, keepdims=True)
        heads.append(jnp.dot(p.astype(BF16), vm_ref[0, :, cols], preferred_element_type=F32) * (1.0 / l))
    z_lo = D_SSM + D_ATTN
    o_mem = jnp.concatenate(heads, axis=-1) * jax.nn.silu(zdot(z_lo, z_lo + D_MEM))
    p_mem = jnp.dot(o_mem.astype(BF16), wbm_ref[...], preferred_element_type=F32)

    g_lo = z_lo + D_MEM
    merged = jnp.zeros((tm, D_MODEL), F32)
    for br, p_br in enumerate((p_ssm, p_attn, p_mem)):
        lo = g_lo + br * D_MODEL
        gate = jax.nn.sigmoid(zdot(lo, lo + D_MODEL) + bg_ref[:, br * D_MODEL:(br + 1) * D_MODEL])
        merged = merged + gate * p_br
    xn = x + jnp.dot(merged.astype(BF16), wout_ref[...], preferred_element_type=F32)
    if final:
        xn = _rms(xn, fg_ref[...])
    out_ref[...] = xn


def _merge(final, x2, y, o_groups, lse_groups, qm, k_mem, v_mem, g, wz, bg, wglu, bglu, wbs, wba, wbm,
           wout, fg, seq_len):
    T = x2.shape[0]
    tm = TM_MERGE
    per_seq = seq_len // tm

    def rows(w):
        return pl.BlockSpec((tm, w), lambda i: (i, 0))

    mem_spec = pl.BlockSpec((1,) + k_mem.shape[1:], lambda i: (i // per_seq, 0, 0))
    in_specs = ([rows(D_MODEL), rows(D_SSM)] + [rows(D_GROUP)] * 3 + [rows(ATTN_HEADS_PER_GROUP)] * 3
                + [rows(D_MEM), mem_spec, mem_spec]
                + [_const_spec(a.shape) for a in (g, wz, bg, wglu, bglu, wbs, wba, wbm, wout, fg)])
    return pl.pallas_call(
        functools.partial(_merge_kernel, final),
        grid=(T // tm,),
        in_specs=in_specs,
        out_specs=rows(D_MODEL),
        out_shape=jax.ShapeDtypeStruct((T, D_MODEL), F32),
        compiler_params=pltpu.CompilerParams(vmem_limit_bytes=VMEM_LIMIT),
        name="merge",
    )(x2, y, *o_groups, *lse_groups, qm, k_mem, v_mem, g, wz, bg, wglu, bglu, wbs, wba, wbm, wout, fg)


def kernel(x, mem, norm_g, mem_norm_g, w_in, b_gate, ssm_lambda_re, ssm_lambda_im, ssm_log_dt, ssm_b_re,
           ssm_b_im, ssm_c_re, ssm_c_im, ssm_d, w_glu, b_glu, w_mem_kv, w_br_ssm, w_br_attn, w_br_mem,
           w_out, rel_bias, final_norm_g):
    B, L, _ = x.shape
    T = B * L
    assert L % (ATTN_CONFIGS[-1][1] * ATTN_BLOCK) == 0 and L % TM_INPROJ == 0 and L % TM_MERGE == 0
    sizes = (D_SSM, D_SSM, D_ATTN, D_ATTN, D_ATTN, D_ATTN, D_MEM, D_MEM, N_BRANCHES * D_MODEL)
    offs = np.concatenate([[0], np.cumsum(sizes)])
    seg = lambda w, i: w[:, offs[i]:offs[i + 1]]
    bias_masks = [_bias_mask(rel_bias[:, gi * ATTN_HEADS_PER_GROUP:(gi + 1) * ATTN_HEADS_PER_GROUP], win, dil)
                  for gi, (win, dil) in enumerate(ATTN_CONFIGS)]
    fg = final_norm_g.reshape(1, D_MODEL)
    x2 = x.reshape(T, D_MODEL)
    n_chunks = T // SSM_CHUNK
    for layer in range(DEPTH):
        wl = w_in[layer]
        w1 = jnp.concatenate([seg(wl, i) for i in (0, 2, 3, 4, 6)], axis=1).astype(BF16)
        wz = jnp.concatenate([seg(wl, i) for i in (1, 5, 7, 8)], axis=1).astype(BF16)
        g = norm_g[layer].reshape(1, D_MODEL)
        k_mem, v_mem = _mem_kv(mem, mem_norm_g[layer].reshape(1, D_MODEL), w_mem_kv[layer].astype(BF16))
        u, q, k, v, qm = _in_proj(x2, g, w1)

        u2 = u.reshape(n_chunks, SSM_CHUNK, SSM_TILES, LANES).transpose(0, 2, 1, 3).reshape(n_chunks, D_SSM * SSM_CHUNK)
        mats = _ssm_prep(ssm_lambda_re[layer], ssm_lambda_im[layer], ssm_log_dt[layer], ssm_b_re[layer],
                         ssm_b_im[layer], ssm_c_re[layer], ssm_c_im[layer], ssm_d[layer])
        y2 = _ssm(u2, *mats, batch=B)
        y = y2.reshape(n_chunks, SSM_TILES, SSM_CHUNK, LANES).transpose(0, 2, 1, 3).reshape(T, D_SSM)

        o_groups, lse_groups = [], []
        for gi, (_, r) in enumerate(ATTN_CONFIGS):
            m_len = L // r
            if r == 1:
                args = [a.reshape(B, L, D_ATTN) for a in (q, k, v)]
                o_g, lse_g = _attention(*args, bias_masks[gi], gi)
            else:
                def dec(a):
                    a = a[:, gi * D_GROUP:(gi + 1) * D_GROUP].reshape(B, m_len, r, D_GROUP)
                    return a.transpose(0, 2, 1, 3).reshape(B * r, m_len, D_GROUP)
                o_g, lse_g = _attention(dec(q), dec(k), dec(v), bias_masks[gi], 0)
                o_g = o_g.reshape(B, r, m_len, D_GROUP).transpose(0, 2, 1, 3)
                lse_g = lse_g.reshape(B, r, m_len, ATTN_HEADS_PER_GROUP).transpose(0, 2, 1, 3)
            o_groups.append(o_g.reshape(T, D_GROUP))
            lse_groups.append(lse_g.reshape(T, ATTN_HEADS_PER_GROUP))

        x2 = _merge(layer == DEPTH - 1, x2, y, o_groups, lse_groups, qm, k_mem, v_mem, g, wz,
                    b_gate[layer].reshape(1, -1), w_glu[layer].astype(BF16), b_glu[layer].reshape(1, -1),
                    w_br_ssm[layer].astype(BF16), w_br_attn[layer].astype(BF16), w_br_mem[layer].astype(BF16),
                    w_out[layer].astype(BF16), fg, L)
    return x2.reshape(B, L, D_MODEL)
```

```python
import functools
import math

import jax
import jax.numpy as jnp
import numpy as np
from jax import lax
from jax.experimental import pallas as pl
from jax.experimental.pallas import tpu as pltpu

F32 = jnp.float32
BF16 = jnp.bfloat16

D_MODEL = 1024
DEPTH = 2
EPS = 1e-6
N_BRANCHES = 3
D_SSM = 768
SSM_GROUP = 16
SSM_GROUPS = 48
SSM_STATE = 64
ATTN_HEAD_DIM = 64
ATTN_HEADS_PER_GROUP = 4
ATTN_CONFIGS = ((128, 1), (512, 4), (2048, 16))
N_ATTN_HEADS = 12
D_ATTN = 768
ATTN_BLOCK = 128
NUM_BUCKETS = 32
REL_MAX_DISTANCE = 2048
NEG_INF = -1e30
MEM_HEADS = 4
MEM_HEAD_DIM = 128
D_MEM = 512
D_GROUP = ATTN_HEADS_PER_GROUP * ATTN_HEAD_DIM

LANES = 128
SUBLANES = 8
SSM_CHUNK = SUBLANES
SSM_TILE_GROUPS = LANES // SSM_GROUP
SSM_TILES = D_SSM // LANES
SSM_TILE_W = SSM_CHUNK * LANES
SSM_HALF = SSM_TILE_GROUPS * SSM_STATE
VMEM_LIMIT = 56 * 1024 * 1024

TM_INPROJ = 512
TM_MERGE = 256
TQ_ATTN = 512


def _rms(x, g):
    return x * lax.rsqrt(jnp.mean(x * x, axis=-1, keepdims=True) + EPS) * g


def _const_spec(shape):
    n = len(shape)
    return pl.BlockSpec(shape, lambda *_: (0,) * n, pipeline_mode=pl.Buffered(1))


def _memkv_kernel(mem_ref, g_ref, w_ref, k_ref, v_ref):
    h = _rms(mem_ref[0], g_ref[...]).astype(BF16)
    k_ref[0] = jnp.dot(h, w_ref[:, :D_MEM], preferred_element_type=F32).astype(BF16)
    v_ref[0] = jnp.dot(h, w_ref[:, D_MEM:], preferred_element_type=F32).astype(BF16)


def _mem_kv(mem, g, w_bf16):
    B, ML, _ = mem.shape
    return pl.pallas_call(
        _memkv_kernel,
        grid=(B,),
        in_specs=[pl.BlockSpec((1, ML, D_MODEL), lambda b: (b, 0, 0)),
                  _const_spec((1, D_MODEL)),
                  _const_spec((D_MODEL, 2 * D_MEM))],
        out_specs=[pl.BlockSpec((1, ML, D_MEM), lambda b: (b, 0, 0)),
                   pl.BlockSpec((1, ML, D_MEM), lambda b: (b, 0, 0))],
        out_shape=[jax.ShapeDtypeStruct((B, ML, D_MEM), BF16)] * 2,
        name="mem_kv",
    )(mem, g, w_bf16)


_IN_SIZES = (D_SSM, D_SSM, D_ATTN, D_ATTN, D_ATTN, D_ATTN, D_MEM, D_MEM, N_BRANCHES * D_MODEL)
_IN_OFFS = tuple(int(v) for v in np.concatenate([[0], np.cumsum(_IN_SIZES)]))
_IN_NAMES = ("u", "z_ssm", "q", "k", "v", "z_attn", "q_mem", "z_mem", "gates")
GATE_W = 512


def _w_in_spec(name, layer, part=0):
    idx = _IN_NAMES.index(name)
    width = GATE_W if name == "gates" else _IN_SIZES[idx]
    block, rem = divmod(_IN_OFFS[idx], width)
    assert rem == 0
    return pl.BlockSpec((None, D_MODEL, width), lambda *_: (layer, 0, block + part),
                        pipeline_mode=pl.Buffered(1))


def _inproj_kernel(x_ref, g_ref, wu_ref, wq_ref, wk_ref, wv_ref, wqm_ref, u2_ref, *rest):
    qkv_refs, qm_ref, scr = rest[:9], rest[9], rest[10]
    h = _rms(x_ref[0], g_ref[...]).astype(BF16)
    tm = h.shape[0]
    def to_scratch(p):
        for j in range(p.shape[1] // LANES):
            scr[j] = p[:, j * LANES:(j + 1) * LANES]

    to_scratch(jnp.dot(h, wu_ref[...], preferred_element_type=F32))
    for s in range(SSM_CHUNK):
        for j in range(SSM_TILES):
            lo = j * SSM_TILE_W + s * LANES
            u2_ref[0, :, lo:lo + LANES] = scr[j, pl.ds(s, tm // SSM_CHUNK, stride=SSM_CHUNK), :].astype(BF16)
    for idx, (w_ref, scale) in enumerate(((wq_ref, ATTN_HEAD_DIM ** -0.5), (wk_ref, None), (wv_ref, None))):
        p = jnp.dot(h, w_ref[...], preferred_element_type=F32)
        if scale is not None:
            p = p * scale
        to_scratch(p)
        tiles = D_GROUP // LANES
        for gi, (_, r) in enumerate(ATTN_CONFIGS):
            o_ref = qkv_refs[3 * idx + gi]
            for s in range(r):
                for c in range(tiles):
                    piece = scr[gi * tiles + c, pl.ds(s, tm // r, stride=r), :]
                    o_ref[0, s, :, c * LANES:(c + 1) * LANES] = piece.astype(BF16)
    qm_ref[0] = jnp.dot(h, wqm_ref[...], preferred_element_type=F32).astype(BF16)


def _in_proj(x, g, w_in_bf, layer):
    B, L, _ = x.shape
    tm = TM_INPROJ
    out_specs = [pl.BlockSpec((1, tm // SSM_CHUNK, D_SSM * SSM_CHUNK), lambda b, i: (b, i, 0))]
    out_shape = [jax.ShapeDtypeStruct((B, L // SSM_CHUNK, D_SSM * SSM_CHUNK), BF16)]
    for _ in range(3):
        for _, r in ATTN_CONFIGS:
            out_specs.append(pl.BlockSpec((1, r, tm // r, D_GROUP), lambda b, i: (b, 0, i, 0)))
            out_shape.append(jax.ShapeDtypeStruct((B, r, L // r, D_GROUP), BF16))
    out_specs.append(pl.BlockSpec((1, tm, D_MEM), lambda b, i: (b, i, 0)))
    out_shape.append(jax.ShapeDtypeStruct((B, L, D_MEM), BF16))
    return pl.pallas_call(
        _inproj_kernel,
        grid=(B, L // tm),
        in_specs=[pl.BlockSpec((1, tm, D_MODEL), lambda b, i: (b, i, 0)),
                  _const_spec((1, D_MODEL))]
                 + [_w_in_spec(n, layer) for n in ("u", "q", "k", "v", "q_mem")],
        out_specs=out_specs,
        out_shape=out_shape,
        scratch_shapes=[pltpu.VMEM((D_SSM // LANES, tm, LANES), F32)],
        compiler_params=pltpu.CompilerParams(vmem_limit_bytes=VMEM_LIMIT),
        name="in_proj",
    )(x, g, *([w_in_bf] * 5))


def _cmul(ar, ai, br, bi):
    return ar * br - ai * bi, ar * bi + ai * br


def _ssm_prep(lre, lim, log_dt, b_re, b_im, c_re, c_im, d):
    hp = lax.Precision.HIGHEST
    G, P, H, C = SSM_GROUPS, SSM_STATE, SSM_GROUP, SSM_CHUNK
    dt = jnp.exp(log_dt)[:, None]
    mag = jnp.exp(lre * dt)
    ar, ai = mag * jnp.cos(lim * dt), mag * jnp.sin(lim * dt)
    den = lre * lre + lim * lim
    nr, ni = ar - 1.0, ai
    fr = (nr * lre + ni * lim) / den
    fi = (ni * lre - nr * lim) / den
    bbr = fr[..., None] * b_re - fi[..., None] * b_im
    bbi = fr[..., None] * b_im + fi[..., None] * b_re
    prs, pis = [jnp.ones_like(ar)], [jnp.zeros_like(ai)]
    for _ in range(C):
        r_, i_ = _cmul(prs[-1], pis[-1], ar, ai)
        prs.append(r_)
        pis.append(i_)
    PR, PI = jnp.stack(prs), jnp.stack(pis)
    wr = PR[:C, :, :, None] * bbr - PI[:C, :, :, None] * bbi
    wi = PR[:C, :, :, None] * bbi + PI[:C, :, :, None] * bbr
    kk = (jnp.einsum('ghp,tgpk->tghk', c_re, wr, precision=hp)
          - jnp.einsum('ghp,tgpk->tghk', c_im, wi, precision=hp))
    kk = kk.at[0].add(jnp.eye(H, dtype=F32)[None] * d.reshape(G, H)[:, :, None])
    s_idx = jnp.arange(C)[:, None]
    t_idx = jnp.arange(C)[None, :]
    lag = t_idx - s_idx
    kst = jnp.where((lag >= 0)[:, :, None, None, None], kk[jnp.clip(lag, 0, C - 1)], 0.0)
    eye = jnp.eye(SSM_TILE_GROUPS, dtype=F32)
    J, GL = SSM_TILES, SSM_TILE_GROUPS
    kst = kst.reshape(C, C, J, GL, H, H)
    m_mat = jnp.einsum('ab,stjbhk->jsaktbh', eye, kst).reshape(J, SSM_TILE_W, SSM_TILE_W)
    sw = jnp.stack([wr[::-1], wi[::-1]]).reshape(2, C, J, GL, P, H)
    s_mat = jnp.einsum('ab,rsjbpk->jsakrbp', eye, sw).reshape(J, SSM_TILE_W, 2 * SSM_HALF)
    cr = c_re[None] * PR[1:, :, None, :] - c_im[None] * PI[1:, :, None, :]
    ci = c_re[None] * PI[1:, :, None, :] + c_im[None] * PR[1:, :, None, :]
    rw = jnp.stack([cr, -ci]).reshape(2, C, J, GL, H, P)
    r_mat = jnp.einsum('ab,rtjbhp->jraptbh', eye, rw).reshape(J, 2 * SSM_HALF, SSM_TILE_W)
    alr, ali = PR[C], PI[C]
    qrs, qis = [jnp.ones_like(alr)], [jnp.zeros_like(ali)]
    for _ in range(SUBLANES):
        r_, i_ = _cmul(qrs[-1], qis[-1], alr, ali)
        qrs.append(r_)
        qis.append(i_)

    def lay(zr, zi):
        zr = zr.reshape(zr.shape[:-2] + (J, GL * P))
        zi = zi.reshape(zi.shape[:-2] + (J, GL * P))
        return jnp.concatenate([zr, zi], axis=-1)

    rows = jnp.arange(SUBLANES)[:, None, None]
    tabs = []
    for dsh in (1, 2, 4):
        full = jnp.broadcast_to(lay(qrs[dsh], qis[dsh])[None], (SUBLANES, J, 2 * SSM_HALF))
        tabs.append(jnp.where(rows >= dsh, full, 0.0))
    tabs.append(jnp.stack([lay(qrs[i], qis[i]) for i in range(SUBLANES)]))
    tabs.append(jnp.broadcast_to(lay(qrs[SUBLANES], qis[SUBLANES])[None], (SUBLANES, J, 2 * SSM_HALF)))
    tab = jnp.stack(tabs).transpose(2, 0, 1, 3)
    return m_mat.astype(BF16), s_mat.astype(BF16), r_mat.astype(BF16), tab.astype(F32)


def _ssm_kernel(u_ref, m_ref, s_ref, r_ref, tab_ref, y_ref, upd_ref, xin_ref):
    hw = SSM_HALF
    u = u_ref[...]
    upd_ref[...] = jnp.dot(u, s_ref[0], preferred_element_type=F32)
    n_blocks = u.shape[0] // SUBLANES
    row = lax.broadcasted_iota(jnp.int32, (SUBLANES, hw), 0)

    def body(i, carry):
        er, ei = carry
        r0 = pl.multiple_of(i * SUBLANES, SUBLANES)
        vr = upd_ref[pl.ds(r0, SUBLANES), 0:hw]
        vi = upd_ref[pl.ds(r0, SUBLANES), hw:2 * hw]
        for lvl, dsh in enumerate((1, 2, 4)):
            cr, ci = tab_ref[0, lvl, :, 0:hw], tab_ref[0, lvl, :, hw:2 * hw]
            sr, si = pltpu.roll(vr, dsh, 0), pltpu.roll(vi, dsh, 0)
            vr, vi = vr + (cr * sr - ci * si), vi + (cr * si + ci * sr)
        sr = jnp.where(row == 0, 0.0, pltpu.roll(vr, 1, 0))
        si = jnp.where(row == 0, 0.0, pltpu.roll(vi, 1, 0))
        pr, pi_ = tab_ref[0, 3, :, 0:hw], tab_ref[0, 3, :, hw:2 * hw]
        xin_ref[pl.ds(r0, SUBLANES), 0:hw] = sr + (pr * er - pi_ * ei)
        xin_ref[pl.ds(r0, SUBLANES), hw:2 * hw] = si + (pr * ei + pi_ * er)
        a8r, a8i = tab_ref[0, 4, :, 0:hw], tab_ref[0, 4, :, hw:2 * hw]
        lr = jnp.broadcast_to(vr[SUBLANES - 1:SUBLANES, :], (SUBLANES, hw))
        li = jnp.broadcast_to(vi[SUBLANES - 1:SUBLANES, :], (SUBLANES, hw))
        return a8r * er - a8i * ei + lr, a8r * ei + a8i * er + li

    zero = jnp.zeros((SUBLANES, hw), F32)
    lax.fori_loop(0, n_blocks, body, (zero, zero))
    y = jnp.dot(u, m_ref[0], preferred_element_type=F32)
    y = y + jnp.dot(xin_ref[...].astype(BF16), r_ref[0], preferred_element_type=F32)
    y_ref[...] = y


def _ssm(u2, m_mat, s_mat, r_mat, tab, batch):
    rows = u2.shape[0] // batch
    w = SSM_TILE_W
    mat_spec = pl.BlockSpec((1, w, w), lambda j, b: (j, 0, 0))
    return pl.pallas_call(
        _ssm_kernel,
        grid=(SSM_TILES, batch),
        in_specs=[pl.BlockSpec((rows, w), lambda j, b: (b, j)),
                  mat_spec, mat_spec, mat_spec,
                  pl.BlockSpec((1, 5, SUBLANES, w), lambda j, b: (j, 0, 0, 0))],
        out_specs=pl.BlockSpec((rows, w), lambda j, b: (b, j)),
        out_shape=jax.ShapeDtypeStruct(u2.shape, F32),
        scratch_shapes=[pltpu.VMEM((rows, w), F32), pltpu.VMEM((rows, w), F32)],
        compiler_params=pltpu.CompilerParams(vmem_limit_bytes=VMEM_LIMIT),
        name="ssm",
    )(u2, m_mat, s_mat, r_mat, tab)


def _rel_bucket(dist):
    n = jnp.maximum(dist, 0)
    max_exact = NUM_BUCKETS // 2
    n_f = jnp.maximum(n, 1).astype(F32)
    large = max_exact + (jnp.log(n_f / max_exact) / math.log(REL_MAX_DISTANCE / max_exact)
                         * (NUM_BUCKETS - max_exact)).astype(jnp.int32)
    large = jnp.minimum(large, NUM_BUCKETS - 1)
    return jnp.where(n < max_exact, n, large)


def _bias_mask(rel_bias_g, window, dilation):
    span = window // dilation
    qi = jnp.arange(ATTN_BLOCK)[:, None]
    kj = jnp.arange(2 * ATTN_BLOCK)[None, :]
    delta = ATTN_BLOCK + qi - kj
    band = (delta >= 0) & (delta <= span)
    bias = rel_bias_g[_rel_bucket(jnp.maximum(delta, 0) * dilation)]
    return jnp.where(band[None], bias.transpose(2, 0, 1).astype(F32), NEG_INF)


def _attn_kernel(q_ref, kc_ref, kp_ref, vc_ref, vp_ref, bm_ref, o_ref, lse_ref):
    blk = ATTN_BLOCK
    first_valid_col = jnp.where(pl.program_id(1) == 0, blk, 0)
    lane = lax.broadcasted_iota(jnp.int32, (blk, LANES), 1)
    low = lane < ATTN_HEAD_DIM
    col = lax.broadcasted_iota(jnp.int32, (blk, 2 * blk), 1)
    lane_w = lax.broadcasted_iota(jnp.int32, (blk, LANES), 1)
    n_sub = q_ref.shape[1] // blk
    dn = (((1,), (1,)), ((), ()))
    for n in range(n_sub):
        rows = slice(n * blk, (n + 1) * blk)
        lse_w = jnp.zeros((blk, LANES), F32)
        for pair in range(ATTN_HEADS_PER_GROUP // 2):
            cols = slice(pair * LANES, (pair + 1) * LANES)
            q32 = q_ref[0, rows, cols].astype(F32)
            if n == 0:
                kk = jnp.concatenate([kp_ref[0, :, cols], kc_ref[0, rows, cols]], axis=0)
                vv = jnp.concatenate([vp_ref[0, :, cols], vc_ref[0, rows, cols]], axis=0)
            else:
                kk = kc_ref[0, (n - 1) * blk:(n + 1) * blk, cols]
                vv = vc_ref[0, (n - 1) * blk:(n + 1) * blk, cols]
            outs = []
            for sub in range(2):
                hh = 2 * pair + sub
                qh = jnp.where(low if sub == 0 else ~low, q32, 0.0).astype(BF16)
                s = lax.dot_general(qh, kk, dn, preferred_element_type=F32) + bm_ref[hh]
                if n == 0:
                    s = jnp.where(col >= first_valid_col, s, NEG_INF)
                m = jnp.max(s, axis=-1, keepdims=True)
                p = jnp.exp(s - m)
                l = jnp.sum(p, axis=-1, keepdims=True)
                o = jnp.dot(p.astype(BF16), vv, preferred_element_type=F32) * (1.0 / l)
                outs.append(o)
                lse_w = jnp.where(lane_w == hh, m + jnp.log(l), lse_w)
            o_ref[0, rows, cols] = jnp.where(low, outs[0], outs[1]).astype(BF16)
        lse_ref[0, rows, :] = lse_w


def _attention(q, k, v, bias_mask, col_block):
    ns, m_len, _ = q.shape
    tq = min(TQ_ATTN, m_len)
    per = tq // ATTN_BLOCK
    cur = pl.BlockSpec((1, tq, D_GROUP), lambda s, i: (s, i, col_block))
    prev = pl.BlockSpec((1, ATTN_BLOCK, D_GROUP), lambda s, i: (s, jnp.maximum(i * per - 1, 0), col_block))
    return pl.pallas_call(
        _attn_kernel,
        grid=(ns, m_len // tq),
        in_specs=[cur, cur, prev, cur, prev,
                  _const_spec((ATTN_HEADS_PER_GROUP, ATTN_BLOCK, 2 * ATTN_BLOCK))],
        out_specs=[pl.BlockSpec((1, tq, D_GROUP), lambda s, i: (s, i, 0)),
                   pl.BlockSpec((1, tq, LANES), lambda s, i: (s, i, 0))],
        out_shape=[jax.ShapeDtypeStruct((ns, m_len, D_GROUP), BF16),
                   jax.ShapeDtypeStruct((ns, m_len, LANES), F32)],
        name="attn",
    )(q, k, k, v, v, bias_mask)


def _merge_kernel(final, x_ref, y2_ref, o0_ref, o1_ref, o2_ref, l0_ref, l1_ref, l2_ref, qm_ref, km_ref, vm_ref,
                  g_ref, wzs_ref, wza_ref, wzm_ref, wg0_ref, wg1_ref, wg2_ref, wg3_ref, wg4_ref, wg5_ref,
                  bg_ref, wglu_ref, bglu_ref, wbs_ref, wba_ref, wbm_ref, wout_ref, fg_ref, out_ref,
                  y_scr, o_scr, l_scr):
    x = x_ref[0]
    tm = x.shape[0]
    h = _rms(x, g_ref[...]).astype(BF16)

    def hdot(w_ref):
        return jnp.dot(h, w_ref[...], preferred_element_type=F32)

    for t in range(SSM_CHUNK):
        for j in range(SSM_TILES):
            lo = j * SSM_TILE_W + t * LANES
            y_scr[j, pl.ds(t, tm // SSM_CHUNK, stride=SSM_CHUNK), :] = y2_ref[0, :, lo:lo + LANES]
    tiles = D_GROUP // LANES
    for gi, (o_ref, l_ref) in enumerate(((o0_ref, l0_ref), (o1_ref, l1_ref), (o2_ref, l2_ref))):
        r = ATTN_CONFIGS[gi][1]
        for s in range(r):
            rows = pl.ds(s, tm // r, stride=r)
            for c in range(tiles):
                o_scr[gi * tiles + c, rows, :] = o_ref[0, s, :, c * LANES:(c + 1) * LANES].astype(F32)
            l_scr[gi, rows, :] = l_ref[0, s]

    yg = jax.nn.gelu(jnp.concatenate([y_scr[j] for j in range(SSM_TILES)], axis=-1))
    t = jnp.dot(yg.astype(BF16), wglu_ref[...], preferred_element_type=F32) + bglu_ref[...]
    o_ssm = yg * jax.nn.sigmoid(t) * jax.nn.silu(hdot(wzs_ref))
    p_ssm = jnp.dot(o_ssm.astype(BF16), wbs_ref[...], preferred_element_type=F32)

    ls = (l_scr[0], l_scr[1], l_scr[2])
    mx = jnp.maximum(jnp.maximum(ls[0], ls[1]), ls[2])
    es = [jnp.exp(l - mx) for l in ls]
    inv = 1.0 / (es[0] + es[1] + es[2])
    head_of_lane = lax.broadcasted_iota(jnp.int32, (tm, D_GROUP), 1) // ATTN_HEAD_DIM
    parts = []
    for gi, e in enumerate(es):
        alpha = e * inv
        wide = jnp.zeros((tm, D_GROUP), F32)
        for j in range(ATTN_HEADS_PER_GROUP):
            wide = jnp.where(head_of_lane == j, alpha[:, j:j + 1], wide)
        o_g = jnp.concatenate([o_scr[gi * tiles + c] for c in range(tiles)], axis=-1)
        parts.append(o_g * wide)
    o_attn = jnp.concatenate(parts, axis=-1) * jax.nn.silu(hdot(wza_ref))
    p_attn = jnp.dot(o_attn.astype(BF16), wba_ref[...], preferred_element_type=F32)

    dn = (((1,), (1,)), ((), ()))
    heads = []
    for hd in range(MEM_HEADS):
        cols = slice(hd * MEM_HEAD_DIM, (hd + 1) * MEM_HEAD_DIM)
        s = lax.dot_general(qm_ref[0, :, cols], km_ref[0, :, cols], dn, preferred_element_type=F32)
        s = s * (MEM_HEAD_DIM ** -0.5)
        m = jnp.max(s, axis=-1, keepdims=True)
        p = jnp.exp(s - m)
        l = jnp.sum(p, axis=-1, keepdims=True)
        heads.append(jnp.dot(p.astype(BF16), vm_ref[0, :, cols], preferred_element_type=F32) * (1.0 / l))
    o_mem = jnp.concatenate(heads, axis=-1) * jax.nn.silu(hdot(wzm_ref))
    p_mem = jnp.dot(o_mem.astype(BF16), wbm_ref[...], preferred_element_type=F32)

    gate_refs = (wg0_ref, wg1_ref, wg2_ref, wg3_ref, wg4_ref, wg5_ref)
    per_branch = D_MODEL // GATE_W
    halves = []
    for part in range(per_branch):
        acc = jnp.zeros((tm, GATE_W), F32)
        for br, p_br in enumerate((p_ssm, p_attn, p_mem)):
            k = br * per_branch + part
            gate = jax.nn.sigmoid(hdot(gate_refs[k]) + bg_ref[:, k * GATE_W:(k + 1) * GATE_W])
            acc = acc + gate * p_br[:, part * GATE_W:(part + 1) * GATE_W]
        halves.append(acc)
    merged = jnp.concatenate(halves, axis=-1)
    xn = x + jnp.dot(merged.astype(BF16), wout_ref[...], preferred_element_type=F32)
    if final:
        xn = _rms(xn, fg_ref[...])
    out_ref[0] = xn


def _merge(final, layer, x, y2, o_groups, lse_groups, qm, k_mem, v_mem, g, w_in_bf, bg, wglu, bglu, wbs, wba,
           wbm, wout, fg):
    B, L, _ = x.shape
    tm = TM_MERGE

    def rows(w):
        return pl.BlockSpec((1, tm, w), lambda b, i: (b, i, 0))

    def dec(r, w):
        return pl.BlockSpec((1, r, tm // r, w), lambda b, i: (b, 0, i, 0))

    mem_spec = pl.BlockSpec((1,) + k_mem.shape[1:], lambda b, i: (b, 0, 0))
    rs = [r for _, r in ATTN_CONFIGS]
    n_gate = N_BRANCHES * D_MODEL // GATE_W
    in_specs = ([rows(D_MODEL), pl.BlockSpec((1, tm // SSM_CHUNK, D_SSM * SSM_CHUNK), lambda b, i: (b, i, 0))]
                + [dec(r, D_GROUP) for r in rs] + [dec(r, LANES) for r in rs]
                + [rows(D_MEM), mem_spec, mem_spec, _const_spec(g.shape)]
                + [_w_in_spec(n, layer) for n in ("z_ssm", "z_attn", "z_mem")]
                + [_w_in_spec("gates", layer, part) for part in range(n_gate)]
                + [_const_spec(a.shape) for a in (bg, wglu, bglu, wbs, wba, wbm, wout, fg)])
    return pl.pallas_call(
        functools.partial(_merge_kernel, final),
        grid=(B, L // tm),
        in_specs=in_specs,
        out_specs=rows(D_MODEL),
        out_shape=jax.ShapeDtypeStruct((B, L, D_MODEL), F32),
        scratch_shapes=[pltpu.VMEM((D_SSM // LANES, tm, LANES), F32), pltpu.VMEM((D_ATTN // LANES, tm, LANES), F32),
                        pltpu.VMEM((len(rs), tm, LANES), F32)],
        compiler_params=pltpu.CompilerParams(vmem_limit_bytes=VMEM_LIMIT),
        name="merge",
    )(x, y2, *o_groups, *lse_groups, qm, k_mem, v_mem, g, *([w_in_bf] * (3 + n_gate)), bg, wglu, bglu, wbs, wba,
      wbm, wout, fg)


def kernel(x, mem, norm_g, mem_norm_g, w_in, b_gate, ssm_lambda_re, ssm_lambda_im, ssm_log_dt, ssm_b_re,
           ssm_b_im, ssm_c_re, ssm_c_im, ssm_d, w_glu, b_glu, w_mem_kv, w_br_ssm, w_br_attn, w_br_mem,
           w_out, rel_bias, final_norm_g):
    B, L, _ = x.shape
    assert L % (ATTN_CONFIGS[-1][1] * ATTN_BLOCK) == 0 and L % TM_INPROJ == 0 and L % TM_MERGE == 0
    bias_masks = [_bias_mask(rel_bias[:, gi * ATTN_HEADS_PER_GROUP:(gi + 1) * ATTN_HEADS_PER_GROUP], win, dil)
                  for gi, (win, dil) in enumerate(ATTN_CONFIGS)]
    fg = final_norm_g.reshape(1, D_MODEL)
    w_in_bf = w_in.astype(BF16)
    n_chunks = B * L // SSM_CHUNK
    for layer in range(DEPTH):
        g = norm_g[layer].reshape(1, D_MODEL)
        k_mem, v_mem = _mem_kv(mem, mem_norm_g[layer].reshape(1, D_MODEL), w_mem_kv[layer].astype(BF16))
        u2, *qkv, qm = _in_proj(x, g, w_in_bf, layer)

        mats = _ssm_prep(ssm_lambda_re[layer], ssm_lambda_im[layer], ssm_log_dt[layer], ssm_b_re[layer],
                         ssm_b_im[layer], ssm_c_re[layer], ssm_c_im[layer], ssm_d[layer])
        y2 = _ssm(u2.reshape(n_chunks, D_SSM * SSM_CHUNK), *mats, batch=B)
        y2 = y2.reshape(B, L // SSM_CHUNK, D_SSM * SSM_CHUNK)

        o_groups, lse_groups = [], []
        for gi, (_, r) in enumerate(ATTN_CONFIGS):
            m_len = L // r
            q_g, k_g, v_g = (qkv[3 * idx + gi].reshape(B * r, m_len, D_GROUP) for idx in range(3))
            o_g, lse_g = _attention(q_g, k_g, v_g, bias_masks[gi], 0)
            o_groups.append(o_g.reshape(B, r, m_len, D_GROUP))
            lse_groups.append(lse_g.reshape(B, r, m_len, LANES))

        x = _merge(layer == DEPTH - 1, layer, x, y2, o_groups, lse_groups, qm, k_mem, v_mem, g, w_in_bf,
                   b_gate[layer].reshape(1, -1), w_glu[layer].astype(BF16), b_glu[layer].reshape(1, -1),
                   w_br_ssm[layer].astype(BF16), w_br_attn[layer].astype(BF16), w_br_mem[layer].astype(BF16),
                   w_out[layer].astype(BF16), fg)
    return x
```

```python
import functools
import math

import jax
import jax.numpy as jnp
import numpy as np
from jax import lax
from jax.experimental import pallas as pl
from jax.experimental.pallas import tpu as pltpu

F32 = jnp.float32
BF16 = jnp.bfloat16

D_MODEL = 1024
DEPTH = 2
EPS = 1e-6
N_BRANCHES = 3
D_SSM = 768
SSM_GROUP = 16
SSM_GROUPS = 48
SSM_STATE = 64
ATTN_HEAD_DIM = 64
ATTN_HEADS_PER_GROUP = 4
ATTN_CONFIGS = ((128, 1), (512, 4), (2048, 16))
N_ATTN_HEADS = 12
D_ATTN = 768
ATTN_BLOCK = 128
NUM_BUCKETS = 32
REL_MAX_DISTANCE = 2048
NEG_INF = -1e30
MEM_HEADS = 4
MEM_HEAD_DIM = 128
D_MEM = 512
D_GROUP = ATTN_HEADS_PER_GROUP * ATTN_HEAD_DIM

LANES = 128
SUBLANES = 8
SSM_CHUNK = SUBLANES
SSM_TILE_GROUPS = LANES // SSM_GROUP
SSM_TILES = D_SSM // LANES
SSM_TILE_W = SSM_CHUNK * LANES
SSM_HALF = SSM_TILE_GROUPS * SSM_STATE
VMEM_LIMIT = 56 * 1024 * 1024

TM_INPROJ = 512
TM_MERGE = 256
TQ_ATTN = 512


def _rms(x, g):
    return x * lax.rsqrt(jnp.mean(x * x, axis=-1, keepdims=True) + EPS) * g


def _const_spec(shape):
    n = len(shape)
    return pl.BlockSpec(shape, lambda *_: (0,) * n, pipeline_mode=pl.Buffered(1))


def _memkv_kernel(mem_ref, g_ref, w_ref, k_ref, v_ref):
    h = _rms(mem_ref[0], g_ref[...]).astype(BF16)
    k_ref[0] = jnp.dot(h, w_ref[:, :D_MEM], preferred_element_type=F32).astype(BF16)
    v_ref[0] = jnp.dot(h, w_ref[:, D_MEM:], preferred_element_type=F32).astype(BF16)


def _mem_kv(mem, g, w_bf16):
    B, ML, _ = mem.shape
    return pl.pallas_call(
        _memkv_kernel,
        grid=(B,),
        in_specs=[pl.BlockSpec((1, ML, D_MODEL), lambda b: (b, 0, 0)),
                  _const_spec((1, D_MODEL)),
                  _const_spec((D_MODEL, 2 * D_MEM))],
        out_specs=[pl.BlockSpec((1, ML, D_MEM), lambda b: (b, 0, 0)),
                   pl.BlockSpec((1, ML, D_MEM), lambda b: (b, 0, 0))],
        out_shape=[jax.ShapeDtypeStruct((B, ML, D_MEM), BF16)] * 2,
        name="mem_kv",
    )(mem, g, w_bf16)


_IN_SIZES = (D_SSM, D_SSM, D_ATTN, D_ATTN, D_ATTN, D_ATTN, D_MEM, D_MEM, N_BRANCHES * D_MODEL)
_IN_OFFS = tuple(int(v) for v in np.concatenate([[0], np.cumsum(_IN_SIZES)]))
_IN_NAMES = ("u", "z_ssm", "q", "k", "v", "z_attn", "q_mem", "z_mem", "gates")
GATE_W = 512


def _w_in_spec(name, layer, part=0):
    idx = _IN_NAMES.index(name)
    width = GATE_W if name == "gates" else _IN_SIZES[idx]
    block, rem = divmod(_IN_OFFS[idx], width)
    assert rem == 0
    return pl.BlockSpec((None, D_MODEL, width), lambda *_: (layer, 0, block + part),
                        pipeline_mode=pl.Buffered(1))


def _inproj_kernel(x_ref, g_ref, wu_ref, wq_ref, wk_ref, wv_ref, wqm_ref, u2_ref, *rest):
    qkv_refs, qm_ref, scr = rest[:9], rest[9], rest[10]
    h = _rms(x_ref[0], g_ref[...]).astype(BF16)
    tm = h.shape[0]
    def to_scratch(p):
        for j in range(p.shape[1] // LANES):
            scr[j] = p[:, j * LANES:(j + 1) * LANES]

    to_scratch(jnp.dot(h, wu_ref[...], preferred_element_type=F32))
    for s in range(SSM_CHUNK):
        for j in range(SSM_TILES):
            lo = j * SSM_TILE_W + s * LANES
            u2_ref[0, :, lo:lo + LANES] = scr[j, pl.ds(s, tm // SSM_CHUNK, stride=SSM_CHUNK), :].astype(BF16)
    for idx, (w_ref, scale) in enumerate(((wq_ref, ATTN_HEAD_DIM ** -0.5), (wk_ref, None), (wv_ref, None))):
        p = jnp.dot(h, w_ref[...], preferred_element_type=F32)
        if scale is not None:
            p = p * scale
        to_scratch(p)
        tiles = D_GROUP // LANES
        for gi, (_, r) in enumerate(ATTN_CONFIGS):
            o_ref = qkv_refs[3 * idx + gi]
            for s in range(r):
                for c in range(tiles):
                    piece = scr[gi * tiles + c, pl.ds(s, tm // r, stride=r), :]
                    o_ref[0, s, :, c * LANES:(c + 1) * LANES] = piece.astype(BF16)
    qm_ref[0] = jnp.dot(h, wqm_ref[...], preferred_element_type=F32).astype(BF16)


def _in_proj(x, g, w_in_bf, layer):
    B, L, _ = x.shape
    tm = TM_INPROJ
    out_specs = [pl.BlockSpec((1, tm // SSM_CHUNK, D_SSM * SSM_CHUNK), lambda b, i: (b, i, 0))]
    out_shape = [jax.ShapeDtypeStruct((B, L // SSM_CHUNK, D_SSM * SSM_CHUNK), BF16)]
    for _ in range(3):
        for _, r in ATTN_CONFIGS:
            out_specs.append(pl.BlockSpec((1, r, tm // r, D_GROUP), lambda b, i: (b, 0, i, 0)))
            out_shape.append(jax.ShapeDtypeStruct((B, r, L // r, D_GROUP), BF16))
    out_specs.append(pl.BlockSpec((1, tm, D_MEM), lambda b, i: (b, i, 0)))
    out_shape.append(jax.ShapeDtypeStruct((B, L, D_MEM), BF16))
    return pl.pallas_call(
        _inproj_kernel,
        grid=(B, L // tm),
        in_specs=[pl.BlockSpec((1, tm, D_MODEL), lambda b, i: (b, i, 0)),
                  _const_spec((1, D_MODEL))]
                 + [_w_in_spec(n, layer) for n in ("u", "q", "k", "v", "q_mem")],
        out_specs=out_specs,
        out_shape=out_shape,
        scratch_shapes=[pltpu.VMEM((D_SSM // LANES, tm, LANES), F32)],
        compiler_params=pltpu.CompilerParams(vmem_limit_bytes=VMEM_LIMIT),
        name="in_proj",
    )(x, g, *([w_in_bf] * 5))


def _cmul(ar, ai, br, bi):
    return ar * br - ai * bi, ar * bi + ai * br


def _ssm_prep(lre, lim, log_dt, b_re, b_im, c_re, c_im, d):
    hp = lax.Precision.HIGHEST
    G, P, H, C = SSM_GROUPS, SSM_STATE, SSM_GROUP, SSM_CHUNK
    dt = jnp.exp(log_dt)[:, None]
    mag = jnp.exp(lre * dt)
    ar, ai = mag * jnp.cos(lim * dt), mag * jnp.sin(lim * dt)
    den = lre * lre + lim * lim
    nr, ni = ar - 1.0, ai
    fr = (nr * lre + ni * lim) / den
    fi = (ni * lre - nr * lim) / den
    bbr = fr[..., None] * b_re - fi[..., None] * b_im
    bbi = fr[..., None] * b_im + fi[..., None] * b_re
    prs, pis = [jnp.ones_like(ar)], [jnp.zeros_like(ai)]
    for _ in range(C):
        r_, i_ = _cmul(prs[-1], pis[-1], ar, ai)
        prs.append(r_)
        pis.append(i_)
    PR, PI = jnp.stack(prs), jnp.stack(pis)
    wr = PR[:C, :, :, None] * bbr - PI[:C, :, :, None] * bbi
    wi = PR[:C, :, :, None] * bbi + PI[:C, :, :, None] * bbr
    kk = (jnp.einsum('ghp,tgpk->tghk', c_re, wr, precision=hp)
          - jnp.einsum('ghp,tgpk->tghk', c_im, wi, precision=hp))
    kk = kk.at[0].add(jnp.eye(H, dtype=F32)[None] * d.reshape(G, H)[:, :, None])
    J, GL = SSM_TILES, SSM_TILE_GROUPS
    k_tab = kk.reshape(C, J, GL, H, H).transpose(1, 0, 4, 2, 3).reshape(J, C, H, LANES)
    sw = jnp.stack([wr[::-1], wi[::-1]]).reshape(2, C, J, GL, P, H)
    s_tab = sw.transpose(2, 1, 5, 0, 3, 4).reshape(J, C, H, 2 * SSM_HALF)
    cr = c_re[None] * PR[1:, :, None, :] - c_im[None] * PI[1:, :, None, :]
    ci = c_re[None] * PI[1:, :, None, :] + c_im[None] * PR[1:, :, None, :]
    rw = jnp.stack([cr, -ci]).reshape(2, C, J, GL, H, P)
    r_tab = rw.transpose(2, 1, 4, 0, 3, 5).reshape(J, C, H, 2 * SSM_HALF)
    alr, ali = PR[C], PI[C]
    qrs, qis = [jnp.ones_like(alr)], [jnp.zeros_like(ali)]
    for _ in range(SUBLANES):
        r_, i_ = _cmul(qrs[-1], qis[-1], alr, ali)
        qrs.append(r_)
        qis.append(i_)

    def lay(zr, zi):
        zr = zr.reshape(zr.shape[:-2] + (J, GL * P))
        zi = zi.reshape(zi.shape[:-2] + (J, GL * P))
        return jnp.concatenate([zr, zi], axis=-1)

    rows = jnp.arange(SUBLANES)[:, None, None]
    tabs = []
    for dsh in (1, 2, 4):
        full = jnp.broadcast_to(lay(qrs[dsh], qis[dsh])[None], (SUBLANES, J, 2 * SSM_HALF))
        tabs.append(jnp.where(rows >= dsh, full, 0.0))
    tabs.append(jnp.stack([lay(qrs[i], qis[i]) for i in range(SUBLANES)]))
    tabs.append(jnp.broadcast_to(lay(qrs[SUBLANES], qis[SUBLANES])[None], (SUBLANES, J, 2 * SSM_HALF)))
    tab = jnp.stack(tabs).transpose(2, 0, 1, 3)
    return k_tab, s_tab, r_tab, tab.astype(F32)


def _ssm_expand(k_ref, s_ref, r_ref, m_scr, s_scr, r_scr):
    C, H, GL = SSM_CHUNK, SSM_GROUP, SSM_TILE_GROUPS
    w = SSM_TILE_W
    row_g = lax.broadcasted_iota(jnp.int32, (LANES, w), 0) // H
    col_g = (lax.broadcasted_iota(jnp.int32, (LANES, w), 1) % SSM_HALF) // SSM_STATE
    same_state = row_g == col_g
    same_chan = (lax.broadcasted_iota(jnp.int32, (LANES, LANES), 0) // H
                 == lax.broadcasted_iota(jnp.int32, (LANES, LANES), 1) // H)
    m_scr[...] = jnp.zeros(m_scr.shape, m_scr.dtype)
    for i in range(C):
        rows = slice(i * LANES, (i + 1) * LANES)
        s_scr[rows, :] = jnp.where(same_state, jnp.tile(s_ref[i], (GL, 1)), 0.0).astype(BF16)
        r_blk = jnp.where(same_state, jnp.tile(r_ref[i], (GL, 1)), 0.0)
        r_scr[:, rows] = r_blk.T.astype(BF16)
        d_blk = jnp.where(same_chan, jnp.tile(k_ref[i], (GL, 1)), 0.0).astype(BF16)
        for s in range(C - i):
            m_scr[s * LANES:(s + 1) * LANES, (s + i) * LANES:(s + i + 1) * LANES] = d_blk


def _ssm_kernel(u_ref, k_ref, s_ref, r_ref, tab_ref, y_ref, m_scr, s_scr, r_scr, upd_ref, xin_ref):
    @pl.when(pl.program_id(1) == 0)
    def _():
        _ssm_expand(k_ref, s_ref, r_ref, m_scr, s_scr, r_scr)

    hw = SSM_HALF
    u = u_ref[...]
    upd_ref[...] = jnp.dot(u, s_scr[...], preferred_element_type=F32)
    n_blocks = u.shape[0] // SUBLANES
    row = lax.broadcasted_iota(jnp.int32, (SUBLANES, hw), 0)

    def body(i, carry):
        er, ei = carry
        r0 = pl.multiple_of(i * SUBLANES, SUBLANES)
        vr = upd_ref[pl.ds(r0, SUBLANES), 0:hw]
        vi = upd_ref[pl.ds(r0, SUBLANES), hw:2 * hw]
        for lvl, dsh in enumerate((1, 2, 4)):
            cr, ci = tab_ref[lvl, :, 0:hw], tab_ref[lvl, :, hw:2 * hw]
            sr, si = pltpu.roll(vr, dsh, 0), pltpu.roll(vi, dsh, 0)
            vr, vi = vr + (cr * sr - ci * si), vi + (cr * si + ci * sr)
        sr = jnp.where(row == 0, 0.0, pltpu.roll(vr, 1, 0))
        si = jnp.where(row == 0, 0.0, pltpu.roll(vi, 1, 0))
        pr, pi_ = tab_ref[3, :, 0:hw], tab_ref[3, :, hw:2 * hw]
        xin_ref[pl.ds(r0, SUBLANES), 0:hw] = sr + (pr * er - pi_ * ei)
        xin_ref[pl.ds(r0, SUBLANES), hw:2 * hw] = si + (pr * ei + pi_ * er)
        a8r, a8i = tab_ref[4, :, 0:hw], tab_ref[4, :, hw:2 * hw]
        lr = jnp.broadcast_to(vr[SUBLANES - 1:SUBLANES, :], (SUBLANES, hw))
        li = jnp.broadcast_to(vi[SUBLANES - 1:SUBLANES, :], (SUBLANES, hw))
        return a8r * er - a8i * ei + lr, a8r * ei + a8i * er + li

    zero = jnp.zeros((SUBLANES, hw), F32)
    lax.fori_loop(0, n_blocks, body, (zero, zero))
    y = jnp.dot(u, m_scr[...], preferred_element_type=F32)
    y = y + jnp.dot(xin_ref[...].astype(BF16), r_scr[...], preferred_element_type=F32)
    y_ref[...] = y


def _ssm(u2, tables, layer, batch):
    rows = u2.shape[0] // batch
    w = SSM_TILE_W

    def tab_spec(a):
        nd = a.ndim - 2
        return pl.BlockSpec((None, None) + a.shape[2:], lambda j, b: (layer, j) + (0,) * nd)

    return pl.pallas_call(
        _ssm_kernel,
        grid=(SSM_TILES, batch),
        in_specs=[pl.BlockSpec((rows, w), lambda j, b: (b, j))] + [tab_spec(a) for a in tables],
        out_specs=pl.BlockSpec((rows, w), lambda j, b: (b, j)),
        out_shape=jax.ShapeDtypeStruct(u2.shape, F32),
        scratch_shapes=[pltpu.VMEM((w, w), BF16)] * 3 + [pltpu.VMEM((rows, w), F32)] * 2,
        compiler_params=pltpu.CompilerParams(vmem_limit_bytes=VMEM_LIMIT),
        name="ssm",
    )(u2, *tables)


def _rel_bucket(dist):
    n = jnp.maximum(dist, 0)
    max_exact = NUM_BUCKETS // 2
    n_f = jnp.maximum(n, 1).astype(F32)
    large = max_exact + (jnp.log(n_f / max_exact) / math.log(REL_MAX_DISTANCE / max_exact)
                         * (NUM_BUCKETS - max_exact)).astype(jnp.int32)
    large = jnp.minimum(large, NUM_BUCKETS - 1)
    return jnp.where(n < max_exact, n, large)


def _bias_mask(rel_bias_g, window, dilation):
    span = window // dilation
    qi = jnp.arange(ATTN_BLOCK)[:, None]
    kj = jnp.arange(2 * ATTN_BLOCK)[None, :]
    delta = ATTN_BLOCK + qi - kj
    band = (delta >= 0) & (delta <= span)
    bias = rel_bias_g[_rel_bucket(jnp.maximum(delta, 0) * dilation)]
    return jnp.where(band[None], bias.transpose(2, 0, 1).astype(F32), NEG_INF)


def _attn_kernel(q_ref, kc_ref, kp_ref, vc_ref, vp_ref, bm_ref, o_ref, lse_ref):
    blk = ATTN_BLOCK
    first_valid_col = jnp.where(pl.program_id(1) == 0, blk, 0)
    lane = lax.broadcasted_iota(jnp.int32, (blk, LANES), 1)
    low = lane < ATTN_HEAD_DIM
    col = lax.broadcasted_iota(jnp.int32, (blk, 2 * blk), 1)
    lane_w = lax.broadcasted_iota(jnp.int32, (blk, LANES), 1)
    n_sub = q_ref.shape[1] // blk
    dn = (((1,), (1,)), ((), ()))
    for n in range(n_sub):
        rows = slice(n * blk, (n + 1) * blk)
        lse_w = jnp.zeros((blk, LANES), F32)
        for pair in range(ATTN_HEADS_PER_GROUP // 2):
            cols = slice(pair * LANES, (pair + 1) * LANES)
            q32 = q_ref[0, rows, cols].astype(F32)
            if n == 0:
                kk = jnp.concatenate([kp_ref[0, :, cols], kc_ref[0, rows, cols]], axis=0)
                vv = jnp.concatenate([vp_ref[0, :, cols], vc_ref[0, rows, cols]], axis=0)
            else:
                kk = kc_ref[0, (n - 1) * blk:(n + 1) * blk, cols]
                vv = vc_ref[0, (n - 1) * blk:(n + 1) * blk, cols]
            outs = []
            for sub in range(2):
                hh = 2 * pair + sub
                qh = jnp.where(low if sub == 0 else ~low, q32, 0.0).astype(BF16)
                s = lax.dot_general(qh, kk, dn, preferred_element_type=F32) + bm_ref[hh]
                if n == 0:
                    s = jnp.where(col >= first_valid_col, s, NEG_INF)
                m = jnp.max(s, axis=-1, keepdims=True)
                p = jnp.exp(s - m)
                l = jnp.sum(p, axis=-1, keepdims=True)
                o = jnp.dot(p.astype(BF16), vv, preferred_element_type=F32) * (1.0 / l)
                outs.append(o)
                lse_w = jnp.where(lane_w == hh, m + jnp.log(l), lse_w)
            o_ref[0, rows, cols] = jnp.where(low, outs[0], outs[1]).astype(BF16)
        lse_ref[0, rows, :] = lse_w


def _attention(q, k, v, bias_mask, col_block):
    ns, m_len, _ = q.shape
    tq = min(TQ_ATTN, m_len)
    per = tq // ATTN_BLOCK
    cur = pl.BlockSpec((1, tq, D_GROUP), lambda s, i: (s, i, col_block))
    prev = pl.BlockSpec((1, ATTN_BLOCK, D_GROUP), lambda s, i: (s, jnp.maximum(i * per - 1, 0), col_block))
    return pl.pallas_call(
        _attn_kernel,
        grid=(ns, m_len // tq),
        in_specs=[cur, cur, prev, cur, prev,
                  _const_spec((ATTN_HEADS_PER_GROUP, ATTN_BLOCK, 2 * ATTN_BLOCK))],
        out_specs=[pl.BlockSpec((1, tq, D_GROUP), lambda s, i: (s, i, 0)),
                   pl.BlockSpec((1, tq, LANES), lambda s, i: (s, i, 0))],
        out_shape=[jax.ShapeDtypeStruct((ns, m_len, D_GROUP), BF16),
                   jax.ShapeDtypeStruct((ns, m_len, LANES), F32)],
        name="attn",
    )(q, k, k, v, v, bias_mask)


def _merge_kernel(final, x_ref, y2_ref, o0_ref, o1_ref, o2_ref, l0_ref, l1_ref, l2_ref, qm_ref, km_ref, vm_ref,
                  g_ref, wzs_ref, wza_ref, wzm_ref, wg0_ref, wg1_ref, wg2_ref, wg3_ref, wg4_ref, wg5_ref,
                  bg_ref, wglu_ref, bglu_ref, wbs_ref, wba_ref, wbm_ref, wout_ref, fg_ref, out_ref,
                  y_scr, o_scr, l_scr):
    x = x_ref[0]
    tm = x.shape[0]
    h = _rms(x, g_ref[...]).astype(BF16)

    def hdot(w_ref):
        return jnp.dot(h, w_ref[...], preferred_element_type=F32)

    for t in range(SSM_CHUNK):
        for j in range(SSM_TILES):
            lo = j * SSM_TILE_W + t * LANES
            y_scr[j, pl.ds(t, tm // SSM_CHUNK, stride=SSM_CHUNK), :] = y2_ref[0, :, lo:lo + LANES]
    tiles = D_GROUP // LANES
    for gi, (o_ref, l_ref) in enumerate(((o0_ref, l0_ref), (o1_ref, l1_ref), (o2_ref, l2_ref))):
        r = ATTN_CONFIGS[gi][1]
        for s in range(r):
            rows = pl.ds(s, tm // r, stride=r)
            for c in range(tiles):
                o_scr[gi * tiles + c, rows, :] = o_ref[0, s, :, c * LANES:(c + 1) * LANES].astype(F32)
            l_scr[gi, rows, :] = l_ref[0, s]

    yg = jax.nn.gelu(jnp.concatenate([y_scr[j] for j in range(SSM_TILES)], axis=-1))
    t = jnp.dot(yg.astype(BF16), wglu_ref[...], preferred_element_type=F32) + bglu_ref[...]
    o_ssm = yg * jax.nn.sigmoid(t) * jax.nn.silu(hdot(wzs_ref))
    p_ssm = jnp.dot(o_ssm.astype(BF16), wbs_ref[...], preferred_element_type=F32)

    ls = (l_scr[0], l_scr[1], l_scr[2])
    mx = jnp.maximum(jnp.maximum(ls[0], ls[1]), ls[2])
    es = [jnp.exp(l - mx) for l in ls]
    inv = 1.0 / (es[0] + es[1] + es[2])
    head_of_lane = lax.broadcasted_iota(jnp.int32, (tm, D_GROUP), 1) // ATTN_HEAD_DIM
    parts = []
    for gi, e in enumerate(es):
        alpha = e * inv
        wide = jnp.zeros((tm, D_GROUP), F32)
        for j in range(ATTN_HEADS_PER_GROUP):
            wide = jnp.where(head_of_lane == j, alpha[:, j:j + 1], wide)
        o_g = jnp.concatenate([o_scr[gi * tiles + c] for c in range(tiles)], axis=-1)
        parts.append(o_g * wide)
    o_attn = jnp.concatenate(parts, axis=-1) * jax.nn.silu(hdot(wza_ref))
    p_attn = jnp.dot(o_attn.astype(BF16), wba_ref[...], preferred_element_type=F32)

    dn = (((1,), (1,)), ((), ()))
    heads = []
    for hd in range(MEM_HEADS):
        cols = slice(hd * MEM_HEAD_DIM, (hd + 1) * MEM_HEAD_DIM)
        s = lax.dot_general(qm_ref[0, :, cols], km_ref[0, :, cols], dn, preferred_element_type=F32)
        s = s * (MEM_HEAD_DIM ** -0.5)
        m = jnp.max(s, axis=-1, keepdims=True)
        p = jnp.exp(s - m)
        l = jnp.sum(p, axis=-1, keepdims=True)
        heads.append(jnp.dot(p.astype(BF16), vm_ref[0, :, cols], preferred_element_type=F32) * (1.0 / l))
    o_mem = jnp.concatenate(heads, axis=-1) * jax.nn.silu(hdot(wzm_ref))
    p_mem = jnp.dot(o_mem.astype(BF16), wbm_ref[...], preferred_element_type=F32)

    gate_refs = (wg0_ref, wg1_ref, wg2_ref, wg3_ref, wg4_ref, wg5_ref)
    per_branch = D_MODEL // GATE_W
    halves = []
    for part in range(per_branch):
        acc = jnp.zeros((tm, GATE_W), F32)
        for br, p_br in enumerate((p_ssm, p_attn, p_mem)):
            k = br * per_branch + part
            gate = jax.nn.sigmoid(hdot(gate_refs[k]) + bg_ref[:, k * GATE_W:(k + 1) * GATE_W])
            acc = acc + gate * p_br[:, part * GATE_W:(part + 1) * GATE_W]
        halves.append(acc)
    merged = jnp.concatenate(halves, axis=-1)
    xn = x + jnp.dot(merged.astype(BF16), wout_ref[...], preferred_element_type=F32)
    if final:
        xn = _rms(xn, fg_ref[...])
    out_ref[0] = xn


def _merge(final, layer, x, y2, o_groups, lse_groups, qm, k_mem, v_mem, g, w_in_bf, bg, wglu, bglu, wbs, wba,
           wbm, wout, fg):
    B, L, _ = x.shape
    tm = TM_MERGE

    def rows(w):
        return pl.BlockSpec((1, tm, w), lambda b, i: (b, i, 0))

    def dec(r, w):
        return pl.BlockSpec((1, r, tm // r, w), lambda b, i: (b, 0, i, 0))

    mem_spec = pl.BlockSpec((1,) + k_mem.shape[1:], lambda b, i: (b, 0, 0))
    rs = [r for _, r in ATTN_CONFIGS]
    n_gate = N_BRANCHES * D_MODEL // GATE_W
    in_specs = ([rows(D_MODEL), pl.BlockSpec((1, tm // SSM_CHUNK, D_SSM * SSM_CHUNK), lambda b, i: (b, i, 0))]
                + [dec(r, D_GROUP) for r in rs] + [dec(r, LANES) for r in rs]
                + [rows(D_MEM), mem_spec, mem_spec, _const_spec(g.shape)]
                + [_w_in_spec(n, layer) for n in ("z_ssm", "z_attn", "z_mem")]
                + [_w_in_spec("gates", layer, part) for part in range(n_gate)]
                + [_const_spec(a.shape) for a in (bg, wglu, bglu, wbs, wba, wbm, wout, fg)])
    return pl.pallas_call(
        functools.partial(_merge_kernel, final),
        grid=(B, L // tm),
        in_specs=in_specs,
        out_specs=rows(D_MODEL),
        out_shape=jax.ShapeDtypeStruct((B, L, D_MODEL), F32),
        scratch_shapes=[pltpu.VMEM((D_SSM // LANES, tm, LANES), F32), pltpu.VMEM((D_ATTN // LANES, tm, LANES), F32),
                        pltpu.VMEM((len(rs), tm, LANES), F32)],
        compiler_params=pltpu.CompilerParams(vmem_limit_bytes=VMEM_LIMIT),
        name="merge",
    )(x, y2, *o_groups, *lse_groups, qm, k_mem, v_mem, g, *([w_in_bf] * (3 + n_gate)), bg, wglu, bglu, wbs, wba,
      wbm, wout, fg)


def kernel(x, mem, norm_g, mem_norm_g, w_in, b_gate, ssm_lambda_re, ssm_lambda_im, ssm_log_dt, ssm_b_re,
           ssm_b_im, ssm_c_re, ssm_c_im, ssm_d, w_glu, b_glu, w_mem_kv, w_br_ssm, w_br_attn, w_br_mem,
           w_out, rel_bias, final_norm_g):
    B, L, _ = x.shape
    assert L % (ATTN_CONFIGS[-1][1] * ATTN_BLOCK) == 0 and L % TM_INPROJ == 0 and L % TM_MERGE == 0
    bias_masks = [_bias_mask(rel_bias[:, gi * ATTN_HEADS_PER_GROUP:(gi + 1) * ATTN_HEADS_PER_GROUP], win, dil)
                  for gi, (win, dil) in enumerate(ATTN_CONFIGS)]
    fg = final_norm_g.reshape(1, D_MODEL)
    w_in_bf = w_in.astype(BF16)
    per_layer = [_ssm_prep(ssm_lambda_re[i], ssm_lambda_im[i], ssm_log_dt[i], ssm_b_re[i], ssm_b_im[i],
                           ssm_c_re[i], ssm_c_im[i], ssm_d[i]) for i in range(DEPTH)]
    ssm_tables = [jnp.stack(t) for t in zip(*per_layer)]
    n_chunks = B * L // SSM_CHUNK
    for layer in range(DEPTH):
        g = norm_g[layer].reshape(1, D_MODEL)
        k_mem, v_mem = _mem_kv(mem, mem_norm_g[layer].reshape(1, D_MODEL), w_mem_kv[layer].astype(BF16))
        u2, *qkv, qm = _in_proj(x, g, w_in_bf, layer)

        y2 = _ssm(u2.reshape(n_chunks, D_SSM * SSM_CHUNK), ssm_tables, layer, batch=B)
        y2 = y2.reshape(B, L // SSM_CHUNK, D_SSM * SSM_CHUNK)

        o_groups, lse_groups = [], []
        for gi, (_, r) in enumerate(ATTN_CONFIGS):
            m_len = L // r
            q_g, k_g, v_g = (qkv[3 * idx + gi].reshape(B * r, m_len, D_GROUP) for idx in range(3))
            o_g, lse_g = _attention(q_g, k_g, v_g, bias_masks[gi], 0)
            o_groups.append(o_g.reshape(B, r, m_len, D_GROUP))
            lse_groups.append(lse_g.reshape(B, r, m_len, LANES))

        x = _merge(layer == DEPTH - 1, layer, x, y2, o_groups, lse_groups, qm, k_mem, v_mem, g, w_in_bf,
                   b_gate[layer].reshape(1, -1), w_glu[layer].astype(BF16), b_glu[layer].reshape(1, -1),
                   w_br_ssm[layer].astype(BF16), w_br_attn[layer].astype(BF16), w_br_mem[layer].astype(BF16),
                   w_out[layer].astype(BF16), fg)
    return x
```

```python
import functools
import math

import jax
import jax.numpy as jnp
import numpy as np
from jax import lax
from jax.experimental import pallas as pl
from jax.experimental.pallas import tpu as pltpu

F32 = jnp.float32
BF16 = jnp.bfloat16

D_MODEL = 1024
DEPTH = 2
EPS = 1e-6
N_BRANCHES = 3
D_SSM = 768
SSM_GROUP = 16
SSM_GROUPS = 48
SSM_STATE = 64
ATTN_HEAD_DIM = 64
ATTN_HEADS_PER_GROUP = 4
ATTN_CONFIGS = ((128, 1), (512, 4), (2048, 16))
N_ATTN_HEADS = 12
D_ATTN = 768
ATTN_BLOCK = 128
NUM_BUCKETS = 32
REL_MAX_DISTANCE = 2048
NEG_INF = -1e30
MEM_HEADS = 4
MEM_HEAD_DIM = 128
D_MEM = 512
D_GROUP = ATTN_HEADS_PER_GROUP * ATTN_HEAD_DIM

LANES = 128
SUBLANES = 8
SSM_CHUNK = SUBLANES
SSM_TILE_GROUPS = LANES // SSM_GROUP
SSM_TILES = D_SSM // LANES
SSM_TILE_W = SSM_CHUNK * LANES
SSM_HALF = SSM_TILE_GROUPS * SSM_STATE
VMEM_LIMIT = 56 * 1024 * 1024

TM_INPROJ = 1024
TM_MERGE = 512
TQ_ATTN = 512


def _rms(x, g):
    return x * lax.rsqrt(jnp.mean(x * x, axis=-1, keepdims=True) + EPS) * g


def _const_spec(shape):
    n = len(shape)
    return pl.BlockSpec(shape, lambda *_: (0,) * n, pipeline_mode=pl.Buffered(1))


def _memkv_kernel(mem_ref, g_ref, w_ref, k_ref, v_ref):
    h = _rms(mem_ref[0], g_ref[...]).astype(BF16)
    k_ref[0] = jnp.dot(h, w_ref[:, :D_MEM], preferred_element_type=F32).astype(BF16)
    v_ref[0] = jnp.dot(h, w_ref[:, D_MEM:], preferred_element_type=F32).astype(BF16)


def _mem_kv(mem, g, w_bf16):
    B, ML, _ = mem.shape
    return pl.pallas_call(
        _memkv_kernel,
        grid=(B,),
        in_specs=[pl.BlockSpec((1, ML, D_MODEL), lambda b: (b, 0, 0)),
                  _const_spec((1, D_MODEL)),
                  _const_spec((D_MODEL, 2 * D_MEM))],
        out_specs=[pl.BlockSpec((1, ML, D_MEM), lambda b: (b, 0, 0)),
                   pl.BlockSpec((1, ML, D_MEM), lambda b: (b, 0, 0))],
        out_shape=[jax.ShapeDtypeStruct((B, ML, D_MEM), BF16)] * 2,
        name="mem_kv",
    )(mem, g, w_bf16)


_IN_SIZES = (D_SSM, D_SSM, D_ATTN, D_ATTN, D_ATTN, D_ATTN, D_MEM, D_MEM, N_BRANCHES * D_MODEL)
_IN_OFFS = tuple(int(v) for v in np.concatenate([[0], np.cumsum(_IN_SIZES)]))
_IN_NAMES = ("u", "z_ssm", "q", "k", "v", "z_attn", "q_mem", "z_mem", "gates")
GATE_W = 512


def _w_in_spec(name, layer, part=0):
    idx = _IN_NAMES.index(name)
    width = GATE_W if name == "gates" else _IN_SIZES[idx]
    block, rem = divmod(_IN_OFFS[idx], width)
    assert rem == 0
    return pl.BlockSpec((None, D_MODEL, width), lambda *_: (layer, 0, block + part),
                        pipeline_mode=pl.Buffered(1))


def _inproj_kernel(x_ref, g_ref, wu_ref, wq_ref, wk_ref, wv_ref, wqm_ref, u2_ref, *rest):
    qkv_refs, qm_ref, scr = rest[:9], rest[9], rest[10]
    h = _rms(x_ref[0], g_ref[...]).astype(BF16)
    tm = h.shape[0]
    def to_scratch(p):
        for j in range(p.shape[1] // LANES):
            scr[j] = p[:, j * LANES:(j + 1) * LANES]

    to_scratch(jnp.dot(h, wu_ref[...], preferred_element_type=F32))
    for s in range(SSM_CHUNK):
        for j in range(SSM_TILES):
            lo = j * SSM_TILE_W + s * LANES
            u2_ref[0, :, lo:lo + LANES] = scr[j, pl.ds(s, tm // SSM_CHUNK, stride=SSM_CHUNK), :].astype(BF16)
    for idx, (w_ref, scale) in enumerate(((wq_ref, ATTN_HEAD_DIM ** -0.5), (wk_ref, None), (wv_ref, None))):
        p = jnp.dot(h, w_ref[...], preferred_element_type=F32)
        if scale is not None:
            p = p * scale
        to_scratch(p)
        tiles = D_GROUP // LANES
        for gi, (_, r) in enumerate(ATTN_CONFIGS):
            o_ref = qkv_refs[3 * idx + gi]
            for s in range(r):
                for c in range(tiles):
                    piece = scr[gi * tiles + c, pl.ds(s, tm // r, stride=r), :]
                    o_ref[0, s, :, c * LANES:(c + 1) * LANES] = piece.astype(BF16)
    qm_ref[0] = jnp.dot(h, wqm_ref[...], preferred_element_type=F32).astype(BF16)


def _in_proj(x, g, w_in_bf, layer):
    B, L, _ = x.shape
    tm = TM_INPROJ
    out_specs = [pl.BlockSpec((1, tm // SSM_CHUNK, D_SSM * SSM_CHUNK), lambda b, i: (b, i, 0))]
    out_shape = [jax.ShapeDtypeStruct((B, L // SSM_CHUNK, D_SSM * SSM_CHUNK), BF16)]
    for _ in range(3):
        for _, r in ATTN_CONFIGS:
            out_specs.append(pl.BlockSpec((1, r, tm // r, D_GROUP), lambda b, i: (b, 0, i, 0)))
            out_shape.append(jax.ShapeDtypeStruct((B, r, L // r, D_GROUP), BF16))
    out_specs.append(pl.BlockSpec((1, tm, D_MEM), lambda b, i: (b, i, 0)))
    out_shape.append(jax.ShapeDtypeStruct((B, L, D_MEM), BF16))
    return pl.pallas_call(
        _inproj_kernel,
        grid=(B, L // tm),
        in_specs=[pl.BlockSpec((1, tm, D_MODEL), lambda b, i: (b, i, 0)),
                  _const_spec((1, D_MODEL))]
                 + [_w_in_spec(n, layer) for n in ("u", "q", "k", "v", "q_mem")],
        out_specs=out_specs,
        out_shape=out_shape,
        scratch_shapes=[pltpu.VMEM((D_SSM // LANES, tm, LANES), F32)],
        compiler_params=pltpu.CompilerParams(vmem_limit_bytes=VMEM_LIMIT),
        name="in_proj",
    )(x, g, *([w_in_bf] * 5))


def _cmul(ar, ai, br, bi):
    return ar * br - ai * bi, ar * bi + ai * br


def _ssm_prep(lre, lim, log_dt, b_re, b_im, c_re, c_im, d):
    hp = lax.Precision.HIGHEST
    G, P, H, C = SSM_GROUPS, SSM_STATE, SSM_GROUP, SSM_CHUNK
    dt = jnp.exp(log_dt)[:, None]
    mag = jnp.exp(lre * dt)
    ar, ai = mag * jnp.cos(lim * dt), mag * jnp.sin(lim * dt)
    den = lre * lre + lim * lim
    nr, ni = ar - 1.0, ai
    fr = (nr * lre + ni * lim) / den
    fi = (ni * lre - nr * lim) / den
    bbr = fr[..., None] * b_re - fi[..., None] * b_im
    bbi = fr[..., None] * b_im + fi[..., None] * b_re
    prs, pis = [jnp.ones_like(ar)], [jnp.zeros_like(ai)]
    for _ in range(C):
        r_, i_ = _cmul(prs[-1], pis[-1], ar, ai)
        prs.append(r_)
        pis.append(i_)
    PR, PI = jnp.stack(prs), jnp.stack(pis)
    wr = PR[:C, :, :, None] * bbr - PI[:C, :, :, None] * bbi
    wi = PR[:C, :, :, None] * bbi + PI[:C, :, :, None] * bbr
    kk = (jnp.einsum('ghp,tgpk->tghk', c_re, wr, precision=hp)
          - jnp.einsum('ghp,tgpk->tghk', c_im, wi, precision=hp))
    kk = kk.at[0].add(jnp.eye(H, dtype=F32)[None] * d.reshape(G, H)[:, :, None])
    J, GL = SSM_TILES, SSM_TILE_GROUPS
    k_tab = kk.reshape(C, J, GL, H, H).transpose(1, 0, 4, 2, 3).reshape(J, C, H, LANES)
    sw = jnp.stack([wr[::-1], wi[::-1]]).reshape(2, C, J, GL, P, H)
    s_tab = sw.transpose(2, 1, 5, 0, 3, 4).reshape(J, C, H, 2 * SSM_HALF)
    cr = c_re[None] * PR[1:, :, None, :] - c_im[None] * PI[1:, :, None, :]
    ci = c_re[None] * PI[1:, :, None, :] + c_im[None] * PR[1:, :, None, :]
    rw = jnp.stack([cr, -ci]).reshape(2, C, J, GL, H, P)
    r_tab = rw.transpose(2, 1, 4, 0, 3, 5).reshape(J, C, H, 2 * SSM_HALF)
    alr, ali = PR[C], PI[C]
    qrs, qis = [jnp.ones_like(alr)], [jnp.zeros_like(ali)]
    for _ in range(SUBLANES):
        r_, i_ = _cmul(qrs[-1], qis[-1], alr, ali)
        qrs.append(r_)
        qis.append(i_)

    def lay(zr, zi):
        zr = zr.reshape(zr.shape[:-2] + (J, GL * P))
        zi = zi.reshape(zi.shape[:-2] + (J, GL * P))
        return jnp.concatenate([zr, zi], axis=-1)

    rows = jnp.arange(SUBLANES)[:, None, None]
    tabs = []
    for dsh in (1, 2, 4):
        full = jnp.broadcast_to(lay(qrs[dsh], qis[dsh])[None], (SUBLANES, J, 2 * SSM_HALF))
        tabs.append(jnp.where(rows >= dsh, full, 0.0))
    tabs.append(jnp.stack([lay(qrs[i], qis[i]) for i in range(SUBLANES)]))
    tabs.append(jnp.broadcast_to(lay(qrs[SUBLANES], qis[SUBLANES])[None], (SUBLANES, J, 2 * SSM_HALF)))
    tab = jnp.stack(tabs).transpose(2, 0, 1, 3)
    return k_tab, s_tab, r_tab, tab.astype(F32)


def _ssm_expand(k_ref, s_ref, r_ref, m_scr, s_scr, r_scr):
    C, H, GL = SSM_CHUNK, SSM_GROUP, SSM_TILE_GROUPS
    w = SSM_TILE_W
    row_g = lax.broadcasted_iota(jnp.int32, (LANES, w), 0) // H
    col_g = (lax.broadcasted_iota(jnp.int32, (LANES, w), 1) % SSM_HALF) // SSM_STATE
    same_state = row_g == col_g
    same_chan = (lax.broadcasted_iota(jnp.int32, (LANES, LANES), 0) // H
                 == lax.broadcasted_iota(jnp.int32, (LANES, LANES), 1) // H)
    m_scr[...] = jnp.zeros(m_scr.shape, m_scr.dtype)
    for i in range(C):
        rows = slice(i * LANES, (i + 1) * LANES)
        s_scr[rows, :] = jnp.where(same_state, jnp.tile(s_ref[i], (GL, 1)), 0.0).astype(BF16)
        r_blk = jnp.where(same_state, jnp.tile(r_ref[i], (GL, 1)), 0.0)
        r_scr[:, rows] = r_blk.T.astype(BF16)
        d_blk = jnp.where(same_chan, jnp.tile(k_ref[i], (GL, 1)), 0.0).astype(BF16)
        for s in range(C - i):
            m_scr[s * LANES:(s + 1) * LANES, (s + i) * LANES:(s + i + 1) * LANES] = d_blk


def _ssm_kernel(u_ref, k_ref, s_ref, r_ref, tab_ref, y_ref, m_scr, s_scr, r_scr, upd_ref, xin_ref):
    @pl.when(pl.program_id(1) == 0)
    def _():
        _ssm_expand(k_ref, s_ref, r_ref, m_scr, s_scr, r_scr)

    hw = SSM_HALF
    u = u_ref[...]
    upd_ref[...] = jnp.dot(u, s_scr[...], preferred_element_type=F32)
    n_blocks = u.shape[0] // SUBLANES
    row = lax.broadcasted_iota(jnp.int32, (SUBLANES, hw), 0)

    def body(i, carry):
        er, ei = carry
        r0 = pl.multiple_of(i * SUBLANES, SUBLANES)
        vr = upd_ref[pl.ds(r0, SUBLANES), 0:hw]
        vi = upd_ref[pl.ds(r0, SUBLANES), hw:2 * hw]
        for lvl, dsh in enumerate((1, 2, 4)):
            cr, ci = tab_ref[lvl, :, 0:hw], tab_ref[lvl, :, hw:2 * hw]
            sr, si = pltpu.roll(vr, dsh, 0), pltpu.roll(vi, dsh, 0)
            vr, vi = vr + (cr * sr - ci * si), vi + (cr * si + ci * sr)
        sr = jnp.where(row == 0, 0.0, pltpu.roll(vr, 1, 0))
        si = jnp.where(row == 0, 0.0, pltpu.roll(vi, 1, 0))
        pr, pi_ = tab_ref[3, :, 0:hw], tab_ref[3, :, hw:2 * hw]
        xin_ref[pl.ds(r0, SUBLANES), 0:hw] = sr + (pr * er - pi_ * ei)
        xin_ref[pl.ds(r0, SUBLANES), hw:2 * hw] = si + (pr * ei + pi_ * er)
        a8r, a8i = tab_ref[4, :, 0:hw], tab_ref[4, :, hw:2 * hw]
        lr = jnp.broadcast_to(vr[SUBLANES - 1:SUBLANES, :], (SUBLANES, hw))
        li = jnp.broadcast_to(vi[SUBLANES - 1:SUBLANES, :], (SUBLANES, hw))
        return a8r * er - a8i * ei + lr, a8r * ei + a8i * er + li

    zero = jnp.zeros((SUBLANES, hw), F32)
    lax.fori_loop(0, n_blocks, body, (zero, zero))
    y = jnp.dot(u, m_scr[...], preferred_element_type=F32)
    y = y + jnp.dot(xin_ref[...].astype(BF16), r_scr[...], preferred_element_type=F32)
    y_ref[...] = y


def _ssm(u2, tables, layer, batch):
    rows = u2.shape[0] // batch
    w = SSM_TILE_W

    def tab_spec(a):
        nd = a.ndim - 2
        return pl.BlockSpec((None, None) + a.shape[2:], lambda j, b: (layer, j) + (0,) * nd)

    return pl.pallas_call(
        _ssm_kernel,
        grid=(SSM_TILES, batch),
        in_specs=[pl.BlockSpec((rows, w), lambda j, b: (b, j))] + [tab_spec(a) for a in tables],
        out_specs=pl.BlockSpec((rows, w), lambda j, b: (b, j)),
        out_shape=jax.ShapeDtypeStruct(u2.shape, F32),
        scratch_shapes=[pltpu.VMEM((w, w), BF16)] * 3 + [pltpu.VMEM((rows, w), F32)] * 2,
        compiler_params=pltpu.CompilerParams(vmem_limit_bytes=VMEM_LIMIT),
        name="ssm",
    )(u2, *tables)


def _rel_bucket(dist):
    n = jnp.maximum(dist, 0)
    max_exact = NUM_BUCKETS // 2
    n_f = jnp.maximum(n, 1).astype(F32)
    large = max_exact + (jnp.log(n_f / max_exact) / math.log(REL_MAX_DISTANCE / max_exact)
                         * (NUM_BUCKETS - max_exact)).astype(jnp.int32)
    large = jnp.minimum(large, NUM_BUCKETS - 1)
    return jnp.where(n < max_exact, n, large)


def _bias_mask(rel_bias_g, window, dilation):
    span = window // dilation
    qi = jnp.arange(ATTN_BLOCK)[:, None]
    kj = jnp.arange(2 * ATTN_BLOCK)[None, :]
    delta = ATTN_BLOCK + qi - kj
    band = (delta >= 0) & (delta <= span)
    bucket = _rel_bucket(jnp.maximum(delta, 0) * dilation)
    hit = bucket[None, None] == jnp.arange(NUM_BUCKETS)[:, None, None, None]
    bias = jnp.sum(jnp.where(hit, rel_bias_g.astype(F32)[:, :, None, None], 0.0), axis=0)
    return jnp.where(band[None], bias, NEG_INF)


def _attn_kernel(q_ref, kc_ref, kp_ref, vc_ref, vp_ref, bm_ref, o_ref, lse_ref):
    blk = ATTN_BLOCK
    first_valid_col = jnp.where(pl.program_id(1) == 0, blk, 0)
    lane = lax.broadcasted_iota(jnp.int32, (blk, LANES), 1)
    low = lane < ATTN_HEAD_DIM
    col = lax.broadcasted_iota(jnp.int32, (blk, 2 * blk), 1)
    lane_w = lax.broadcasted_iota(jnp.int32, (blk, LANES), 1)
    n_sub = q_ref.shape[1] // blk
    dn = (((1,), (1,)), ((), ()))
    for n in range(n_sub):
        rows = slice(n * blk, (n + 1) * blk)
        lse_w = jnp.zeros((blk, LANES), F32)
        for pair in range(ATTN_HEADS_PER_GROUP // 2):
            cols = slice(pair * LANES, (pair + 1) * LANES)
            q32 = q_ref[0, rows, cols].astype(F32)
            if n == 0:
                kk = jnp.concatenate([kp_ref[0, :, cols], kc_ref[0, rows, cols]], axis=0)
                vv = jnp.concatenate([vp_ref[0, :, cols], vc_ref[0, rows, cols]], axis=0)
            else:
                kk = kc_ref[0, (n - 1) * blk:(n + 1) * blk, cols]
                vv = vc_ref[0, (n - 1) * blk:(n + 1) * blk, cols]
            outs = []
            for sub in range(2):
                hh = 2 * pair + sub
                qh = jnp.where(low if sub == 0 else ~low, q32, 0.0).astype(BF16)
                s = lax.dot_general(qh, kk, dn, preferred_element_type=F32) + bm_ref[hh]
                if n == 0:
                    s = jnp.where(col >= first_valid_col, s, NEG_INF)
                m = jnp.max(s, axis=-1, keepdims=True)
                p = jnp.exp(s - m)
                l = jnp.sum(p, axis=-1, keepdims=True)
                o = jnp.dot(p.astype(BF16), vv, preferred_element_type=F32) * (1.0 / l)
                outs.append(o)
                lse_w = jnp.where(lane_w == hh, m + jnp.log(l), lse_w)
            o_ref[0, rows, cols] = jnp.where(low, outs[0], outs[1]).astype(BF16)
        lse_ref[0, rows, :] = lse_w


def _attention(q, k, v, bias_mask, col_block):
    ns, m_len, _ = q.shape
    tq = min(TQ_ATTN, m_len)
    per = tq // ATTN_BLOCK
    cur = pl.BlockSpec((1, tq, D_GROUP), lambda s, i: (s, i, col_block))
    prev = pl.BlockSpec((1, ATTN_BLOCK, D_GROUP), lambda s, i: (s, jnp.maximum(i * per - 1, 0), col_block))
    return pl.pallas_call(
        _attn_kernel,
        grid=(ns, m_len // tq),
        in_specs=[cur, cur, prev, cur, prev,
                  _const_spec((ATTN_HEADS_PER_GROUP, ATTN_BLOCK, 2 * ATTN_BLOCK))],
        out_specs=[pl.BlockSpec((1, tq, D_GROUP), lambda s, i: (s, i, 0)),
                   pl.BlockSpec((1, tq, LANES), lambda s, i: (s, i, 0))],
        out_shape=[jax.ShapeDtypeStruct((ns, m_len, D_GROUP), BF16),
                   jax.ShapeDtypeStruct((ns, m_len, LANES), F32)],
        name="attn",
    )(q, k, k, v, v, bias_mask)


def _merge_kernel(final, x_ref, y2_ref, o0_ref, o1_ref, o2_ref, l0_ref, l1_ref, l2_ref, qm_ref, km_ref, vm_ref,
                  g_ref, wzs_ref, wza_ref, wzm_ref, wg0_ref, wg1_ref, wg2_ref, wg3_ref, wg4_ref, wg5_ref,
                  bg_ref, wglu_ref, bglu_ref, wbs_ref, wba_ref, wbm_ref, wout_ref, fg_ref, out_ref,
                  y_scr, o_scr, l_scr):
    x = x_ref[0]
    tm = x.shape[0]
    h = _rms(x, g_ref[...]).astype(BF16)

    def hdot(w_ref):
        return jnp.dot(h, w_ref[...], preferred_element_type=F32)

    for t in range(SSM_CHUNK):
        for j in range(SSM_TILES):
            lo = j * SSM_TILE_W + t * LANES
            y_scr[j, pl.ds(t, tm // SSM_CHUNK, stride=SSM_CHUNK), :] = y2_ref[0, :, lo:lo + LANES]
    tiles = D_GROUP // LANES
    for gi, (o_ref, l_ref) in enumerate(((o0_ref, l0_ref), (o1_ref, l1_ref), (o2_ref, l2_ref))):
        r = ATTN_CONFIGS[gi][1]
        for s in range(r):
            rows = pl.ds(s, tm // r, stride=r)
            for c in range(tiles):
                o_scr[gi * tiles + c, rows, :] = o_ref[0, s, :, c * LANES:(c + 1) * LANES].astype(F32)
            l_scr[gi, rows, :] = l_ref[0, s]

    yg = jax.nn.gelu(jnp.concatenate([y_scr[j] for j in range(SSM_TILES)], axis=-1))
    t = jnp.dot(yg.astype(BF16), wglu_ref[...], preferred_element_type=F32) + bglu_ref[...]
    o_ssm = yg * jax.nn.sigmoid(t) * jax.nn.silu(hdot(wzs_ref))
    p_ssm = jnp.dot(o_ssm.astype(BF16), wbs_ref[...], preferred_element_type=F32)

    ls = (l_scr[0], l_scr[1], l_scr[2])
    mx = jnp.maximum(jnp.maximum(ls[0], ls[1]), ls[2])
    es = [jnp.exp(l - mx) for l in ls]
    inv = 1.0 / (es[0] + es[1] + es[2])
    head_of_lane = lax.broadcasted_iota(jnp.int32, (tm, D_GROUP), 1) // ATTN_HEAD_DIM
    parts = []
    for gi, e in enumerate(es):
        alpha = e * inv
        wide = jnp.zeros((tm, D_GROUP), F32)
        for j in range(ATTN_HEADS_PER_GROUP):
            wide = jnp.where(head_of_lane == j, alpha[:, j:j + 1], wide)
        o_g = jnp.concatenate([o_scr[gi * tiles + c] for c in range(tiles)], axis=-1)
        parts.append(o_g * wide)
    o_attn = jnp.concatenate(parts, axis=-1) * jax.nn.silu(hdot(wza_ref))
    p_attn = jnp.dot(o_attn.astype(BF16), wba_ref[...], preferred_element_type=F32)

    dn = (((1,), (1,)), ((), ()))
    heads = []
    for hd in range(MEM_HEADS):
        cols = slice(hd * MEM_HEAD_DIM, (hd + 1) * MEM_HEAD_DIM)
        s = lax.dot_general(qm_ref[0, :, cols], km_ref[0, :, cols], dn, preferred_element_type=F32)
        s = s * (MEM_HEAD_DIM ** -0.5)
        m = jnp.max(s, axis=-1, keepdims=True)
        p = jnp.exp(s - m)
        l = jnp.sum(p, axis=-1, keepdims=True)
        heads.append(jnp.dot(p.astype(BF16), vm_ref[0, :, cols], preferred_element_type=F32) * (1.0 / l))
    o_mem = jnp.concatenate(heads, axis=-1) * jax.nn.silu(hdot(wzm_ref))
    p_mem = jnp.dot(o_mem.astype(BF16), wbm_ref[...], preferred_element_type=F32)

    gate_refs = (wg0_ref, wg1_ref, wg2_ref, wg3_ref, wg4_ref, wg5_ref)
    per_branch = D_MODEL // GATE_W
    halves = []
    for part in range(per_branch):
        acc = jnp.zeros((tm, GATE_W), F32)
        for br, p_br in enumerate((p_ssm, p_attn, p_mem)):
            k = br * per_branch + part
            gate = jax.nn.sigmoid(hdot(gate_refs[k]) + bg_ref[:, k * GATE_W:(k + 1) * GATE_W])
            acc = acc + gate * p_br[:, part * GATE_W:(part + 1) * GATE_W]
        halves.append(acc)
    merged = jnp.concatenate(halves, axis=-1)
    xn = x + jnp.dot(merged.astype(BF16), wout_ref[...], preferred_element_type=F32)
    if final:
        xn = _rms(xn, fg_ref[...])
    out_ref[0] = xn


def _merge(final, layer, x, y2, o_groups, lse_groups, qm, k_mem, v_mem, g, w_in_bf, bg, wglu, bglu, wbs, wba,
           wbm, wout, fg):
    B, L, _ = x.shape
    tm = TM_MERGE

    def rows(w):
        return pl.BlockSpec((1, tm, w), lambda b, i: (b, i, 0))

    def dec(r, w):
        return pl.BlockSpec((1, r, tm // r, w), lambda b, i: (b, 0, i, 0))

    mem_spec = pl.BlockSpec((1,) + k_mem.shape[1:], lambda b, i: (b, 0, 0))
    rs = [r for _, r in ATTN_CONFIGS]
    n_gate = N_BRANCHES * D_MODEL // GATE_W
    in_specs = ([rows(D_MODEL), pl.BlockSpec((1, tm // SSM_CHUNK, D_SSM * SSM_CHUNK), lambda b, i: (b, i, 0))]
                + [dec(r, D_GROUP) for r in rs] + [dec(r, LANES) for r in rs]
                + [rows(D_MEM), mem_spec, mem_spec, _const_spec(g.shape)]
                + [_w_in_spec(n, layer) for n in ("z_ssm", "z_attn", "z_mem")]
                + [_w_in_spec("gates", layer, part) for part in range(n_gate)]
                + [_const_spec(a.shape) for a in (bg, wglu, bglu, wbs, wba, wbm, wout, fg)])
    return pl.pallas_call(
        functools.partial(_merge_kernel, final),
        grid=(B, L // tm),
        in_specs=in_specs,
        out_specs=rows(D_MODEL),
        out_shape=jax.ShapeDtypeStruct((B, L, D_MODEL), F32),
        scratch_shapes=[pltpu.VMEM((D_SSM // LANES, tm, LANES), F32), pltpu.VMEM((D_ATTN // LANES, tm, LANES), F32),
                        pltpu.VMEM((len(rs), tm, LANES), F32)],
        compiler_params=pltpu.CompilerParams(vmem_limit_bytes=VMEM_LIMIT),
        name="merge",
    )(x, y2, *o_groups, *lse_groups, qm, k_mem, v_mem, g, *([w_in_bf] * (3 + n_gate)), bg, wglu, bglu, wbs, wba,
      wbm, wout, fg)


def kernel(x, mem, norm_g, mem_norm_g, w_in, b_gate, ssm_lambda_re, ssm_lambda_im, ssm_log_dt, ssm_b_re,
           ssm_b_im, ssm_c_re, ssm_c_im, ssm_d, w_glu, b_glu, w_mem_kv, w_br_ssm, w_br_attn, w_br_mem,
           w_out, rel_bias, final_norm_g):
    B, L, _ = x.shape
    assert L % (ATTN_CONFIGS[-1][1] * ATTN_BLOCK) == 0 and L % TM_INPROJ == 0 and L % TM_MERGE == 0
    bias_masks = [_bias_mask(rel_bias[:, gi * ATTN_HEADS_PER_GROUP:(gi + 1) * ATTN_HEADS_PER_GROUP], win, dil)
                  for gi, (win, dil) in enumerate(ATTN_CONFIGS)]
    fg = final_norm_g.reshape(1, D_MODEL)
    w_in_bf = w_in.astype(BF16)
    per_layer = [_ssm_prep(ssm_lambda_re[i], ssm_lambda_im[i], ssm_log_dt[i], ssm_b_re[i], ssm_b_im[i],
                           ssm_c_re[i], ssm_c_im[i], ssm_d[i]) for i in range(DEPTH)]
    ssm_tables = [jnp.stack(t) for t in zip(*per_layer)]
    n_chunks = B * L // SSM_CHUNK
    for layer in range(DEPTH):
        g = norm_g[layer].reshape(1, D_MODEL)
        k_mem, v_mem = _mem_kv(mem, mem_norm_g[layer].reshape(1, D_MODEL), w_mem_kv[layer].astype(BF16))
        u2, *qkv, qm = _in_proj(x, g, w_in_bf, layer)

        y2 = _ssm(u2.reshape(n_chunks, D_SSM * SSM_CHUNK), ssm_tables, layer, batch=B)
        y2 = y2.reshape(B, L // SSM_CHUNK, D_SSM * SSM_CHUNK)

        o_groups, lse_groups = [], []
        for gi, (_, r) in enumerate(ATTN_CONFIGS):
            m_len = L // r
            q_g, k_g, v_g = (qkv[3 * idx + gi].reshape(B * r, m_len, D_GROUP) for idx in range(3))
            o_g, lse_g = _attention(q_g, k_g, v_g, bias_masks[gi], 0)
            o_groups.append(o_g.reshape(B, r, m_len, D_GROUP))
            lse_groups.append(lse_g.reshape(B, r, m_len, LANES))

        x = _merge(layer == DEPTH - 1, layer, x, y2, o_groups, lse_groups, qm, k_mem, v_mem, g, w_in_bf,
                   b_gate[layer].reshape(1, -1), w_glu[layer].astype(BF16), b_glu[layer].reshape(1, -1),
                   w_br_ssm[layer].astype(BF16), w_br_attn[layer].astype(BF16), w_br_mem[layer].astype(BF16),
                   w_out[layer].astype(BF16), fg)
    return x
```

```python
import functools
import math

import jax
import jax.numpy as jnp
import numpy as np
from jax import lax
from jax.experimental import pallas as pl
from jax.experimental.pallas import tpu as pltpu

F32 = jnp.float32
BF16 = jnp.bfloat16

D_MODEL = 1024
DEPTH = 2
EPS = 1e-6
N_BRANCHES = 3
D_SSM = 768
SSM_GROUP = 16
SSM_GROUPS = 48
SSM_STATE = 64
ATTN_HEAD_DIM = 64
ATTN_HEADS_PER_GROUP = 4
ATTN_CONFIGS = ((128, 1), (512, 4), (2048, 16))
N_ATTN_HEADS = 12
D_ATTN = 768
ATTN_BLOCK = 128
NUM_BUCKETS = 32
REL_MAX_DISTANCE = 2048
NEG_INF = -1e30
MEM_HEADS = 4
MEM_HEAD_DIM = 128
D_MEM = 512
D_GROUP = ATTN_HEADS_PER_GROUP * ATTN_HEAD_DIM

LANES = 128
SUBLANES = 8
SSM_CHUNK = SUBLANES
SSM_TILE_GROUPS = LANES // SSM_GROUP
SSM_TILES = D_SSM // LANES
SSM_TILE_W = SSM_CHUNK * LANES
SSM_PAIRS_PER_TILE = SSM_TILE_GROUPS // 2
VMEM_LIMIT = 56 * 1024 * 1024

TM_INPROJ = 1024
TM_MERGE = 512
TQ_ATTN = 512


def _rms(x, g):
    return x * lax.rsqrt(jnp.mean(x * x, axis=-1, keepdims=True) + EPS) * g


def _block_transpose(vs):
    n = len(vs)
    width = LANES // n
    block = lax.broadcasted_iota(jnp.int32, vs[0].shape, 1) // width
    d = n // 2
    while d >= 1:
        bit_set = (block & d) != 0
        new = list(vs)
        for i in range(n):
            if i & d == 0:
                a, b = vs[i], vs[i + d]
                new[i] = jnp.where(bit_set, pltpu.roll(b, d * width, 1), a)
                new[i + d] = jnp.where(bit_set, b, pltpu.roll(a, LANES - d * width, 1))
        vs = new
        d //= 2
    return vs


def _const_spec(shape):
    n = len(shape)
    return pl.BlockSpec(shape, lambda *_: (0,) * n, pipeline_mode=pl.Buffered(1))


def _memkv_kernel(mem_ref, g_ref, w_ref, k_ref, v_ref):
    h = _rms(mem_ref[0], g_ref[...]).astype(BF16)
    k_ref[0] = jnp.dot(h, w_ref[:, :D_MEM], preferred_element_type=F32).astype(BF16)
    v_ref[0] = jnp.dot(h, w_ref[:, D_MEM:], preferred_element_type=F32).astype(BF16)


def _mem_kv(mem, g, w_bf16):
    B, ML, _ = mem.shape
    return pl.pallas_call(
        _memkv_kernel,
        grid=(B,),
        in_specs=[pl.BlockSpec((1, ML, D_MODEL), lambda b: (b, 0, 0)),
                  _const_spec((1, D_MODEL)),
                  _const_spec((D_MODEL, 2 * D_MEM))],
        out_specs=[pl.BlockSpec((1, ML, D_MEM), lambda b: (b, 0, 0)),
                   pl.BlockSpec((1, ML, D_MEM), lambda b: (b, 0, 0))],
        out_shape=[jax.ShapeDtypeStruct((B, ML, D_MEM), BF16)] * 2,
        name="mem_kv",
    )(mem, g, w_bf16)


_IN_SIZES = (D_SSM, D_SSM, D_ATTN, D_ATTN, D_ATTN, D_ATTN, D_MEM, D_MEM, N_BRANCHES * D_MODEL)
_IN_OFFS = tuple(int(v) for v in np.concatenate([[0], np.cumsum(_IN_SIZES)]))
_IN_NAMES = ("u", "z_ssm", "q", "k", "v", "z_attn", "q_mem", "z_mem", "gates")
GATE_W = 512


def _w_in_spec(name, layer, part=0):
    idx = _IN_NAMES.index(name)
    width = GATE_W if name == "gates" else _IN_SIZES[idx]
    block, rem = divmod(_IN_OFFS[idx], width)
    assert rem == 0
    return pl.BlockSpec((None, D_MODEL, width), lambda *_: (layer, 0, block + part),
                        pipeline_mode=pl.Buffered(1))


def _inproj_kernel(x_ref, g_ref, wu_ref, wq_ref, wk_ref, wv_ref, wqm_ref, u2_ref, *rest):
    qkv_refs, qm_ref, scr = rest[:9], rest[9], rest[10]
    h = _rms(x_ref[0], g_ref[...]).astype(BF16)
    tm = h.shape[0]
    def to_scratch(p):
        for j in range(p.shape[1] // LANES):
            scr[j] = p[:, j * LANES:(j + 1) * LANES]

    to_scratch(jnp.dot(h, wu_ref[...], preferred_element_type=F32))
    for j in range(SSM_TILES):
        steps = [scr[j, pl.ds(s, tm // SSM_CHUNK, stride=SSM_CHUNK), :] for s in range(SSM_CHUNK)]
        for gl, blk in enumerate(_block_transpose(steps)):
            lo = j * SSM_TILE_W + gl * LANES
            u2_ref[0, :, lo:lo + LANES] = blk.astype(BF16)
    for idx, (w_ref, scale) in enumerate(((wq_ref, ATTN_HEAD_DIM ** -0.5), (wk_ref, None), (wv_ref, None))):
        p = jnp.dot(h, w_ref[...], preferred_element_type=F32)
        if scale is not None:
            p = p * scale
        to_scratch(p)
        tiles = D_GROUP // LANES
        for gi, (_, r) in enumerate(ATTN_CONFIGS):
            o_ref = qkv_refs[3 * idx + gi]
            for s in range(r):
                for c in range(tiles):
                    piece = scr[gi * tiles + c, pl.ds(s, tm // r, stride=r), :]
                    o_ref[0, s, :, c * LANES:(c + 1) * LANES] = piece.astype(BF16)
    qm_ref[0] = jnp.dot(h, wqm_ref[...], preferred_element_type=F32).astype(BF16)


def _in_proj(x, g, w_in_bf, layer):
    B, L, _ = x.shape
    tm = TM_INPROJ
    out_specs = [pl.BlockSpec((1, tm // SSM_CHUNK, D_SSM * SSM_CHUNK), lambda b, i: (b, i, 0))]
    out_shape = [jax.ShapeDtypeStruct((B, L // SSM_CHUNK, D_SSM * SSM_CHUNK), BF16)]
    for _ in range(3):
        for _, r in ATTN_CONFIGS:
            out_specs.append(pl.BlockSpec((1, r, tm // r, D_GROUP), lambda b, i: (b, 0, i, 0)))
            out_shape.append(jax.ShapeDtypeStruct((B, r, L // r, D_GROUP), BF16))
    out_specs.append(pl.BlockSpec((1, tm, D_MEM), lambda b, i: (b, i, 0)))
    out_shape.append(jax.ShapeDtypeStruct((B, L, D_MEM), BF16))
    return pl.pallas_call(
        _inproj_kernel,
        grid=(B, L // tm),
        in_specs=[pl.BlockSpec((1, tm, D_MODEL), lambda b, i: (b, i, 0)),
                  _const_spec((1, D_MODEL))]
                 + [_w_in_spec(n, layer) for n in ("u", "q", "k", "v", "q_mem")],
        out_specs=out_specs,
        out_shape=out_shape,
        scratch_shapes=[pltpu.VMEM((D_SSM // LANES, tm, LANES), F32)],
        compiler_params=pltpu.CompilerParams(vmem_limit_bytes=VMEM_LIMIT),
        name="in_proj",
    )(x, g, *([w_in_bf] * 5))


def _cmul(ar, ai, br, bi):
    return ar * br - ai * bi, ar * bi + ai * br


def _ssm_prep(lre, lim, log_dt, b_re, b_im, c_re, c_im, d):
    hp = lax.Precision.HIGHEST
    G, P, H, C = SSM_GROUPS, SSM_STATE, SSM_GROUP, SSM_CHUNK
    dt = jnp.exp(log_dt)[:, None]
    mag = jnp.exp(lre * dt)
    ar, ai = mag * jnp.cos(lim * dt), mag * jnp.sin(lim * dt)
    den = lre * lre + lim * lim
    nr, ni = ar - 1.0, ai
    fr = (nr * lre + ni * lim) / den
    fi = (ni * lre - nr * lim) / den
    bbr = fr[..., None] * b_re - fi[..., None] * b_im
    bbi = fr[..., None] * b_im + fi[..., None] * b_re
    prs, pis = [jnp.ones_like(ar)], [jnp.zeros_like(ai)]
    for _ in range(C):
        r_, i_ = _cmul(prs[-1], pis[-1], ar, ai)
        prs.append(r_)
        pis.append(i_)
    PR, PI = jnp.stack(prs), jnp.stack(pis)
    wr = PR[:C, :, :, None] * bbr - PI[:C, :, :, None] * bbi
    wi = PR[:C, :, :, None] * bbi + PI[:C, :, :, None] * bbr
    kk = (jnp.einsum('ghp,tgpk->tghk', c_re, wr, precision=hp)
          - jnp.einsum('ghp,tgpk->tghk', c_im, wi, precision=hp))
    kk = kk.at[0].add(jnp.eye(H, dtype=F32)[None] * d.reshape(G, H)[:, :, None])
    NP, PT = G // 2, SSM_PAIRS_PER_TILE
    pair_eye = jnp.eye(2, dtype=F32)
    lag = jnp.arange(C)[None, :] - jnp.arange(C)[:, None]
    hit = lag[None, :, :] == jnp.arange(C)[:, None, None]
    kt = kk.transpose(0, 1, 3, 2)
    m_g = jnp.sum(jnp.where(hit[:, None, :, None, :, None], kt[:, :, None, :, None, :], 0.0), axis=0)
    m_g = m_g.reshape(NP, 2, LANES, 1, LANES)
    m_q = (m_g * pair_eye[None, :, None, :, None]).reshape(NP, 2 * LANES, 2 * LANES)
    sw = jnp.stack([wr[::-1], wi[::-1]])
    s_g = sw.transpose(2, 1, 4, 0, 3).reshape(NP, 2, LANES, 2, 1, P)
    s_q = (s_g * pair_eye[None, :, None, None, :, None]).reshape(NP, 2 * LANES, 2 * LANES)
    cr = c_re[None] * PR[1:, :, None, :] - c_im[None] * PI[1:, :, None, :]
    ci = c_re[None] * PI[1:, :, None, :] + c_im[None] * PR[1:, :, None, :]
    rw = jnp.stack([cr, -ci])
    r_g = rw.transpose(2, 0, 4, 1, 3).reshape(NP, 2, 2, P, 1, LANES)
    r_q = (r_g.transpose(0, 2, 1, 3, 4, 5) * pair_eye[None, None, :, None, :, None]
           ).reshape(NP, 2 * LANES, 2 * LANES)
    alr, ali = PR[C], PI[C]
    qrs, qis = [jnp.ones_like(alr)], [jnp.zeros_like(ali)]
    for _ in range(SUBLANES):
        r_, i_ = _cmul(qrs[-1], qis[-1], alr, ali)
        qrs.append(r_)
        qis.append(i_)

    def lay(zr, zi):
        z = jnp.stack([zr.reshape(SSM_TILES, PT, LANES), zi.reshape(SSM_TILES, PT, LANES)], axis=2)
        return z.reshape(SSM_TILES, SSM_TILE_W)

    rows = jnp.arange(SUBLANES)[:, None, None]
    tabs = []
    for dsh in (1, 2, 4):
        full = jnp.broadcast_to(lay(qrs[dsh], qis[dsh])[None], (SUBLANES, SSM_TILES, SSM_TILE_W))
        tabs.append(jnp.where(rows >= dsh, full, 0.0))
    tabs.append(jnp.stack([lay(qrs[i], qis[i]) for i in range(SUBLANES)]))
    tabs.append(jnp.broadcast_to(lay(qrs[SUBLANES], qis[SUBLANES])[None], (SUBLANES, SSM_TILES, SSM_TILE_W)))
    tab = jnp.stack(tabs).transpose(2, 0, 1, 3)

    def tiles(a):
        return a.astype(BF16).reshape(SSM_TILES, PT, 2 * LANES, 2 * LANES)

    return tiles(m_q), tiles(s_q), tiles(r_q), tab.astype(F32)


def _ssm_kernel(u_ref, m_ref, s_ref, r_ref, tab_ref, y_ref, upd_ref, xin_ref):
    pw = 2 * LANES
    for q in range(SSM_PAIRS_PER_TILE):
        cols = slice(q * pw, (q + 1) * pw)
        upd_ref[:, cols] = jnp.dot(u_ref[:, cols], s_ref[q], preferred_element_type=F32)
    n_blocks = u_ref.shape[0] // SUBLANES
    row = lax.broadcasted_iota(jnp.int32, (SUBLANES, LANES), 0)

    def body(i, carry):
        r0 = pl.multiple_of(i * SUBLANES, SUBLANES)
        rows = pl.ds(r0, SUBLANES)
        out = []
        for q in range(SSM_PAIRS_PER_TILE):
            re, im = slice(q * pw, q * pw + LANES), slice(q * pw + LANES, (q + 1) * pw)
            er, ei = carry[2 * q], carry[2 * q + 1]
            vr, vi = upd_ref[rows, re], upd_ref[rows, im]
            for lvl, dsh in enumerate((1, 2, 4)):
                cr, ci = tab_ref[lvl, :, re], tab_ref[lvl, :, im]
                sr, si = pltpu.roll(vr, dsh, 0), pltpu.roll(vi, dsh, 0)
                vr, vi = vr + (cr * sr - ci * si), vi + (cr * si + ci * sr)
            sr = jnp.where(row == 0, 0.0, pltpu.roll(vr, 1, 0))
            si = jnp.where(row == 0, 0.0, pltpu.roll(vi, 1, 0))
            pr, pi_ = tab_ref[3, :, re], tab_ref[3, :, im]
            xin_ref[rows, re] = sr + (pr * er - pi_ * ei)
            xin_ref[rows, im] = si + (pr * ei + pi_ * er)
            a8r, a8i = tab_ref[4, :, re], tab_ref[4, :, im]
            lr = jnp.broadcast_to(vr[SUBLANES - 1:SUBLANES, :], (SUBLANES, LANES))
            li = jnp.broadcast_to(vi[SUBLANES - 1:SUBLANES, :], (SUBLANES, LANES))
            out += [a8r * er - a8i * ei + lr, a8r * ei + a8i * er + li]
        return tuple(out)

    zero = jnp.zeros((SUBLANES, LANES), F32)
    lax.fori_loop(0, n_blocks, body, (zero,) * (2 * SSM_PAIRS_PER_TILE))
    for q in range(SSM_PAIRS_PER_TILE):
        cols = slice(q * pw, (q + 1) * pw)
        y = jnp.dot(u_ref[:, cols], m_ref[q], preferred_element_type=F32)
        y_ref[:, cols] = y + jnp.dot(xin_ref[:, cols].astype(BF16), r_ref[q], preferred_element_type=F32)


def _ssm(u2, tables, layer, batch):
    rows = u2.shape[0] // batch
    w = SSM_TILE_W

    def tab_spec(a):
        nd = a.ndim - 2
        return pl.BlockSpec((None, None) + a.shape[2:], lambda j, b: (layer, j) + (0,) * nd)

    return pl.pallas_call(
        _ssm_kernel,
        grid=(SSM_TILES, batch),
        in_specs=[pl.BlockSpec((rows, w), lambda j, b: (b, j))] + [tab_spec(a) for a in tables],
        out_specs=pl.BlockSpec((rows, w), lambda j, b: (b, j)),
        out_shape=jax.ShapeDtypeStruct(u2.shape, F32),
        scratch_shapes=[pltpu.VMEM((rows, w), F32)] * 2,
        compiler_params=pltpu.CompilerParams(vmem_limit_bytes=VMEM_LIMIT),
        name="ssm",
    )(u2, *tables)


def _rel_bucket(dist):
    n = jnp.maximum(dist, 0)
    max_exact = NUM_BUCKETS // 2
    n_f = jnp.maximum(n, 1).astype(F32)
    large = max_exact + (jnp.log(n_f / max_exact) / math.log(REL_MAX_DISTANCE / max_exact)
                         * (NUM_BUCKETS - max_exact)).astype(jnp.int32)
    large = jnp.minimum(large, NUM_BUCKETS - 1)
    return jnp.where(n < max_exact, n, large)


def _bias_mask(rel_bias_g, window, dilation):
    span = window // dilation
    qi = jnp.arange(ATTN_BLOCK)[:, None]
    kj = jnp.arange(2 * ATTN_BLOCK)[None, :]
    delta = ATTN_BLOCK + qi - kj
    band = (delta >= 0) & (delta <= span)
    bucket = _rel_bucket(jnp.maximum(delta, 0) * dilation)
    hit = bucket[None, None] == jnp.arange(NUM_BUCKETS)[:, None, None, None]
    bias = jnp.sum(jnp.where(hit, rel_bias_g.astype(F32)[:, :, None, None], 0.0), axis=0)
    return jnp.where(band[None], bias, NEG_INF)


def _attn_kernel(q_ref, kc_ref, kp_ref, vc_ref, vp_ref, bm_ref, o_ref, lse_ref):
    blk = ATTN_BLOCK
    first_valid_col = jnp.where(pl.program_id(1) == 0, blk, 0)
    lane = lax.broadcasted_iota(jnp.int32, (blk, LANES), 1)
    low = lane < ATTN_HEAD_DIM
    col = lax.broadcasted_iota(jnp.int32, (blk, 2 * blk), 1)
    lane_w = lax.broadcasted_iota(jnp.int32, (blk, LANES), 1)
    n_sub = q_ref.shape[1] // blk
    dn = (((1,), (1,)), ((), ()))
    for n in range(n_sub):
        rows = slice(n * blk, (n + 1) * blk)
        lse_w = jnp.zeros((blk, LANES), F32)
        for pair in range(ATTN_HEADS_PER_GROUP // 2):
            cols = slice(pair * LANES, (pair + 1) * LANES)
            q32 = q_ref[0, rows, cols].astype(F32)
            if n == 0:
                kk = jnp.concatenate([kp_ref[0, :, cols], kc_ref[0, rows, cols]], axis=0)
                vv = jnp.concatenate([vp_ref[0, :, cols], vc_ref[0, rows, cols]], axis=0)
            else:
                kk = kc_ref[0, (n - 1) * blk:(n + 1) * blk, cols]
                vv = vc_ref[0, (n - 1) * blk:(n + 1) * blk, cols]
            outs = []
            for sub in range(2):
                hh = 2 * pair + sub
                qh = jnp.where(low if sub == 0 else ~low, q32, 0.0).astype(BF16)
                s = lax.dot_general(qh, kk, dn, preferred_element_type=F32) + bm_ref[hh]
                if n == 0:
                    s = jnp.where(col >= first_valid_col, s, NEG_INF)
                m = jnp.max(s, axis=-1, keepdims=True)
                p = jnp.exp(s - m)
                l = jnp.sum(p, axis=-1, keepdims=True)
                o = jnp.dot(p.astype(BF16), vv, preferred_element_type=F32) * (1.0 / l)
                outs.append(o)
                lse_w = jnp.where(lane_w == hh, m + jnp.log(l), lse_w)
            o_ref[0, rows, cols] = jnp.where(low, outs[0], outs[1]).astype(BF16)
        lse_ref[0, rows, :] = lse_w


def _attention(q, k, v, bias_mask, col_block):
    ns, m_len, _ = q.shape
    tq = min(TQ_ATTN, m_len)
    per = tq // ATTN_BLOCK
    cur = pl.BlockSpec((1, tq, D_GROUP), lambda s, i: (s, i, col_block))
    prev = pl.BlockSpec((1, ATTN_BLOCK, D_GROUP), lambda s, i: (s, jnp.maximum(i * per - 1, 0), col_block))
    return pl.pallas_call(
        _attn_kernel,
        grid=(ns, m_len // tq),
        in_specs=[cur, cur, prev, cur, prev,
                  _const_spec((ATTN_HEADS_PER_GROUP, ATTN_BLOCK, 2 * ATTN_BLOCK))],
        out_specs=[pl.BlockSpec((1, tq, D_GROUP), lambda s, i: (s, i, 0)),
                   pl.BlockSpec((1, tq, LANES), lambda s, i: (s, i, 0))],
        out_shape=[jax.ShapeDtypeStruct((ns, m_len, D_GROUP), BF16),
                   jax.ShapeDtypeStruct((ns, m_len, LANES), F32)],
        name="attn",
    )(q, k, k, v, v, bias_mask)


def _merge_kernel(final, x_ref, y2_ref, o0_ref, o1_ref, o2_ref, l0_ref, l1_ref, l2_ref, qm_ref, km_ref, vm_ref,
                  g_ref, wzs_ref, wza_ref, wzm_ref, wg0_ref, wg1_ref, wg2_ref, wg3_ref, wg4_ref, wg5_ref,
                  bg_ref, wglu_ref, bglu_ref, wbs_ref, wba_ref, wbm_ref, wout_ref, fg_ref, out_ref,
                  y_scr, o_scr, l_scr):
    x = x_ref[0]
    tm = x.shape[0]
    h = _rms(x, g_ref[...]).astype(BF16)

    def hdot(w_ref):
        return jnp.dot(h, w_ref[...], preferred_element_type=F32)

    for j in range(SSM_TILES):
        groups = [y2_ref[0, :, j * SSM_TILE_W + gl * LANES:j * SSM_TILE_W + (gl + 1) * LANES]
                  for gl in range(SSM_TILE_GROUPS)]
        for t, blk in enumerate(_block_transpose(groups)):
            y_scr[j, pl.ds(t, tm // SSM_CHUNK, stride=SSM_CHUNK), :] = blk
    tiles = D_GROUP // LANES
    for gi, (o_ref, l_ref) in enumerate(((o0_ref, l0_ref), (o1_ref, l1_ref), (o2_ref, l2_ref))):
        r = ATTN_CONFIGS[gi][1]
        for s in range(r):
            rows = pl.ds(s, tm // r, stride=r)
            for c in range(tiles):
                o_scr[gi * tiles + c, rows, :] = o_ref[0, s, :, c * LANES:(c + 1) * LANES].astype(F32)
            l_scr[gi, rows, :] = l_ref[0, s]

    yg = jax.nn.gelu(jnp.concatenate([y_scr[j] for j in range(SSM_TILES)], axis=-1))
    t = jnp.dot(yg.astype(BF16), wglu_ref[...], preferred_element_type=F32) + bglu_ref[...]
    o_ssm = yg * jax.nn.sigmoid(t) * jax.nn.silu(hdot(wzs_ref))
    p_ssm = jnp.dot(o_ssm.astype(BF16), wbs_ref[...], preferred_element_type=F32)

    ls = (l_scr[0], l_scr[1], l_scr[2])
    mx = jnp.maximum(jnp.maximum(ls[0], ls[1]), ls[2])
    es = [jnp.exp(l - mx) for l in ls]
    inv = 1.0 / (es[0] + es[1] + es[2])
    head_of_lane = lax.broadcasted_iota(jnp.int32, (tm, D_GROUP), 1) // ATTN_HEAD_DIM
    parts = []
    for gi, e in enumerate(es):
        alpha = e * inv
        wide = jnp.zeros((tm, D_GROUP), F32)
        for j in range(ATTN_HEADS_PER_GROUP):
            wide = jnp.where(head_of_lane == j, alpha[:, j:j + 1], wide)
        o_g = jnp.concatenate([o_scr[gi * tiles + c] for c in range(tiles)], axis=-1)
        parts.append(o_g * wide)
    o_attn = jnp.concatenate(parts, axis=-1) * jax.nn.silu(hdot(wza_ref))
    p_attn = jnp.dot(o_attn.astype(BF16), wba_ref[...], preferred_element_type=F32)

    dn = (((1,), (1,)), ((), ()))
    heads = []
    for hd in range(MEM_HEADS):
        cols = slice(hd * MEM_HEAD_DIM, (hd + 1) * MEM_HEAD_DIM)
        s = lax.dot_general(qm_ref[0, :, cols], km_ref[0, :, cols], dn, preferred_element_type=F32)
        s = s * (MEM_HEAD_DIM ** -0.5)
        m = jnp.max(s, axis=-1, keepdims=True)
        p = jnp.exp(s - m)
        l = jnp.sum(p, axis=-1, keepdims=True)
        heads.append(jnp.dot(p.astype(BF16), vm_ref[0, :, cols], preferred_element_type=F32) * (1.0 / l))
    o_mem = jnp.concatenate(heads, axis=-1) * jax.nn.silu(hdot(wzm_ref))
    p_mem = jnp.dot(o_mem.astype(BF16), wbm_ref[...], preferred_element_type=F32)

    gate_refs = (wg0_ref, wg1_ref, wg2_ref, wg3_ref, wg4_ref, wg5_ref)
    per_branch = D_MODEL // GATE_W
    halves = []
    for part in range(per_branch):
        acc = jnp.zeros((tm, GATE_W), F32)
        for br, p_br in enumerate((p_ssm, p_attn, p_mem)):
            k = br * per_branch + part
            gate = jax.nn.sigmoid(hdot(gate_refs[k]) + bg_ref[:, k * GATE_W:(k + 1) * GATE_W])
            acc = acc + gate * p_br[:, part * GATE_W:(part + 1) * GATE_W]
        halves.append(acc)
    merged = jnp.concatenate(halves, axis=-1)
    xn = x + jnp.dot(merged.astype(BF16), wout_ref[...], preferred_element_type=F32)
    if final:
        xn = _rms(xn, fg_ref[...])
    out_ref[0] = xn


def _merge(final, layer, x, y2, o_groups, lse_groups, qm, k_mem, v_mem, g, w_in_bf, bg, wglu, bglu, wbs, wba,
           wbm, wout, fg):
    B, L, _ = x.shape
    tm = TM_MERGE

    def rows(w):
        return pl.BlockSpec((1, tm, w), lambda b, i: (b, i, 0))

    def dec(r, w):
        return pl.BlockSpec((1, r, tm // r, w), lambda b, i: (b, 0, i, 0))

    mem_spec = pl.BlockSpec((1,) + k_mem.shape[1:], lambda b, i: (b, 0, 0))
    rs = [r for _, r in ATTN_CONFIGS]
    n_gate = N_BRANCHES * D_MODEL // GATE_W
    in_specs = ([rows(D_MODEL), pl.BlockSpec((1, tm // SSM_CHUNK, D_SSM * SSM_CHUNK), lambda b, i: (b, i, 0))]
                + [dec(r, D_GROUP) for r in rs] + [dec(r, LANES) for r in rs]
                + [rows(D_MEM), mem_spec, mem_spec, _const_spec(g.shape)]
                + [_w_in_spec(n, layer) for n in ("z_ssm", "z_attn", "z_mem")]
                + [_w_in_spec("gates", layer, part) for part in range(n_gate)]
                + [_const_spec(a.shape) for a in (bg, wglu, bglu, wbs, wba, wbm, wout, fg)])
    return pl.pallas_call(
        functools.partial(_merge_kernel, final),
        grid=(B, L // tm),
        in_specs=in_specs,
        out_specs=rows(D_MODEL),
        out_shape=jax.ShapeDtypeStruct((B, L, D_MODEL), F32),
        scratch_shapes=[pltpu.VMEM((D_SSM // LANES, tm, LANES), F32), pltpu.VMEM((D_ATTN // LANES, tm, LANES), F32),
                        pltpu.VMEM((len(rs), tm, LANES), F32)],
        compiler_params=pltpu.CompilerParams(vmem_limit_bytes=VMEM_LIMIT),
        name="merge",
    )(x, y2, *o_groups, *lse_groups, qm, k_mem, v_mem, g, *([w_in_bf] * (3 + n_gate)), bg, wglu, bglu, wbs, wba,
      wbm, wout, fg)


def kernel(x, mem, norm_g, mem_norm_g, w_in, b_gate, ssm_lambda_re, ssm_lambda_im, ssm_log_dt, ssm_b_re,
           ssm_b_im, ssm_c_re, ssm_c_im, ssm_d, w_glu, b_glu, w_mem_kv, w_br_ssm, w_br_attn, w_br_mem,
           w_out, rel_bias, final_norm_g):
    B, L, _ = x.shape
    assert L % (ATTN_CONFIGS[-1][1] * ATTN_BLOCK) == 0 and L % TM_INPROJ == 0 and L % TM_MERGE == 0
    bias_masks = [_bias_mask(rel_bias[:, gi * ATTN_HEADS_PER_GROUP:(gi + 1) * ATTN_HEADS_PER_GROUP], win, dil)
                  for gi, (win, dil) in enumerate(ATTN_CONFIGS)]
    fg = final_norm_g.reshape(1, D_MODEL)
    w_in_bf = w_in.astype(BF16)
    per_layer = [_ssm_prep(ssm_lambda_re[i], ssm_lambda_im[i], ssm_log_dt[i], ssm_b_re[i], ssm_b_im[i],
                           ssm_c_re[i], ssm_c_im[i], ssm_d[i]) for i in range(DEPTH)]
    ssm_tables = [jnp.stack(t) for t in zip(*per_layer)]
    n_chunks = B * L // SSM_CHUNK
    for layer in range(DEPTH):
        g = norm_g[layer].reshape(1, D_MODEL)
        k_mem, v_mem = _mem_kv(mem, mem_norm_g[layer].reshape(1, D_MODEL), w_mem_kv[layer].astype(BF16))
        u2, *qkv, qm = _in_proj(x, g, w_in_bf, layer)

        y2 = _ssm(u2.reshape(n_chunks, D_SSM * SSM_CHUNK), ssm_tables, layer, batch=B)
        y2 = y2.reshape(B, L // SSM_CHUNK, D_SSM * SSM_CHUNK)

        o_groups, lse_groups = [], []
        for gi, (_, r) in enumerate(ATTN_CONFIGS):
            m_len = L // r
            q_g, k_g, v_g = (qkv[3 * idx + gi].reshape(B * r, m_len, D_GROUP) for idx in range(3))
            o_g, lse_g = _attention(q_g, k_g, v_g, bias_masks[gi], 0)
            o_groups.append(o_g.reshape(B, r, m_len, D_GROUP))
            lse_groups.append(lse_g.reshape(B, r, m_len, LANES))

        x = _merge(layer == DEPTH - 1, layer, x, y2, o_groups, lse_groups, qm, k_mem, v_mem, g, w_in_bf,
                   b_gate[layer].reshape(1, -1), w_glu[layer].astype(BF16), b_glu[layer].reshape(1, -1),
                   w_br_ssm[layer].astype(BF16), w_br_attn[layer].astype(BF16), w_br_mem[layer].astype(BF16),
                   w_out[layer].astype(BF16), fg)
    return x
```

```python
import functools
import math

import jax
import jax.numpy as jnp
import numpy as np
from jax import lax
from jax.experimental import pallas as pl
from jax.experimental.pallas import tpu as pltpu

F32 = jnp.float32
BF16 = jnp.bfloat16

D_MODEL = 1024
DEPTH = 2
EPS = 1e-6
N_BRANCHES = 3
D_SSM = 768
SSM_GROUP = 16
SSM_GROUPS = 48
SSM_STATE = 64
ATTN_HEAD_DIM = 64
ATTN_HEADS_PER_GROUP = 4
ATTN_CONFIGS = ((128, 1), (512, 4), (2048, 16))
N_ATTN_HEADS = 12
D_ATTN = 768
ATTN_BLOCK = 128
NUM_BUCKETS = 32
REL_MAX_DISTANCE = 2048
NEG_INF = -1e30
MEM_HEADS = 4
MEM_HEAD_DIM = 128
D_MEM = 512
D_GROUP = ATTN_HEADS_PER_GROUP * ATTN_HEAD_DIM

LANES = 128
SUBLANES = 8
SSM_CHUNK = SUBLANES
SSM_TILE_GROUPS = LANES // SSM_GROUP
SSM_TILES = D_SSM // LANES
SSM_TILE_W = SSM_CHUNK * LANES
SSM_PAIRS_PER_TILE = SSM_TILE_GROUPS // 2
VMEM_LIMIT = 56 * 1024 * 1024

TM_INPROJ = 1024
TM_MERGE = 512
TQ_ATTN = 512


def _rms(x, g):
    return x * lax.rsqrt(jnp.mean(x * x, axis=-1, keepdims=True) + EPS) * g


def _block_transpose(vs):
    n = len(vs)
    width = LANES // n
    block = lax.broadcasted_iota(jnp.int32, vs[0].shape, 1) // width
    d = n // 2
    while d >= 1:
        bit_set = (block & d) != 0
        new = list(vs)
        for i in range(n):
            if i & d == 0:
                a, b = vs[i], vs[i + d]
                new[i] = jnp.where(bit_set, pltpu.roll(b, d * width, 1), a)
                new[i + d] = jnp.where(bit_set, b, pltpu.roll(a, LANES - d * width, 1))
        vs = new
        d //= 2
    return vs


def _const_spec(shape):
    n = len(shape)
    return pl.BlockSpec(shape, lambda *_: (0,) * n, pipeline_mode=pl.Buffered(1))


def _memkv_kernel(mem_ref, g_ref, w_ref, k_ref, v_ref):
    h = _rms(mem_ref[0], g_ref[...]).astype(BF16)
    k_ref[0] = jnp.dot(h, w_ref[:, :D_MEM], preferred_element_type=F32).astype(BF16)
    v_ref[0] = jnp.dot(h, w_ref[:, D_MEM:], preferred_element_type=F32).astype(BF16)


def _mem_kv(mem, g, w_bf16):
    B, ML, _ = mem.shape
    return pl.pallas_call(
        _memkv_kernel,
        grid=(B,),
        in_specs=[pl.BlockSpec((1, ML, D_MODEL), lambda b: (b, 0, 0)),
                  _const_spec((1, D_MODEL)),
                  _const_spec((D_MODEL, 2 * D_MEM))],
        out_specs=[pl.BlockSpec((1, ML, D_MEM), lambda b: (b, 0, 0)),
                   pl.BlockSpec((1, ML, D_MEM), lambda b: (b, 0, 0))],
        out_shape=[jax.ShapeDtypeStruct((B, ML, D_MEM), BF16)] * 2,
        name="mem_kv",
    )(mem, g, w_bf16)


_IN_SIZES = (D_SSM, D_SSM, D_ATTN, D_ATTN, D_ATTN, D_ATTN, D_MEM, D_MEM, N_BRANCHES * D_MODEL)
_IN_OFFS = tuple(int(v) for v in np.concatenate([[0], np.cumsum(_IN_SIZES)]))
_IN_NAMES = ("u", "z_ssm", "q", "k", "v", "z_attn", "q_mem", "z_mem", "gates")
GATE_W = 512


def _w_in_spec(name, layer, part=0):
    idx = _IN_NAMES.index(name)
    width = GATE_W if name == "gates" else _IN_SIZES[idx]
    block, rem = divmod(_IN_OFFS[idx], width)
    assert rem == 0
    return pl.BlockSpec((None, D_MODEL, width), lambda *_: (layer, 0, block + part),
                        pipeline_mode=pl.Buffered(1))


def _inproj_kernel(x_ref, g_ref, wu_ref, wq_ref, wk_ref, wv_ref, wqm_ref, u2_ref, *rest):
    qkv_refs, qm_ref, scr = rest[:9], rest[9], rest[10]
    h = _rms(x_ref[0], g_ref[...]).astype(BF16)
    tm = h.shape[0]
    def to_scratch(p):
        for j in range(p.shape[1] // LANES):
            scr[j] = p[:, j * LANES:(j + 1) * LANES]

    to_scratch(jnp.dot(h, wu_ref[...], preferred_element_type=F32))
    for j in range(SSM_TILES):
        steps = [scr[j, pl.ds(s, tm // SSM_CHUNK, stride=SSM_CHUNK), :] for s in range(SSM_CHUNK)]
        for gl, blk in enumerate(_block_transpose(steps)):
            lo = j * SSM_TILE_W + gl * LANES
            u2_ref[0, :, lo:lo + LANES] = blk.astype(BF16)
    for idx, (w_ref, scale) in enumerate(((wq_ref, ATTN_HEAD_DIM ** -0.5), (wk_ref, None), (wv_ref, None))):
        p = jnp.dot(h, w_ref[...], preferred_element_type=F32)
        if scale is not None:
            p = p * scale
        to_scratch(p)
        tiles = D_GROUP // LANES
        for gi, (_, r) in enumerate(ATTN_CONFIGS):
            o_ref = qkv_refs[3 * idx + gi]
            for s in range(r):
                for c in range(tiles):
                    piece = scr[gi * tiles + c, pl.ds(s, tm // r, stride=r), :]
                    o_ref[0, s, :, c * LANES:(c + 1) * LANES] = piece.astype(BF16)
    qm_ref[0] = jnp.dot(h, wqm_ref[...], preferred_element_type=F32).astype(BF16)


def _in_proj(x, g, w_in_bf, layer):
    B, L, _ = x.shape
    tm = TM_INPROJ
    out_specs = [pl.BlockSpec((1, tm // SSM_CHUNK, D_SSM * SSM_CHUNK), lambda b, i: (b, i, 0))]
    out_shape = [jax.ShapeDtypeStruct((B, L // SSM_CHUNK, D_SSM * SSM_CHUNK), BF16)]
    for _ in range(3):
        for _, r in ATTN_CONFIGS:
            out_specs.append(pl.BlockSpec((1, r, tm // r, D_GROUP), lambda b, i: (b, 0, i, 0)))
            out_shape.append(jax.ShapeDtypeStruct((B, r, L // r, D_GROUP), BF16))
    out_specs.append(pl.BlockSpec((1, tm, D_MEM), lambda b, i: (b, i, 0)))
    out_shape.append(jax.ShapeDtypeStruct((B, L, D_MEM), BF16))
    return pl.pallas_call(
        _inproj_kernel,
        grid=(B, L // tm),
        in_specs=[pl.BlockSpec((1, tm, D_MODEL), lambda b, i: (b, i, 0)),
                  _const_spec((1, D_MODEL))]
                 + [_w_in_spec(n, layer) for n in ("u", "q", "k", "v", "q_mem")],
        out_specs=out_specs,
        out_shape=out_shape,
        scratch_shapes=[pltpu.VMEM((D_SSM // LANES, tm, LANES), F32)],
        compiler_params=pltpu.CompilerParams(vmem_limit_bytes=VMEM_LIMIT),
        name="in_proj",
    )(x, g, *([w_in_bf] * 5))


def _cmul(ar, ai, br, bi):
    return ar * br - ai * bi, ar * bi + ai * br


def _ssm_prep(lre, lim, log_dt, b_re, b_im, c_re, c_im, d):
    hp = lax.Precision.HIGHEST
    G, P, H, C = SSM_GROUPS, SSM_STATE, SSM_GROUP, SSM_CHUNK
    dt = jnp.exp(log_dt)[:, None]
    mag = jnp.exp(lre * dt)
    ar, ai = mag * jnp.cos(lim * dt), mag * jnp.sin(lim * dt)
    den = lre * lre + lim * lim
    nr, ni = ar - 1.0, ai
    fr = (nr * lre + ni * lim) / den
    fi = (ni * lre - nr * lim) / den
    bbr = fr[..., None] * b_re - fi[..., None] * b_im
    bbi = fr[..., None] * b_im + fi[..., None] * b_re
    prs, pis = [jnp.ones_like(ar)], [jnp.zeros_like(ai)]
    for _ in range(C):
        r_, i_ = _cmul(prs[-1], pis[-1], ar, ai)
        prs.append(r_)
        pis.append(i_)
    PR, PI = jnp.stack(prs), jnp.stack(pis)
    wr = PR[:C, :, :, None] * bbr - PI[:C, :, :, None] * bbi
    wi = PR[:C, :, :, None] * bbi + PI[:C, :, :, None] * bbr
    kk = (jnp.einsum('ghp,tgpk->tghk', c_re, wr, precision=hp)
          - jnp.einsum('ghp,tgpk->tghk', c_im, wi, precision=hp))
    kk = kk.at[0].add(jnp.eye(H, dtype=F32)[None] * d.reshape(G, H)[:, :, None])
    PT = SSM_PAIRS_PER_TILE
    k_row = kk.transpose(1, 3, 0, 2).reshape(G, H, LANES)
    sw = jnp.stack([wr[::-1], wi[::-1]])
    s_g = sw.transpose(2, 1, 4, 0, 3).reshape(G, LANES, LANES)
    cr = c_re[None] * PR[1:, :, None, :] - c_im[None] * PI[1:, :, None, :]
    ci = c_re[None] * PI[1:, :, None, :] + c_im[None] * PR[1:, :, None, :]
    rw = jnp.stack([cr, -ci])
    r_g = rw.transpose(2, 0, 4, 1, 3).reshape(G, LANES, LANES)
    alr, ali = PR[C], PI[C]
    qrs, qis = [jnp.ones_like(alr)], [jnp.zeros_like(ali)]
    for _ in range(SUBLANES):
        r_, i_ = _cmul(qrs[-1], qis[-1], alr, ali)
        qrs.append(r_)
        qis.append(i_)

    def lay(zr, zi):
        z = jnp.stack([zr.reshape(SSM_TILES, PT, LANES), zi.reshape(SSM_TILES, PT, LANES)], axis=2)
        return z.reshape(SSM_TILES, SSM_TILE_W)

    rows = jnp.arange(SUBLANES)[:, None, None]
    tabs = []
    for dsh in (1, 2, 4):
        full = jnp.broadcast_to(lay(qrs[dsh], qis[dsh])[None], (SUBLANES, SSM_TILES, SSM_TILE_W))
        tabs.append(jnp.where(rows >= dsh, full, 0.0))
    tabs.append(jnp.stack([lay(qrs[i], qis[i]) for i in range(SUBLANES)]))
    tabs.append(jnp.broadcast_to(lay(qrs[SUBLANES], qis[SUBLANES])[None], (SUBLANES, SSM_TILES, SSM_TILE_W)))
    tab = jnp.stack(tabs).transpose(2, 0, 1, 3)

    def per_tile(a):
        return a.reshape((SSM_TILES, SSM_TILE_GROUPS) + a.shape[1:])

    return per_tile(k_row), per_tile(s_g), per_tile(r_g), tab.astype(F32)


def _ssm_expand(k_ref, s_ref, r_ref, m_scr, s_scr, r_scr):
    H, P = SSM_GROUP, SSM_STATE
    lane_k = lax.broadcasted_iota(jnp.int32, (H, LANES), 1)
    low = lax.broadcasted_iota(jnp.int32, (LANES, LANES), 1) < P
    m_scr[...] = jnp.zeros(m_scr.shape, m_scr.dtype)
    r_scr[...] = jnp.zeros(r_scr.shape, r_scr.dtype)
    for gl in range(SSM_TILE_GROUPS):
        q, gl2 = divmod(gl, 2)
        own = slice(gl2 * LANES, (gl2 + 1) * LANES)
        k_row = k_ref[gl]
        for s in range(SSM_CHUNK):
            blk = jnp.where(lane_k >= s * H, pltpu.roll(k_row, s * H, 1), 0.0) if s else k_row
            m_scr[q, gl2 * LANES + s * H:gl2 * LANES + (s + 1) * H, own] = blk.astype(BF16)
        s_g = s_ref[gl]
        swapped = pltpu.roll(s_g, P, 1)
        mine = low if gl2 == 0 else ~low
        s_scr[q, own, 0:LANES] = jnp.where(mine, s_g if gl2 == 0 else swapped, 0.0).astype(BF16)
        s_scr[q, own, LANES:2 * LANES] = jnp.where(mine, swapped if gl2 == 0 else s_g, 0.0).astype(BF16)
        for r in range(2):
            rows = slice(r * LANES + gl2 * P, r * LANES + (gl2 + 1) * P)
            r_scr[q, rows, own] = r_ref[gl, r * P:(r + 1) * P, :].astype(BF16)


def _ssm_kernel(u_ref, k_ref, sg_ref, rg_ref, tab_ref, y_ref, m_scr, s_scr, r_scr, upd_ref, xin_ref):
    @pl.when(pl.program_id(1) == 0)
    def _():
        _ssm_expand(k_ref, sg_ref, rg_ref, m_scr, s_scr, r_scr)

    pw = 2 * LANES
    for q in range(SSM_PAIRS_PER_TILE):
        cols = slice(q * pw, (q + 1) * pw)
        upd_ref[:, cols] = jnp.dot(u_ref[:, cols], s_scr[q], preferred_element_type=F32)
    n_blocks = u_ref.shape[0] // SUBLANES
    row = lax.broadcasted_iota(jnp.int32, (SUBLANES, LANES), 0)

    def scan(i, carry):
        r0 = pl.multiple_of(i * SUBLANES, SUBLANES)
        rows = pl.ds(r0, SUBLANES)
        out = []
        for q in range(SSM_PAIRS_PER_TILE):
            re, im = slice(q * pw, q * pw + LANES), slice(q * pw + LANES, (q + 1) * pw)
            er, ei = carry[2 * q], carry[2 * q + 1]
            vr, vi = upd_ref[rows, re], upd_ref[rows, im]
            for lvl, dsh in enumerate((1, 2, 4)):
                cr, ci = tab_ref[lvl, :, re], tab_ref[lvl, :, im]
                sr, si = pltpu.roll(vr, dsh, 0), pltpu.roll(vi, dsh, 0)
                vr, vi = vr + (cr * sr - ci * si), vi + (cr * si + ci * sr)
            sr = jnp.where(row == 0, 0.0, pltpu.roll(vr, 1, 0))
            si = jnp.where(row == 0, 0.0, pltpu.roll(vi, 1, 0))
            pr, pi_ = tab_ref[3, :, re], tab_ref[3, :, im]
            xin_ref[rows, re] = sr + (pr * er - pi_ * ei)
            xin_ref[rows, im] = si + (pr * ei + pi_ * er)
            a8r, a8i = tab_ref[4, :, re], tab_ref[4, :, im]
            lr = jnp.broadcast_to(vr[SUBLANES - 1:SUBLANES, :], (SUBLANES, LANES))
            li = jnp.broadcast_to(vi[SUBLANES - 1:SUBLANES, :], (SUBLANES, LANES))
            out += [a8r * er - a8i * ei + lr, a8r * ei + a8i * er + li]
        return tuple(out)

    zero = (jnp.zeros((SUBLANES, LANES), F32),) * (2 * SSM_PAIRS_PER_TILE)

    lax.fori_loop(0, n_blocks, scan, zero)
    for q in range(SSM_PAIRS_PER_TILE):
        cols = slice(q * pw, (q + 1) * pw)
        y = jnp.dot(u_ref[:, cols], m_scr[q], preferred_element_type=F32)
        y_ref[:, cols] = y + jnp.dot(xin_ref[:, cols].astype(BF16), r_scr[q], preferred_element_type=F32)


def _ssm(u2, tables, layer, batch):
    rows = u2.shape[0] // batch
    w = SSM_TILE_W
    pw = 2 * LANES

    def tab_spec(a):
        nd = a.ndim - 2
        return pl.BlockSpec((None, None) + a.shape[2:], lambda j, b: (layer, j) + (0,) * nd)

    return pl.pallas_call(
        _ssm_kernel,
        grid=(SSM_TILES, batch),
        in_specs=[pl.BlockSpec((rows, w), lambda j, b: (b, j))] + [tab_spec(a) for a in tables],
        out_specs=pl.BlockSpec((rows, w), lambda j, b: (b, j)),
        out_shape=jax.ShapeDtypeStruct(u2.shape, F32),
        scratch_shapes=[pltpu.VMEM((SSM_PAIRS_PER_TILE, pw, pw), BF16)] * 3 + [pltpu.VMEM((rows, w), F32)] * 2,
        compiler_params=pltpu.CompilerParams(vmem_limit_bytes=VMEM_LIMIT),
        name="ssm",
    )(u2, *tables)


def _rel_bucket(dist):
    n = jnp.maximum(dist, 0)
    max_exact = NUM_BUCKETS // 2
    n_f = jnp.maximum(n, 1).astype(F32)
    large = max_exact + (jnp.log(n_f / max_exact) / math.log(REL_MAX_DISTANCE / max_exact)
                         * (NUM_BUCKETS - max_exact)).astype(jnp.int32)
    large = jnp.minimum(large, NUM_BUCKETS - 1)
    return jnp.where(n < max_exact, n, large)


def _bias_mask(rel_bias_g, window, dilation):
    span = window // dilation
    qi = jnp.arange(ATTN_BLOCK)[:, None]
    kj = jnp.arange(2 * ATTN_BLOCK)[None, :]
    delta = ATTN_BLOCK + qi - kj
    band = (delta >= 0) & (delta <= span)
    bucket = _rel_bucket(jnp.maximum(delta, 0) * dilation)
    hit = bucket[None, None] == jnp.arange(NUM_BUCKETS)[:, None, None, None]
    bias = jnp.sum(jnp.where(hit, rel_bias_g.astype(F32)[:, :, None, None], 0.0), axis=0)
    return jnp.where(band[None], bias, NEG_INF)


def _attn_kernel(q_ref, kc_ref, kp_ref, vc_ref, vp_ref, bm_ref, o_ref, lse_ref):
    blk = ATTN_BLOCK
    first_valid_col = jnp.where(pl.program_id(1) == 0, blk, 0)
    lane = lax.broadcasted_iota(jnp.int32, (blk, LANES), 1)
    low = lane < ATTN_HEAD_DIM
    col = lax.broadcasted_iota(jnp.int32, (blk, 2 * blk), 1)
    lane_w = lax.broadcasted_iota(jnp.int32, (blk, LANES), 1)
    n_sub = q_ref.shape[1] // blk
    dn = (((1,), (1,)), ((), ()))
    for n in range(n_sub):
        rows = slice(n * blk, (n + 1) * blk)
        lse_w = jnp.zeros((blk, LANES), F32)
        for pair in range(ATTN_HEADS_PER_GROUP // 2):
            cols = slice(pair * LANES, (pair + 1) * LANES)
            q32 = q_ref[0, rows, cols].astype(F32)
            if n == 0:
                kk = jnp.concatenate([kp_ref[0, :, cols], kc_ref[0, rows, cols]], axis=0)
                vv = jnp.concatenate([vp_ref[0, :, cols], vc_ref[0, rows, cols]], axis=0)
            else:
                kk = kc_ref[0, (n - 1) * blk:(n + 1) * blk, cols]
                vv = vc_ref[0, (n - 1) * blk:(n + 1) * blk, cols]
            outs = []
            for sub in range(2):
                hh = 2 * pair + sub
                qh = jnp.where(low if sub == 0 else ~low, q32, 0.0).astype(BF16)
                s = lax.dot_general(qh, kk, dn, preferred_element_type=F32) + bm_ref[hh]
                if n == 0:
                    s = jnp.where(col >= first_valid_col, s, NEG_INF)
                m = jnp.max(s, axis=-1, keepdims=True)
                p = jnp.exp(s - m)
                l = jnp.sum(p, axis=-1, keepdims=True)
                o = jnp.dot(p.astype(BF16), vv, preferred_element_type=F32) * (1.0 / l)
                outs.append(o)
                lse_w = jnp.where(lane_w == hh, m + jnp.log(l), lse_w)
            o_ref[0, rows, cols] = jnp.where(low, outs[0], outs[1]).astype(BF16)
        lse_ref[0, rows, :] = lse_w


def _attention(q, k, v, bias_mask, col_block):
    ns, m_len, _ = q.shape
    tq = min(TQ_ATTN, m_len)
    per = tq // ATTN_BLOCK
    cur = pl.BlockSpec((1, tq, D_GROUP), lambda s, i: (s, i, col_block))
    prev = pl.BlockSpec((1, ATTN_BLOCK, D_GROUP), lambda s, i: (s, jnp.maximum(i * per - 1, 0), col_block))
    return pl.pallas_call(
        _attn_kernel,
        grid=(ns, m_len // tq),
        in_specs=[cur, cur, prev, cur, prev,
                  _const_spec((ATTN_HEADS_PER_GROUP, ATTN_BLOCK, 2 * ATTN_BLOCK))],
        out_specs=[pl.BlockSpec((1, tq, D_GROUP), lambda s, i: (s, i, 0)),
                   pl.BlockSpec((1, tq, LANES), lambda s, i: (s, i, 0))],
        out_shape=[jax.ShapeDtypeStruct((ns, m_len, D_GROUP), BF16),
                   jax.ShapeDtypeStruct((ns, m_len, LANES), F32)],
        name="attn",
    )(q, k, k, v, v, bias_mask)


def _merge_kernel(final, x_ref, y2_ref, o0_ref, o1_ref, o2_ref, l0_ref, l1_ref, l2_ref, qm_ref, km_ref, vm_ref,
                  g_ref, wzs_ref, wza_ref, wzm_ref, wg0_ref, wg1_ref, wg2_ref, wg3_ref, wg4_ref, wg5_ref,
                  bg_ref, wglu_ref, bglu_ref, wbs_ref, wba_ref, wbm_ref, wout_ref, fg_ref, out_ref,
                  y_scr, o_scr, l_scr):
    x = x_ref[0]
    tm = x.shape[0]
    h = _rms(x, g_ref[...]).astype(BF16)

    def hdot(w_ref):
        return jnp.dot(h, w_ref[...], preferred_element_type=F32)

    for j in range(SSM_TILES):
        groups = [y2_ref[0, :, j * SSM_TILE_W + gl * LANES:j * SSM_TILE_W + (gl + 1) * LANES]
                  for gl in range(SSM_TILE_GROUPS)]
        for t, blk in enumerate(_block_transpose(groups)):
            y_scr[j, pl.ds(t, tm // SSM_CHUNK, stride=SSM_CHUNK), :] = blk
    tiles = D_GROUP // LANES
    for gi, (o_ref, l_ref) in enumerate(((o0_ref, l0_ref), (o1_ref, l1_ref), (o2_ref, l2_ref))):
        r = ATTN_CONFIGS[gi][1]
        for s in range(r):
            rows = pl.ds(s, tm // r, stride=r)
            for c in range(tiles):
                o_scr[gi * tiles + c, rows, :] = o_ref[0, s, :, c * LANES:(c + 1) * LANES].astype(F32)
            l_scr[gi, rows, :] = l_ref[0, s]

    yg = jax.nn.gelu(jnp.concatenate([y_scr[j] for j in range(SSM_TILES)], axis=-1))
    t = jnp.dot(yg.astype(BF16), wglu_ref[...], preferred_element_type=F32) + bglu_ref[...]
    o_ssm = yg * jax.nn.sigmoid(t) * jax.nn.silu(hdot(wzs_ref))
    p_ssm = jnp.dot(o_ssm.astype(BF16), wbs_ref[...], preferred_element_type=F32)

    ls = (l_scr[0], l_scr[1], l_scr[2])
    mx = jnp.maximum(jnp.maximum(ls[0], ls[1]), ls[2])
    es = [jnp.exp(l - mx) for l in ls]
    inv = 1.0 / (es[0] + es[1] + es[2])
    head_of_lane = lax.broadcasted_iota(jnp.int32, (tm, D_GROUP), 1) // ATTN_HEAD_DIM
    parts = []
    for gi, e in enumerate(es):
        alpha = e * inv
        wide = jnp.zeros((tm, D_GROUP), F32)
        for j in range(ATTN_HEADS_PER_GROUP):
            wide = jnp.where(head_of_lane == j, alpha[:, j:j + 1], wide)
        o_g = jnp.concatenate([o_scr[gi * tiles + c] for c in range(tiles)], axis=-1)
        parts.append(o_g * wide)
    o_attn = jnp.concatenate(parts, axis=-1) * jax.nn.silu(hdot(wza_ref))
    p_attn = jnp.dot(o_attn.astype(BF16), wba_ref[...], preferred_element_type=F32)

    dn = (((1,), (1,)), ((), ()))
    heads = []
    for hd in range(MEM_HEADS):
        cols = slice(hd * MEM_HEAD_DIM, (hd + 1) * MEM_HEAD_DIM)
        s = lax.dot_general(qm_ref[0, :, cols], km_ref[0, :, cols], dn, preferred_element_type=F32)
        s = s * (MEM_HEAD_DIM ** -0.5)
        m = jnp.max(s, axis=-1, keepdims=True)
        p = jnp.exp(s - m)
        l = jnp.sum(p, axis=-1, keepdims=True)
        heads.append(jnp.dot(p.astype(BF16), vm_ref[0, :, cols], preferred_element_type=F32) * (1.0 / l))
    o_mem = jnp.concatenate(heads, axis=-1) * jax.nn.silu(hdot(wzm_ref))
    p_mem = jnp.dot(o_mem.astype(BF16), wbm_ref[...], preferred_element_type=F32)

    gate_refs = (wg0_ref, wg1_ref, wg2_ref, wg3_ref, wg4_ref, wg5_ref)
    per_branch = D_MODEL // GATE_W
    halves = []
    for part in range(per_branch):
        acc = jnp.zeros((tm, GATE_W), F32)
        for br, p_br in enumerate((p_ssm, p_attn, p_mem)):
            k = br * per_branch + part
            gate = jax.nn.sigmoid(hdot(gate_refs[k]) + bg_ref[:, k * GATE_W:(k + 1) * GATE_W])
            acc = acc + gate * p_br[:, part * GATE_W:(part + 1) * GATE_W]
        halves.append(acc)
    merged = jnp.concatenate(halves, axis=-1)
    xn = x + jnp.dot(merged.astype(BF16), wout_ref[...], preferred_element_type=F32)
    if final:
        xn = _rms(xn, fg_ref[...])
    out_ref[0] = xn


def _merge(final, layer, x, y, o_groups, lse_groups, qm, k_mem, v_mem, g, w_in_bf, bg, wglu, bglu, wbs, wba,
           wbm, wout, fg):
    B, L, _ = x.shape
    tm = TM_MERGE

    def rows(w):
        return pl.BlockSpec((1, tm, w), lambda b, i: (b, i, 0))

    def dec(r, w):
        return pl.BlockSpec((1, r, tm // r, w), lambda b, i: (b, 0, i, 0))

    mem_spec = pl.BlockSpec((1,) + k_mem.shape[1:], lambda b, i: (b, 0, 0))
    rs = [r for _, r in ATTN_CONFIGS]
    n_gate = N_BRANCHES * D_MODEL // GATE_W
    in_specs = ([rows(D_MODEL), pl.BlockSpec((1, tm // SSM_CHUNK, D_SSM * SSM_CHUNK), lambda b, i: (b, i, 0))]
                + [dec(r, D_GROUP) for r in rs] + [dec(r, LANES) for r in rs]
                + [rows(D_MEM), mem_spec, mem_spec, _const_spec(g.shape)]
                + [_w_in_spec(n, layer) for n in ("z_ssm", "z_attn", "z_mem")]
                + [_w_in_spec("gates", layer, part) for part in range(n_gate)]
                + [_const_spec(a.shape) for a in (bg, wglu, bglu, wbs, wba, wbm, wout, fg)])
    return pl.pallas_call(
        functools.partial(_merge_kernel, final),
        grid=(B, L // tm),
        in_specs=in_specs,
        out_specs=rows(D_MODEL),
        out_shape=jax.ShapeDtypeStruct((B, L, D_MODEL), F32),
        scratch_shapes=[pltpu.VMEM((D_SSM // LANES, tm, LANES), F32), pltpu.VMEM((D_ATTN // LANES, tm, LANES), F32),
                        pltpu.VMEM((len(rs), tm, LANES), F32)],
        compiler_params=pltpu.CompilerParams(vmem_limit_bytes=VMEM_LIMIT),
        name="merge",
    )(x, y, *o_groups, *lse_groups, qm, k_mem, v_mem, g, *([w_in_bf] * (3 + n_gate)), bg, wglu, bglu, wbs, wba,
      wbm, wout, fg)


def kernel(x, mem, norm_g, mem_norm_g, w_in, b_gate, ssm_lambda_re, ssm_lambda_im, ssm_log_dt, ssm_b_re,
           ssm_b_im, ssm_c_re, ssm_c_im, ssm_d, w_glu, b_glu, w_mem_kv, w_br_ssm, w_br_attn, w_br_mem,
           w_out, rel_bias, final_norm_g):
    B, L, _ = x.shape
    assert L % (ATTN_CONFIGS[-1][1] * ATTN_BLOCK) == 0 and L % TM_INPROJ == 0 and L % TM_MERGE == 0
    bias_masks = [_bias_mask(rel_bias[:, gi * ATTN_HEADS_PER_GROUP:(gi + 1) * ATTN_HEADS_PER_GROUP], win, dil)
                  for gi, (win, dil) in enumerate(ATTN_CONFIGS)]
    fg = final_norm_g.reshape(1, D_MODEL)
    w_in_bf = w_in.astype(BF16)
    per_layer = [_ssm_prep(ssm_lambda_re[i], ssm_lambda_im[i], ssm_log_dt[i], ssm_b_re[i], ssm_b_im[i],
                           ssm_c_re[i], ssm_c_im[i], ssm_d[i]) for i in range(DEPTH)]
    ssm_tables = [jnp.stack(t) for t in zip(*per_layer)]
    n_chunks = B * L // SSM_CHUNK
    for layer in range(DEPTH):
        g = norm_g[layer].reshape(1, D_MODEL)
        k_mem, v_mem = _mem_kv(mem, mem_norm_g[layer].reshape(1, D_MODEL), w_mem_kv[layer].astype(BF16))
        u2, *qkv, qm = _in_proj(x, g, w_in_bf, layer)

        y = _ssm(u2.reshape(n_chunks, D_SSM * SSM_CHUNK), ssm_tables, layer, batch=B)
        y = y.reshape(B, L // SSM_CHUNK, D_SSM * SSM_CHUNK)

        o_groups, lse_groups = [], []
        for gi, (_, r) in enumerate(ATTN_CONFIGS):
            m_len = L // r
            q_g, k_g, v_g = (qkv[3 * idx + gi].reshape(B * r, m_len, D_GROUP) for idx in range(3))
            o_g, lse_g = _attention(q_g, k_g, v_g, bias_masks[gi], 0)
            o_groups.append(o_g.reshape(B, r, m_len, D_GROUP))
            lse_groups.append(lse_g.reshape(B, r, m_len, LANES))

        x = _merge(layer == DEPTH - 1, layer, x, y, o_groups, lse_groups, qm, k_mem, v_mem, g, w_in_bf,
                   b_gate[layer].reshape(1, -1), w_glu[layer].astype(BF16), b_glu[layer].reshape(1, -1),
                   w_br_ssm[layer].astype(BF16), w_br_attn[layer].astype(BF16), w_br_mem[layer].astype(BF16),
                   w_out[layer].astype(BF16), fg)
    return x
```

```python
import functools
import math

import jax
import jax.numpy as jnp
import numpy as np
from jax import lax
from jax.experimental import pallas as pl
from jax.experimental.pallas import tpu as pltpu

F32 = jnp.float32
BF16 = jnp.bfloat16

D_MODEL = 1024
DEPTH = 2
EPS = 1e-6
N_BRANCHES = 3
D_SSM = 768
SSM_GROUP = 16
SSM_GROUPS = 48
SSM_STATE = 64
ATTN_HEAD_DIM = 64
ATTN_HEADS_PER_GROUP = 4
ATTN_CONFIGS = ((128, 1), (512, 4), (2048, 16))
N_ATTN_HEADS = 12
D_ATTN = 768
ATTN_BLOCK = 128
NUM_BUCKETS = 32
REL_MAX_DISTANCE = 2048
NEG_INF = -1e30
MEM_HEADS = 4
MEM_HEAD_DIM = 128
D_MEM = 512
D_GROUP = ATTN_HEADS_PER_GROUP * ATTN_HEAD_DIM

LANES = 128
SUBLANES = 8
SSM_CHUNK = SUBLANES
SSM_TILE_GROUPS = LANES // SSM_GROUP
SSM_TILES = D_SSM // LANES
SSM_TILE_W = SSM_CHUNK * LANES
SSM_PAIRS_PER_TILE = SSM_TILE_GROUPS // 2
VMEM_LIMIT = 56 * 1024 * 1024

TM_INPROJ = 1024
TM_MERGE = 512
TQ_ATTN = 512


def _rms(x, g):
    return x * lax.rsqrt(jnp.mean(x * x, axis=-1, keepdims=True) + EPS) * g


def _block_transpose(vs):
    n = len(vs)
    width = LANES // n
    block = lax.broadcasted_iota(jnp.int32, vs[0].shape, 1) // width
    d = n // 2
    while d >= 1:
        bit_set = (block & d) != 0
        new = list(vs)
        for i in range(n):
            if i & d == 0:
                a, b = vs[i], vs[i + d]
                new[i] = jnp.where(bit_set, pltpu.roll(b, d * width, 1), a)
                new[i + d] = jnp.where(bit_set, b, pltpu.roll(a, LANES - d * width, 1))
        vs = new
        d //= 2
    return vs


def _const_spec(shape):
    n = len(shape)
    return pl.BlockSpec(shape, lambda *_: (0,) * n, pipeline_mode=pl.Buffered(1))


def _memkv_kernel(mem_ref, g_ref, w_ref, k_ref, v_ref):
    h = _rms(mem_ref[0], g_ref[...]).astype(BF16)
    k_ref[0] = jnp.dot(h, w_ref[:, :D_MEM], preferred_element_type=F32).astype(BF16)
    v_ref[0] = jnp.dot(h, w_ref[:, D_MEM:], preferred_element_type=F32).astype(BF16)


def _mem_kv(mem, g, w_bf16):
    B, ML, _ = mem.shape
    return pl.pallas_call(
        _memkv_kernel,
        grid=(B,),
        in_specs=[pl.BlockSpec((1, ML, D_MODEL), lambda b: (b, 0, 0)),
                  _const_spec((1, D_MODEL)),
                  _const_spec((D_MODEL, 2 * D_MEM))],
        out_specs=[pl.BlockSpec((1, ML, D_MEM), lambda b: (b, 0, 0)),
                   pl.BlockSpec((1, ML, D_MEM), lambda b: (b, 0, 0))],
        out_shape=[jax.ShapeDtypeStruct((B, ML, D_MEM), BF16)] * 2,
        name="mem_kv",
    )(mem, g, w_bf16)


_IN_SIZES = (D_SSM, D_SSM, D_ATTN, D_ATTN, D_ATTN, D_ATTN, D_MEM, D_MEM, N_BRANCHES * D_MODEL)
_IN_OFFS = tuple(int(v) for v in np.concatenate([[0], np.cumsum(_IN_SIZES)]))
_IN_NAMES = ("u", "z_ssm", "q", "k", "v", "z_attn", "q_mem", "z_mem", "gates")
GATE_W = 512


def _w_in_spec(name, layer, part=0):
    idx = _IN_NAMES.index(name)
    width = GATE_W if name == "gates" else _IN_SIZES[idx]
    block, rem = divmod(_IN_OFFS[idx], width)
    assert rem == 0
    return pl.BlockSpec((None, D_MODEL, width), lambda *_: (layer, 0, block + part),
                        pipeline_mode=pl.Buffered(1))


def _inproj_kernel(x_ref, g_ref, wu_ref, wq_ref, wk_ref, wv_ref, wqm_ref, u2_ref, *rest):
    qkv_refs, qm_ref, scr = rest[:9], rest[9], rest[10]
    h = _rms(x_ref[0], g_ref[...]).astype(BF16)
    tm = h.shape[0]
    def to_scratch(p):
        for j in range(p.shape[1] // LANES):
            scr[j] = p[:, j * LANES:(j + 1) * LANES]

    to_scratch(jnp.dot(h, wu_ref[...], preferred_element_type=F32))
    for j in range(SSM_TILES):
        steps = [scr[j, pl.ds(s, tm // SSM_CHUNK, stride=SSM_CHUNK), :] for s in range(SSM_CHUNK)]
        for gl, blk in enumerate(_block_transpose(steps)):
            lo = j * SSM_TILE_W + gl * LANES
            u2_ref[0, :, lo:lo + LANES] = blk.astype(BF16)
    for idx, (w_ref, scale) in enumerate(((wq_ref, ATTN_HEAD_DIM ** -0.5), (wk_ref, None), (wv_ref, None))):
        p = jnp.dot(h, w_ref[...], preferred_element_type=F32)
        if scale is not None:
            p = p * scale
        to_scratch(p)
        tiles = D_GROUP // LANES
        for gi, (_, r) in enumerate(ATTN_CONFIGS):
            o_ref = qkv_refs[3 * idx + gi]
            for s in range(r):
                for c in range(tiles):
                    piece = scr[gi * tiles + c, pl.ds(s, tm // r, stride=r), :]
                    o_ref[0, s, :, c * LANES:(c + 1) * LANES] = piece.astype(BF16)
    qm_ref[0] = jnp.dot(h, wqm_ref[...], preferred_element_type=F32).astype(BF16)


def _in_proj(x, g, w_in_bf, layer):
    B, L, _ = x.shape
    tm = TM_INPROJ
    out_specs = [pl.BlockSpec((1, tm // SSM_CHUNK, D_SSM * SSM_CHUNK), lambda b, i: (b, i, 0))]
    out_shape = [jax.ShapeDtypeStruct((B, L // SSM_CHUNK, D_SSM * SSM_CHUNK), BF16)]
    for _ in range(3):
        for _, r in ATTN_CONFIGS:
            out_specs.append(pl.BlockSpec((1, r, tm // r, D_GROUP), lambda b, i: (b, 0, i, 0)))
            out_shape.append(jax.ShapeDtypeStruct((B, r, L // r, D_GROUP), BF16))
    out_specs.append(pl.BlockSpec((1, tm, D_MEM), lambda b, i: (b, i, 0)))
    out_shape.append(jax.ShapeDtypeStruct((B, L, D_MEM), BF16))
    return pl.pallas_call(
        _inproj_kernel,
        grid=(B, L // tm),
        in_specs=[pl.BlockSpec((1, tm, D_MODEL), lambda b, i: (b, i, 0)),
                  _const_spec((1, D_MODEL))]
                 + [_w_in_spec(n, layer) for n in ("u", "q", "k", "v", "q_mem")],
        out_specs=out_specs,
        out_shape=out_shape,
        scratch_shapes=[pltpu.VMEM((D_SSM // LANES, tm, LANES), F32)],
        compiler_params=pltpu.CompilerParams(vmem_limit_bytes=VMEM_LIMIT),
        name="in_proj",
    )(x, g, *([w_in_bf] * 5))


def _cmul(ar, ai, br, bi):
    return ar * br - ai * bi, ar * bi + ai * br


def _ssm_prep(lre, lim, log_dt, b_re, b_im, c_re, c_im, d):
    hp = lax.Precision.HIGHEST
    G, P, H, C = SSM_GROUPS, SSM_STATE, SSM_GROUP, SSM_CHUNK
    dt = jnp.exp(log_dt)[:, None]
    mag = jnp.exp(lre * dt)
    ar, ai = mag * jnp.cos(lim * dt), mag * jnp.sin(lim * dt)
    den = lre * lre + lim * lim
    nr, ni = ar - 1.0, ai
    fr = (nr * lre + ni * lim) / den
    fi = (ni * lre - nr * lim) / den
    bbr = fr[..., None] * b_re - fi[..., None] * b_im
    bbi = fr[..., None] * b_im + fi[..., None] * b_re
    prs, pis = [jnp.ones_like(ar)], [jnp.zeros_like(ai)]
    for _ in range(C):
        r_, i_ = _cmul(prs[-1], pis[-1], ar, ai)
        prs.append(r_)
        pis.append(i_)
    PR, PI = jnp.stack(prs), jnp.stack(pis)
    wr = PR[:C, :, :, None] * bbr - PI[:C, :, :, None] * bbi
    wi = PR[:C, :, :, None] * bbi + PI[:C, :, :, None] * bbr
    kk = (jnp.einsum('ghp,tgpk->tghk', c_re, wr, precision=hp)
          - jnp.einsum('ghp,tgpk->tghk', c_im, wi, precision=hp))
    kk = kk.at[0].add(jnp.eye(H, dtype=F32)[None] * d.reshape(G, H)[:, :, None])
    PT = SSM_PAIRS_PER_TILE
    k_row = kk.transpose(1, 3, 0, 2).reshape(G, H, LANES)
    sw = jnp.stack([wr[::-1], wi[::-1]])
    s_g = sw.transpose(2, 1, 4, 0, 3).reshape(G, LANES, LANES)
    cr = c_re[None] * PR[1:, :, None, :] - c_im[None] * PI[1:, :, None, :]
    ci = c_re[None] * PI[1:, :, None, :] + c_im[None] * PR[1:, :, None, :]
    rw = jnp.stack([cr, -ci])
    r_g = rw.transpose(2, 0, 4, 1, 3).reshape(G, LANES, LANES)
    alr, ali = PR[C], PI[C]
    qrs, qis = [jnp.ones_like(alr)], [jnp.zeros_like(ali)]
    for _ in range(SUBLANES):
        r_, i_ = _cmul(qrs[-1], qis[-1], alr, ali)
        qrs.append(r_)
        qis.append(i_)

    def lay(zr, zi):
        z = jnp.stack([zr.reshape(SSM_TILES, PT, LANES), zi.reshape(SSM_TILES, PT, LANES)], axis=2)
        return z.reshape(SSM_TILES, SSM_TILE_W)

    rows = jnp.arange(SUBLANES)[:, None, None]
    tabs = []
    for dsh in (1, 2, 4):
        full = jnp.broadcast_to(lay(qrs[dsh], qis[dsh])[None], (SUBLANES, SSM_TILES, SSM_TILE_W))
        tabs.append(jnp.where(rows >= dsh, full, 0.0))
    tabs.append(jnp.stack([lay(qrs[i], qis[i]) for i in range(SUBLANES)]))
    tabs.append(jnp.broadcast_to(lay(qrs[SUBLANES], qis[SUBLANES])[None], (SUBLANES, SSM_TILES, SSM_TILE_W)))
    tab = jnp.stack(tabs).transpose(2, 0, 1, 3)

    def per_tile(a):
        return a.reshape((SSM_TILES, SSM_TILE_GROUPS) + a.shape[1:])

    return per_tile(k_row), per_tile(s_g), per_tile(r_g), tab.astype(F32)


def _ssm_expand(k_ref, s_ref, r_ref, m_scr, s_scr, r_scr):
    H, P = SSM_GROUP, SSM_STATE
    lane_k = lax.broadcasted_iota(jnp.int32, (H, LANES), 1)
    low = lax.broadcasted_iota(jnp.int32, (LANES, LANES), 1) < P
    m_scr[...] = jnp.zeros(m_scr.shape, m_scr.dtype)
    r_scr[...] = jnp.zeros(r_scr.shape, r_scr.dtype)
    for gl in range(SSM_TILE_GROUPS):
        q, gl2 = divmod(gl, 2)
        own = slice(gl2 * LANES, (gl2 + 1) * LANES)
        k_row = k_ref[gl]
        for s in range(SSM_CHUNK):
            blk = jnp.where(lane_k >= s * H, pltpu.roll(k_row, s * H, 1), 0.0) if s else k_row
            m_scr[q, gl2 * LANES + s * H:gl2 * LANES + (s + 1) * H, own] = blk.astype(BF16)
        s_g = s_ref[gl]
        swapped = pltpu.roll(s_g, P, 1)
        mine = low if gl2 == 0 else ~low
        s_scr[q, own, 0:LANES] = jnp.where(mine, s_g if gl2 == 0 else swapped, 0.0).astype(BF16)
        s_scr[q, own, LANES:2 * LANES] = jnp.where(mine, swapped if gl2 == 0 else s_g, 0.0).astype(BF16)
        for r in range(2):
            rows = slice(r * LANES + gl2 * P, r * LANES + (gl2 + 1) * P)
            r_scr[q, rows, own] = r_ref[gl, r * P:(r + 1) * P, :].astype(BF16)


def _ssm_kernel(u_ref, k_ref, sg_ref, rg_ref, tab_ref, y_ref, m_scr, s_scr, r_scr, upd_ref, xin_ref):
    @pl.when(pl.program_id(1) == 0)
    def _():
        _ssm_expand(k_ref, sg_ref, rg_ref, m_scr, s_scr, r_scr)

    pw = 2 * LANES
    for q in range(SSM_PAIRS_PER_TILE):
        cols = slice(q * pw, (q + 1) * pw)
        upd_ref[:, cols] = jnp.dot(u_ref[:, cols], s_scr[q], preferred_element_type=F32)
    n_blocks = u_ref.shape[0] // SUBLANES
    row = lax.broadcasted_iota(jnp.int32, (SUBLANES, LANES), 0)

    def scan(i, carry):
        r0 = pl.multiple_of(i * SUBLANES, SUBLANES)
        rows = pl.ds(r0, SUBLANES)
        out = []
        for q in range(SSM_PAIRS_PER_TILE):
            re, im = slice(q * pw, q * pw + LANES), slice(q * pw + LANES, (q + 1) * pw)
            er, ei = carry[2 * q], carry[2 * q + 1]
            vr, vi = upd_ref[rows, re], upd_ref[rows, im]
            for lvl, dsh in enumerate((1, 2, 4)):
                cr, ci = tab_ref[lvl, :, re], tab_ref[lvl, :, im]
                sr, si = pltpu.roll(vr, dsh, 0), pltpu.roll(vi, dsh, 0)
                vr, vi = vr + (cr * sr - ci * si), vi + (cr * si + ci * sr)
            sr = jnp.where(row == 0, 0.0, pltpu.roll(vr, 1, 0))
            si = jnp.where(row == 0, 0.0, pltpu.roll(vi, 1, 0))
            pr, pi_ = tab_ref[3, :, re], tab_ref[3, :, im]
            xin_ref[rows, re] = sr + (pr * er - pi_ * ei)
            xin_ref[rows, im] = si + (pr * ei + pi_ * er)
            a8r, a8i = tab_ref[4, :, re], tab_ref[4, :, im]
            lr = jnp.broadcast_to(vr[SUBLANES - 1:SUBLANES, :], (SUBLANES, LANES))
            li = jnp.broadcast_to(vi[SUBLANES - 1:SUBLANES, :], (SUBLANES, LANES))
            out += [a8r * er - a8i * ei + lr, a8r * ei + a8i * er + li]
        return tuple(out)

    zero = (jnp.zeros((SUBLANES, LANES), F32),) * (2 * SSM_PAIRS_PER_TILE)

    lax.fori_loop(0, n_blocks, scan, zero)
    for q in range(SSM_PAIRS_PER_TILE):
        cols = slice(q * pw, (q + 1) * pw)
        y = jnp.dot(u_ref[:, cols], m_scr[q], preferred_element_type=F32)
        y_ref[:, cols] = y + jnp.dot(xin_ref[:, cols].astype(BF16), r_scr[q], preferred_element_type=F32)


def _ssm(u2, tables, layer, batch):
    rows = u2.shape[0] // batch
    w = SSM_TILE_W
    pw = 2 * LANES

    def tab_spec(a):
        nd = a.ndim - 2
        return pl.BlockSpec((None, None) + a.shape[2:], lambda j, b: (layer, j) + (0,) * nd)

    return pl.pallas_call(
        _ssm_kernel,
        grid=(SSM_TILES, batch),
        in_specs=[pl.BlockSpec((rows, w), lambda j, b: (b, j))] + [tab_spec(a) for a in tables],
        out_specs=pl.BlockSpec((rows, w), lambda j, b: (b, j)),
        out_shape=jax.ShapeDtypeStruct(u2.shape, F32),
        scratch_shapes=[pltpu.VMEM((SSM_PAIRS_PER_TILE, pw, pw), BF16)] * 3 + [pltpu.VMEM((rows, w), F32)] * 2,
        compiler_params=pltpu.CompilerParams(vmem_limit_bytes=VMEM_LIMIT),
        name="ssm",
    )(u2, *tables)


def _rel_bucket(dist):
    n = jnp.maximum(dist, 0)
    max_exact = NUM_BUCKETS // 2
    n_f = jnp.maximum(n, 1).astype(F32)
    large = max_exact + (jnp.log(n_f / max_exact) / math.log(REL_MAX_DISTANCE / max_exact)
                         * (NUM_BUCKETS - max_exact)).astype(jnp.int32)
    large = jnp.minimum(large, NUM_BUCKETS - 1)
    return jnp.where(n < max_exact, n, large)


def _bias_mask(rel_bias_g, window, dilation):
    span = window // dilation
    qi = jnp.arange(ATTN_BLOCK)[:, None]
    kj = jnp.arange(2 * ATTN_BLOCK)[None, :]
    delta = ATTN_BLOCK + qi - kj
    band = (delta >= 0) & (delta <= span)
    bucket = _rel_bucket(jnp.maximum(delta, 0) * dilation)
    hit = bucket[None, None] == jnp.arange(NUM_BUCKETS)[:, None, None, None]
    bias = jnp.sum(jnp.where(hit, rel_bias_g.astype(F32)[:, :, None, None], 0.0), axis=0)
    return jnp.where(band[None], bias, NEG_INF)


def _attn_kernel(q_ref, kc_ref, kp_ref, vc_ref, vp_ref, bm_ref, o_ref, lse_ref):
    blk = ATTN_BLOCK
    first_valid_col = jnp.where(pl.program_id(1) == 0, blk, 0)
    lane = lax.broadcasted_iota(jnp.int32, (blk, LANES), 1)
    low = lane < ATTN_HEAD_DIM
    col = lax.broadcasted_iota(jnp.int32, (blk, 2 * blk), 1)
    lane_w = lax.broadcasted_iota(jnp.int32, (blk, LANES), 1)
    n_sub = q_ref.shape[1] // blk
    dn = (((1,), (1,)), ((), ()))
    for n in range(n_sub):
        rows = slice(n * blk, (n + 1) * blk)
        lse_w = jnp.zeros((blk, LANES), F32)
        for pair in range(ATTN_HEADS_PER_GROUP // 2):
            cols = slice(pair * LANES, (pair + 1) * LANES)
            q32 = q_ref[0, rows, cols].astype(F32)
            if n == 0:
                kk = jnp.concatenate([kp_ref[0, :, cols], kc_ref[0, rows, cols]], axis=0)
                vv = jnp.concatenate([vp_ref[0, :, cols], vc_ref[0, rows, cols]], axis=0)
            else:
                kk = kc_ref[0, (n - 1) * blk:(n + 1) * blk, cols]
                vv = vc_ref[0, (n - 1) * blk:(n + 1) * blk, cols]
            outs = []
            for sub in range(2):
                hh = 2 * pair + sub
                qh = jnp.where(low if sub == 0 else ~low, q32, 0.0).astype(BF16)
                s = lax.dot_general(qh, kk, dn, preferred_element_type=F32) + bm_ref[hh]
                if n == 0:
                    s = jnp.where(col >= first_valid_col, s, NEG_INF)
                m = jnp.max(s, axis=-1, keepdims=True)
                p = jnp.exp(s - m)
                l = jnp.sum(p, axis=-1, keepdims=True)
                o = jnp.dot(p.astype(BF16), vv, preferred_element_type=F32) * (1.0 / l)
                outs.append(o)
                lse_w = jnp.where(lane_w == hh, m + jnp.log(l), lse_w)
            o_ref[0, rows, cols] = jnp.where(low, outs[0], outs[1]).astype(BF16)
        lse_ref[0, rows, :] = lse_w


def _attention(q, k, v, bias_mask, col_block):
    ns, m_len, _ = q.shape
    tq = min(TQ_ATTN, m_len)
    per = tq // ATTN_BLOCK
    cur = pl.BlockSpec((1, tq, D_GROUP), lambda s, i: (s, i, col_block))
    prev = pl.BlockSpec((1, ATTN_BLOCK, D_GROUP), lambda s, i: (s, jnp.maximum(i * per - 1, 0), col_block))
    return pl.pallas_call(
        _attn_kernel,
        grid=(ns, m_len // tq),
        in_specs=[cur, cur, prev, cur, prev,
                  _const_spec((ATTN_HEADS_PER_GROUP, ATTN_BLOCK, 2 * ATTN_BLOCK))],
        out_specs=[pl.BlockSpec((1, tq, D_GROUP), lambda s, i: (s, i, 0)),
                   pl.BlockSpec((1, tq, LANES), lambda s, i: (s, i, 0))],
        out_shape=[jax.ShapeDtypeStruct((ns, m_len, D_GROUP), BF16),
                   jax.ShapeDtypeStruct((ns, m_len, LANES), F32)],
        name="attn",
    )(q, k, k, v, v, bias_mask)


def _merge_kernel(final, x_ref, y2_ref, o0_ref, o1_ref, o2_ref, l0_ref, l1_ref, l2_ref, qm_ref, km_ref, vm_ref,
                  g_ref, wzs_ref, wza_ref, wzm_ref, wg0_ref, wg1_ref, wg2_ref, wg3_ref, wg4_ref, wg5_ref,
                  bg_ref, wglu_ref, bglu_ref, wbs_ref, wba_ref, wbm_ref, wout_ref, fg_ref, out_ref,
                  y_scr, o_scr, l_scr):
    i = pl.program_id(1)
    n_blocks = pl.num_programs(1) - 1
    tm = x_ref.shape[1]
    tiles = D_GROUP // LANES
    dilated = ((1, o1_ref, l1_ref), (2, o2_ref, l2_ref))

    def prepare(slot):
        for j in range(SSM_TILES):
            groups = [y2_ref[0, :, j * SSM_TILE_W + gl * LANES:j * SSM_TILE_W + (gl + 1) * LANES]
                      for gl in range(SSM_TILE_GROUPS)]
            for t, blk in enumerate(_block_transpose(groups)):
                y_scr[slot, j, pl.ds(t, tm // SSM_CHUNK, stride=SSM_CHUNK), :] = blk
        for gi, o_ref, l_ref in dilated:
            r = ATTN_CONFIGS[gi][1]
            for s in range(r):
                rows = pl.ds(s, tm // r, stride=r)
                for c in range(tiles):
                    o_scr[slot, (gi - 1) * tiles + c, rows, :] = o_ref[0, s, :, c * LANES:(c + 1) * LANES].astype(F32)
                l_scr[slot, gi - 1, rows, :] = l_ref[0, s]

    def compute(slot):
        x = x_ref[0]
        h = _rms(x, g_ref[...]).astype(BF16)

        def hdot(w_ref):
            return jnp.dot(h, w_ref[...], preferred_element_type=F32)

        yg = jax.nn.gelu(jnp.concatenate([y_scr[slot, j] for j in range(SSM_TILES)], axis=-1))
        t = jnp.dot(yg.astype(BF16), wglu_ref[...], preferred_element_type=F32) + bglu_ref[...]
        o_ssm = yg * jax.nn.sigmoid(t) * jax.nn.silu(hdot(wzs_ref))
        p_ssm = jnp.dot(o_ssm.astype(BF16), wbs_ref[...], preferred_element_type=F32)

        ls = (l0_ref[0, 0], l_scr[slot, 0], l_scr[slot, 1])
        os_ = [o0_ref[0, 0].astype(F32)]
        os_ += [jnp.concatenate([o_scr[slot, g * tiles + c] for c in range(tiles)], axis=-1) for g in range(2)]
        mx = jnp.maximum(jnp.maximum(ls[0], ls[1]), ls[2])
        es = [jnp.exp(l - mx) for l in ls]
        inv = 1.0 / (es[0] + es[1] + es[2])
        head_of_lane = lax.broadcasted_iota(jnp.int32, (tm, D_GROUP), 1) // ATTN_HEAD_DIM
        parts = []
        for e, o_g in zip(es, os_):
            alpha = e * inv
            wide = jnp.zeros((tm, D_GROUP), F32)
            for j in range(ATTN_HEADS_PER_GROUP):
                wide = jnp.where(head_of_lane == j, alpha[:, j:j + 1], wide)
            parts.append(o_g * wide)
        o_attn = jnp.concatenate(parts, axis=-1) * jax.nn.silu(hdot(wza_ref))
        p_attn = jnp.dot(o_attn.astype(BF16), wba_ref[...], preferred_element_type=F32)

        dn = (((1,), (1,)), ((), ()))
        heads = []
        for hd in range(MEM_HEADS):
            cols = slice(hd * MEM_HEAD_DIM, (hd + 1) * MEM_HEAD_DIM)
            s = lax.dot_general(qm_ref[0, :, cols], km_ref[0, :, cols], dn, preferred_element_type=F32)
            s = s * (MEM_HEAD_DIM ** -0.5)
            m = jnp.max(s, axis=-1, keepdims=True)
            p = jnp.exp(s - m)
            l = jnp.sum(p, axis=-1, keepdims=True)
            heads.append(jnp.dot(p.astype(BF16), vm_ref[0, :, cols], preferred_element_type=F32) * (1.0 / l))
        o_mem = jnp.concatenate(heads, axis=-1) * jax.nn.silu(hdot(wzm_ref))
        p_mem = jnp.dot(o_mem.astype(BF16), wbm_ref[...], preferred_element_type=F32)

        gate_refs = (wg0_ref, wg1_ref, wg2_ref, wg3_ref, wg4_ref, wg5_ref)
        per_branch = D_MODEL // GATE_W
        halves = []
        for part in range(per_branch):
            acc = jnp.zeros((tm, GATE_W), F32)
            for br, p_br in enumerate((p_ssm, p_attn, p_mem)):
                k = br * per_branch + part
                gate = jax.nn.sigmoid(hdot(gate_refs[k]) + bg_ref[:, k * GATE_W:(k + 1) * GATE_W])
                acc = acc + gate * p_br[:, part * GATE_W:(part + 1) * GATE_W]
            halves.append(acc)
        merged = jnp.concatenate(halves, axis=-1)
        xn = x + jnp.dot(merged.astype(BF16), wout_ref[...], preferred_element_type=F32)
        if final:
            xn = _rms(xn, fg_ref[...])
        out_ref[0] = xn

    slot = i % 2

    @pl.when(i == 0)
    def _():
        prepare(slot)

    @pl.when(jnp.logical_and(i > 0, i < n_blocks))
    def _():
        compute(1 - slot)
        prepare(slot)

    @pl.when(i == n_blocks)
    def _():
        compute(1 - slot)


def _merge(final, layer, x, y2, o_groups, lse_groups, qm, k_mem, v_mem, g, w_in_bf, bg, wglu, bglu, wbs, wba,
           wbm, wout, fg):
    B, L, _ = x.shape
    tm = TM_MERGE
    n_blocks = L // tm

    def cur(i):
        return jnp.maximum(i - 1, 0)

    def nxt(i):
        return jnp.minimum(i, n_blocks - 1)

    def rows(w):
        return pl.BlockSpec((1, tm, w), lambda b, i: (b, cur(i), 0))

    def dec(r, w, which):
        return pl.BlockSpec((1, r, tm // r, w), lambda b, i: (b, 0, which(i), 0))

    mem_spec = pl.BlockSpec((1,) + k_mem.shape[1:], lambda b, i: (b, 0, 0))
    rs = [r for _, r in ATTN_CONFIGS]
    assert rs[0] == 1
    n_gate = N_BRANCHES * D_MODEL // GATE_W
    in_specs = ([rows(D_MODEL),
                 pl.BlockSpec((1, tm // SSM_CHUNK, D_SSM * SSM_CHUNK), lambda b, i: (b, nxt(i), 0))]
                + [dec(r, D_GROUP, cur if r == 1 else nxt) for r in rs]
                + [dec(r, LANES, cur if r == 1 else nxt) for r in rs]
                + [rows(D_MEM), mem_spec, mem_spec, _const_spec(g.shape)]
                + [_w_in_spec(n, layer) for n in ("z_ssm", "z_attn", "z_mem")]
                + [_w_in_spec("gates", layer, part) for part in range(n_gate)]
                + [_const_spec(a.shape) for a in (bg, wglu, bglu, wbs, wba, wbm, wout, fg)])
    n_dil = len(rs) - 1
    return pl.pallas_call(
        functools.partial(_merge_kernel, final),
        grid=(B, n_blocks + 1),
        in_specs=in_specs,
        out_specs=rows(D_MODEL),
        out_shape=jax.ShapeDtypeStruct((B, L, D_MODEL), F32),
        scratch_shapes=[pltpu.VMEM((2, D_SSM // LANES, tm, LANES), F32),
                        pltpu.VMEM((2, n_dil * D_GROUP // LANES, tm, LANES), F32),
                        pltpu.VMEM((2, n_dil, tm, LANES), F32)],
        compiler_params=pltpu.CompilerParams(vmem_limit_bytes=VMEM_LIMIT),
        name="merge",
    )(x, y2, *o_groups, *lse_groups, qm, k_mem, v_mem, g, *([w_in_bf] * (3 + n_gate)), bg, wglu, bglu, wbs, wba,
      wbm, wout, fg)


def kernel(x, mem, norm_g, mem_norm_g, w_in, b_gate, ssm_lambda_re, ssm_lambda_im, ssm_log_dt, ssm_b_re,
           ssm_b_im, ssm_c_re, ssm_c_im, ssm_d, w_glu, b_glu, w_mem_kv, w_br_ssm, w_br_attn, w_br_mem,
           w_out, rel_bias, final_norm_g):
    B, L, _ = x.shape
    assert L % (ATTN_CONFIGS[-1][1] * ATTN_BLOCK) == 0 and L % TM_INPROJ == 0 and L % TM_MERGE == 0
    bias_masks = [_bias_mask(rel_bias[:, gi * ATTN_HEADS_PER_GROUP:(gi + 1) * ATTN_HEADS_PER_GROUP], win, dil)
                  for gi, (win, dil) in enumerate(ATTN_CONFIGS)]
    fg = final_norm_g.reshape(1, D_MODEL)
    w_in_bf = w_in.astype(BF16)
    per_layer = [_ssm_prep(ssm_lambda_re[i], ssm_lambda_im[i], ssm_log_dt[i], ssm_b_re[i], ssm_b_im[i],
                           ssm_c_re[i], ssm_c_im[i], ssm_d[i]) for i in range(DEPTH)]
    ssm_tables = [jnp.stack(t) for t in zip(*per_layer)]
    n_chunks = B * L // SSM_CHUNK
    for layer in range(DEPTH):
        g = norm_g[layer].reshape(1, D_MODEL)
        k_mem, v_mem = _mem_kv(mem, mem_norm_g[layer].reshape(1, D_MODEL), w_mem_kv[layer].astype(BF16))
        u2, *qkv, qm = _in_proj(x, g, w_in_bf, layer)

        y = _ssm(u2.reshape(n_chunks, D_SSM * SSM_CHUNK), ssm_tables, layer, batch=B)
        y = y.reshape(B, L // SSM_CHUNK, D_SSM * SSM_CHUNK)

        o_groups, lse_groups = [], []
        for gi, (_, r) in enumerate(ATTN_CONFIGS):
            m_len = L // r
            q_g, k_g, v_g = (qkv[3 * idx + gi].reshape(B * r, m_len, D_GROUP) for idx in range(3))
            o_g, lse_g = _attention(q_g, k_g, v_g, bias_masks[gi], 0)
            o_groups.append(o_g.reshape(B, r, m_len, D_GROUP))
            lse_groups.append(lse_g.reshape(B, r, m_len, LANES))

        x = _merge(layer == DEPTH - 1, layer, x, y, o_groups, lse_groups, qm, k_mem, v_mem, g, w_in_bf,
                   b_gate[layer].reshape(1, -1), w_glu[layer].astype(BF16), b_glu[layer].reshape(1, -1),
                   w_br_ssm[layer].astype(BF16), w_br_attn[layer].astype(BF16), w_br_mem[layer].astype(BF16),
                   w_out[layer].astype(BF16), fg)
    return x
```

```python
import functools
import math

import jax
import jax.numpy as jnp
import numpy as np
from jax import lax
from jax.experimental import pallas as pl
from jax.experimental.pallas import tpu as pltpu

F32 = jnp.float32
BF16 = jnp.bfloat16

D_MODEL = 1024
DEPTH = 2
EPS = 1e-6
N_BRANCHES = 3
D_SSM = 768
SSM_GROUP = 16
SSM_GROUPS = 48
SSM_STATE = 64
ATTN_HEAD_DIM = 64
ATTN_HEADS_PER_GROUP = 4
ATTN_CONFIGS = ((128, 1), (512, 4), (2048, 16))
N_ATTN_HEADS = 12
D_ATTN = 768
ATTN_BLOCK = 128
NUM_BUCKETS = 32
REL_MAX_DISTANCE = 2048
NEG_INF = -1e30
MEM_HEADS = 4
MEM_HEAD_DIM = 128
D_MEM = 512
D_GROUP = ATTN_HEADS_PER_GROUP * ATTN_HEAD_DIM

LANES = 128
SUBLANES = 8
SSM_CHUNK = SUBLANES
SSM_TILE_GROUPS = LANES // SSM_GROUP
SSM_TILES = D_SSM // LANES
SSM_TILE_W = SSM_CHUNK * LANES
SSM_PAIRS_PER_TILE = SSM_TILE_GROUPS // 2
VMEM_LIMIT = 56 * 1024 * 1024

TM_INPROJ = 1024
TM_MERGE = 512
TQ_ATTN = 1024


def _rms(x, g):
    return x * lax.rsqrt(jnp.mean(x * x, axis=-1, keepdims=True) + EPS) * g


def _block_transpose(vs):
    n = len(vs)
    width = LANES // n
    block = lax.broadcasted_iota(jnp.int32, vs[0].shape, 1) // width
    d = n // 2
    while d >= 1:
        bit_set = (block & d) != 0
        new = list(vs)
        for i in range(n):
            if i & d == 0:
                a, b = vs[i], vs[i + d]
                new[i] = jnp.where(bit_set, pltpu.roll(b, d * width, 1), a)
                new[i + d] = jnp.where(bit_set, b, pltpu.roll(a, LANES - d * width, 1))
        vs = new
        d //= 2
    return vs


def _const_spec(shape):
    n = len(shape)
    return pl.BlockSpec(shape, lambda *_: (0,) * n, pipeline_mode=pl.Buffered(1))


def _memkv_kernel(mem_ref, g_ref, w_ref, k_ref, v_ref):
    h = _rms(mem_ref[0], g_ref[...]).astype(BF16)
    k_ref[0] = jnp.dot(h, w_ref[:, :D_MEM], preferred_element_type=F32).astype(BF16)
    v_ref[0] = jnp.dot(h, w_ref[:, D_MEM:], preferred_element_type=F32).astype(BF16)


def _mem_kv(mem, g, w_bf16):
    B, ML, _ = mem.shape
    return pl.pallas_call(
        _memkv_kernel,
        grid=(B,),
        in_specs=[pl.BlockSpec((1, ML, D_MODEL), lambda b: (b, 0, 0)),
                  _const_spec((1, D_MODEL)),
                  _const_spec((D_MODEL, 2 * D_MEM))],
        out_specs=[pl.BlockSpec((1, ML, D_MEM), lambda b: (b, 0, 0)),
                   pl.BlockSpec((1, ML, D_MEM), lambda b: (b, 0, 0))],
        out_shape=[jax.ShapeDtypeStruct((B, ML, D_MEM), BF16)] * 2,
        name="mem_kv",
    )(mem, g, w_bf16)


_IN_SIZES = (D_SSM, D_SSM, D_ATTN, D_ATTN, D_ATTN, D_ATTN, D_MEM, D_MEM, N_BRANCHES * D_MODEL)
_IN_OFFS = tuple(int(v) for v in np.concatenate([[0], np.cumsum(_IN_SIZES)]))
_IN_NAMES = ("u", "z_ssm", "q", "k", "v", "z_attn", "q_mem", "z_mem", "gates")
GATE_W = 512


def _w_in_spec(name, layer, part=0):
    idx = _IN_NAMES.index(name)
    width = GATE_W if name == "gates" else _IN_SIZES[idx]
    block, rem = divmod(_IN_OFFS[idx], width)
    assert rem == 0
    return pl.BlockSpec((None, D_MODEL, width), lambda *_: (layer, 0, block + part),
                        pipeline_mode=pl.Buffered(1))


def _inproj_kernel(x_ref, g_ref, wu_ref, wq_ref, wk_ref, wv_ref, wqm_ref, u2_ref, *rest):
    qkv_refs, qm_ref, scr = rest[:9], rest[9], rest[10]
    h = _rms(x_ref[0], g_ref[...]).astype(BF16)
    tm = h.shape[0]

    def to_scratch(p):
        for j in range(p.shape[1] // LANES):
            scr[j] = p[:, j * LANES:(j + 1) * LANES]

    to_scratch(jnp.dot(h, wu_ref[...], preferred_element_type=F32))
    for j in range(SSM_TILES):
        steps = [scr[j, pl.ds(s, tm // SSM_CHUNK, stride=SSM_CHUNK), :] for s in range(SSM_CHUNK)]
        for gl, blk in enumerate(_block_transpose(steps)):
            lo = j * SSM_TILE_W + gl * LANES
            u2_ref[0, :, lo:lo + LANES] = blk.astype(BF16)
    for idx, (w_ref, scale) in enumerate(((wq_ref, ATTN_HEAD_DIM ** -0.5), (wk_ref, None), (wv_ref, None))):
        p = jnp.dot(h, w_ref[...], preferred_element_type=F32)
        if scale is not None:
            p = p * scale
        to_scratch(p)
        tiles = D_GROUP // LANES
        for gi, (_, r) in enumerate(ATTN_CONFIGS):
            o_ref = qkv_refs[3 * idx + gi]
            for s in range(r):
                for c in range(tiles):
                    piece = scr[gi * tiles + c, pl.ds(s, tm // r, stride=r), :]
                    o_ref[0, s, :, c * LANES:(c + 1) * LANES] = piece.astype(BF16)
    qm_ref[0] = jnp.dot(h, wqm_ref[...], preferred_element_type=F32).astype(BF16)


def _in_proj(x, g, w_in_bf, layer):
    B, L, _ = x.shape
    tm = TM_INPROJ
    out_specs = [pl.BlockSpec((1, tm // SSM_CHUNK, D_SSM * SSM_CHUNK), lambda b, i: (b, i, 0))]
    out_shape = [jax.ShapeDtypeStruct((B, L // SSM_CHUNK, D_SSM * SSM_CHUNK), BF16)]
    for _ in range(3):
        for _, r in ATTN_CONFIGS:
            out_specs.append(pl.BlockSpec((1, r, tm // r, D_GROUP), lambda b, i: (b, 0, i, 0)))
            out_shape.append(jax.ShapeDtypeStruct((B, r, L // r, D_GROUP), BF16))
    out_specs.append(pl.BlockSpec((1, tm, D_MEM), lambda b, i: (b, i, 0)))
    out_shape.append(jax.ShapeDtypeStruct((B, L, D_MEM), BF16))
    return pl.pallas_call(
        _inproj_kernel,
        grid=(B, L // tm),
        in_specs=[pl.BlockSpec((1, tm, D_MODEL), lambda b, i: (b, i, 0)),
                  _const_spec((1, D_MODEL))]
                 + [_w_in_spec(n, layer) for n in ("u", "q", "k", "v", "q_mem")],
        out_specs=out_specs,
        out_shape=out_shape,
        scratch_shapes=[pltpu.VMEM((D_SSM // LANES, tm, LANES), F32)],
        compiler_params=pltpu.CompilerParams(vmem_limit_bytes=VMEM_LIMIT),
        name="in_proj",
    )(x, g, *([w_in_bf] * 5))


def _cmul(ar, ai, br, bi):
    return ar * br - ai * bi, ar * bi + ai * br


def _ssm_prep(lre, lim, log_dt, b_re, b_im, c_re, c_im, d):
    hp = lax.Precision.HIGHEST
    P, H, C = SSM_STATE, SSM_GROUP, SSM_CHUNK
    G = lre.shape[0]
    NT = G // SSM_TILE_GROUPS
    dt = jnp.exp(log_dt)[:, None]
    mag = jnp.exp(lre * dt)
    ar, ai = mag * jnp.cos(lim * dt), mag * jnp.sin(lim * dt)
    den = lre * lre + lim * lim
    nr, ni = ar - 1.0, ai
    fr = (nr * lre + ni * lim) / den
    fi = (ni * lre - nr * lim) / den
    bbr = fr[..., None] * b_re - fi[..., None] * b_im
    bbi = fr[..., None] * b_im + fi[..., None] * b_re
    prs, pis = [jnp.ones_like(ar)], [jnp.zeros_like(ai)]
    for _ in range(C):
        r_, i_ = _cmul(prs[-1], pis[-1], ar, ai)
        prs.append(r_)
        pis.append(i_)
    PR, PI = jnp.stack(prs), jnp.stack(pis)
    wr = PR[:C, :, :, None] * bbr - PI[:C, :, :, None] * bbi
    wi = PR[:C, :, :, None] * bbi + PI[:C, :, :, None] * bbr
    kk = (jnp.einsum('ghp,tgpk->tghk', c_re, wr, precision=hp)
          - jnp.einsum('ghp,tgpk->tghk', c_im, wi, precision=hp))
    kk = kk.at[0].add(jnp.eye(H, dtype=F32)[None] * d.reshape(G, H)[:, :, None])
    PT = SSM_PAIRS_PER_TILE
    k_row = kk.transpose(1, 3, 0, 2).reshape(G, H, LANES)
    sw = jnp.stack([wr[::-1], wi[::-1]])
    s_g = sw.transpose(2, 1, 4, 0, 3).reshape(G, LANES, LANES)
    cr = c_re[None] * PR[1:, :, None, :] - c_im[None] * PI[1:, :, None, :]
    ci = c_re[None] * PI[1:, :, None, :] + c_im[None] * PR[1:, :, None, :]
    rw = jnp.stack([cr, -ci])
    r_g = rw.transpose(2, 0, 4, 1, 3).reshape(G, LANES, LANES)
    alr, ali = PR[C], PI[C]
    qrs, qis = [jnp.ones_like(alr)], [jnp.zeros_like(ali)]
    for _ in range(SUBLANES):
        r_, i_ = _cmul(qrs[-1], qis[-1], alr, ali)
        qrs.append(r_)
        qis.append(i_)

    def lay(zr, zi):
        z = jnp.stack([zr.reshape(NT, PT, LANES), zi.reshape(NT, PT, LANES)], axis=2)
        return z.reshape(NT, SSM_TILE_W)

    rows = jnp.arange(SUBLANES)[:, None, None]
    tabs = []
    for dsh in (1, 2, 4):
        full = jnp.broadcast_to(lay(qrs[dsh], qis[dsh])[None], (SUBLANES, NT, SSM_TILE_W))
        tabs.append(jnp.where(rows >= dsh, full, 0.0))
    tabs.append(jnp.stack([lay(qrs[i], qis[i]) for i in range(SUBLANES)]))
    tabs.append(jnp.broadcast_to(lay(qrs[SUBLANES], qis[SUBLANES])[None], (SUBLANES, NT, SSM_TILE_W)))
    tab = jnp.stack(tabs).transpose(2, 0, 1, 3)

    def per_tile(a):
        return a.reshape((NT, SSM_TILE_GROUPS) + a.shape[1:])

    return per_tile(k_row), per_tile(s_g), per_tile(r_g), tab.astype(F32)


def _ssm_expand(k_ref, s_ref, r_ref, m_scr, s_scr, r_scr):
    H, P = SSM_GROUP, SSM_STATE
    lane_k = lax.broadcasted_iota(jnp.int32, (H, LANES), 1)
    low = lax.broadcasted_iota(jnp.int32, (LANES, LANES), 1) < P
    m_scr[...] = jnp.zeros(m_scr.shape, m_scr.dtype)
    r_scr[...] = jnp.zeros(r_scr.shape, r_scr.dtype)
    for gl in range(SSM_TILE_GROUPS):
        q, gl2 = divmod(gl, 2)
        own = slice(gl2 * LANES, (gl2 + 1) * LANES)
        k_row = k_ref[gl]
        for s in range(SSM_CHUNK):
            blk = jnp.where(lane_k >= s * H, pltpu.roll(k_row, s * H, 1), 0.0) if s else k_row
            m_scr[q, gl2 * LANES + s * H:gl2 * LANES + (s + 1) * H, own] = blk.astype(BF16)
        s_g = s_ref[gl]
        swapped = pltpu.roll(s_g, P, 1)
        mine = low if gl2 == 0 else ~low
        s_scr[q, own, 0:LANES] = jnp.where(mine, s_g if gl2 == 0 else swapped, 0.0).astype(BF16)
        s_scr[q, own, LANES:2 * LANES] = jnp.where(mine, swapped if gl2 == 0 else s_g, 0.0).astype(BF16)
        for r in range(2):
            rows = slice(r * LANES + gl2 * P, r * LANES + (gl2 + 1) * P)
            r_scr[q, rows, own] = r_ref[gl, r * P:(r + 1) * P, :].astype(BF16)


def _ssm_kernel(u_ref, k_ref, sg_ref, rg_ref, tab_ref, y_ref, m_scr, s_scr, r_scr, upd_ref, xin_ref):
    @pl.when(pl.program_id(1) == 0)
    def _():
        _ssm_expand(k_ref, sg_ref, rg_ref, m_scr, s_scr, r_scr)

    pw = 2 * LANES
    for q in range(SSM_PAIRS_PER_TILE):
        cols = slice(q * pw, (q + 1) * pw)
        upd_ref[:, cols] = jnp.dot(u_ref[:, cols], s_scr[q], preferred_element_type=F32)
    n_blocks = u_ref.shape[0] // SUBLANES
    row = lax.broadcasted_iota(jnp.int32, (SUBLANES, LANES), 0)

    def scan(i, carry):
        r0 = pl.multiple_of(i * SUBLANES, SUBLANES)
        rows = pl.ds(r0, SUBLANES)
        out = []
        for q in range(SSM_PAIRS_PER_TILE):
            re, im = slice(q * pw, q * pw + LANES), slice(q * pw + LANES, (q + 1) * pw)
            er, ei = carry[2 * q], carry[2 * q + 1]
            vr, vi = upd_ref[rows, re], upd_ref[rows, im]
            for lvl, dsh in enumerate((1, 2, 4)):
                cr, ci = tab_ref[lvl, :, re], tab_ref[lvl, :, im]
                sr, si = pltpu.roll(vr, dsh, 0), pltpu.roll(vi, dsh, 0)
                vr, vi = vr + (cr * sr - ci * si), vi + (cr * si + ci * sr)
            sr = jnp.where(row == 0, 0.0, pltpu.roll(vr, 1, 0))
            si = jnp.where(row == 0, 0.0, pltpu.roll(vi, 1, 0))
            pr, pi_ = tab_ref[3, :, re], tab_ref[3, :, im]
            xin_ref[rows, re] = sr + (pr * er - pi_ * ei)
            xin_ref[rows, im] = si + (pr * ei + pi_ * er)
            a8r, a8i = tab_ref[4, :, re], tab_ref[4, :, im]
            lr = jnp.broadcast_to(vr[SUBLANES - 1:SUBLANES, :], (SUBLANES, LANES))
            li = jnp.broadcast_to(vi[SUBLANES - 1:SUBLANES, :], (SUBLANES, LANES))
            out += [a8r * er - a8i * ei + lr, a8r * ei + a8i * er + li]
        return tuple(out)

    zero = (jnp.zeros((SUBLANES, LANES), F32),) * (2 * SSM_PAIRS_PER_TILE)

    lax.fori_loop(0, n_blocks, scan, zero)
    for q in range(SSM_PAIRS_PER_TILE):
        cols = slice(q * pw, (q + 1) * pw)
        y = jnp.dot(u_ref[:, cols], m_scr[q], preferred_element_type=F32)
        y_ref[:, cols] = y + jnp.dot(xin_ref[:, cols].astype(BF16), r_scr[q], preferred_element_type=F32)


def _ssm(u2, tables, layer, batch):
    rows = u2.shape[0] // batch
    w = SSM_TILE_W
    pw = 2 * LANES

    def tab_spec(a):
        nd = a.ndim - 2
        return pl.BlockSpec((None, None) + a.shape[2:], lambda j, b: (layer, j) + (0,) * nd)

    return pl.pallas_call(
        _ssm_kernel,
        grid=(SSM_TILES, batch),
        in_specs=[pl.BlockSpec((rows, w), lambda j, b: (b, j))] + [tab_spec(a) for a in tables],
        out_specs=pl.BlockSpec((rows, w), lambda j, b: (b, j)),
        out_shape=jax.ShapeDtypeStruct(u2.shape, F32),
        scratch_shapes=[pltpu.VMEM((SSM_PAIRS_PER_TILE, pw, pw), BF16)] * 3 + [pltpu.VMEM((rows, w), F32)] * 2,
        compiler_params=pltpu.CompilerParams(vmem_limit_bytes=VMEM_LIMIT),
        name="ssm",
    )(u2, *tables)


def _rel_bucket(dist):
    n = jnp.maximum(dist, 0)
    max_exact = NUM_BUCKETS // 2
    n_f = jnp.maximum(n, 1).astype(F32)
    large = max_exact + (jnp.log(n_f / max_exact) / math.log(REL_MAX_DISTANCE / max_exact)
                         * (NUM_BUCKETS - max_exact)).astype(jnp.int32)
    large = jnp.minimum(large, NUM_BUCKETS - 1)
    return jnp.where(n < max_exact, n, large)


def _bias_mask(rel_bias_g, window, dilation):
    span = window // dilation
    qi = jnp.arange(ATTN_BLOCK)[:, None]
    kj = jnp.arange(2 * ATTN_BLOCK)[None, :]
    delta = ATTN_BLOCK + qi - kj
    band = (delta >= 0) & (delta <= span)
    bucket = _rel_bucket(jnp.maximum(delta, 0) * dilation)
    hit = bucket[None, None] == jnp.arange(NUM_BUCKETS)[:, None, None, None]
    bias = jnp.sum(jnp.where(hit, rel_bias_g.astype(F32)[:, :, None, None], 0.0), axis=0)
    return jnp.where(band[None], bias, NEG_INF)


def _attn_kernel(q_ref, kc_ref, kp_ref, vc_ref, vp_ref, bm_ref, o_ref, lse_ref):
    blk = ATTN_BLOCK
    first_valid_col = jnp.where(pl.program_id(1) == 0, blk, 0)
    lane = lax.broadcasted_iota(jnp.int32, (blk, LANES), 1)
    low = lane < ATTN_HEAD_DIM
    col = lax.broadcasted_iota(jnp.int32, (blk, 2 * blk), 1)
    n_sub = q_ref.shape[1] // blk
    dn = (((1,), (1,)), ((), ()))
    for sq in range(q_ref.shape[0]):
        for n in range(n_sub):
            rows = slice(n * blk, (n + 1) * blk)
            lse_w = jnp.zeros((blk, LANES), F32)
            for pair in range(ATTN_HEADS_PER_GROUP // 2):
                cols = slice(pair * LANES, (pair + 1) * LANES)
                q32 = q_ref[sq, rows, cols].astype(F32)
                if n == 0:
                    kk = jnp.concatenate([kp_ref[sq, :, cols], kc_ref[sq, rows, cols]], axis=0)
                    vv = jnp.concatenate([vp_ref[sq, :, cols], vc_ref[sq, rows, cols]], axis=0)
                else:
                    kk = kc_ref[sq, (n - 1) * blk:(n + 1) * blk, cols]
                    vv = vc_ref[sq, (n - 1) * blk:(n + 1) * blk, cols]
                outs = []
                for sub in range(2):
                    hh = 2 * pair + sub
                    qh = jnp.where(low if sub == 0 else ~low, q32, 0.0).astype(BF16)
                    s = lax.dot_general(qh, kk, dn, preferred_element_type=F32) + bm_ref[hh]
                    if n == 0:
                        s = jnp.where(col >= first_valid_col, s, NEG_INF)
                    m = jnp.max(s, axis=-1, keepdims=True)
                    p = jnp.exp(s - m)
                    l = jnp.sum(p, axis=-1, keepdims=True)
                    o = jnp.dot(p.astype(BF16), vv, preferred_element_type=F32) * (1.0 / l)
                    outs.append(o)
                    lse_w = jnp.where(lane == hh, m + jnp.log(l), lse_w)
                o_ref[sq, rows, cols] = jnp.where(low, outs[0], outs[1]).astype(BF16)
            lse_ref[sq, rows, :] = lse_w


def _attention(q, k, v, bias_mask):
    ns, m_len, _ = q.shape
    tq = min(TQ_ATTN, m_len)
    per_step = TQ_ATTN // tq
    per = tq // ATTN_BLOCK
    cur = pl.BlockSpec((per_step, tq, D_GROUP), lambda s, i: (s, i, 0))
    prev = pl.BlockSpec((per_step, ATTN_BLOCK, D_GROUP), lambda s, i: (s, jnp.maximum(i * per - 1, 0), 0))
    return pl.pallas_call(
        _attn_kernel,
        grid=(ns // per_step, m_len // tq),
        in_specs=[cur, cur, prev, cur, prev,
                  _const_spec((ATTN_HEADS_PER_GROUP, ATTN_BLOCK, 2 * ATTN_BLOCK))],
        out_specs=[cur, pl.BlockSpec((per_step, tq, LANES), lambda s, i: (s, i, 0))],
        out_shape=[jax.ShapeDtypeStruct((ns, m_len, D_GROUP), BF16),
                   jax.ShapeDtypeStruct((ns, m_len, LANES), F32)],
        name="attn",
    )(q, k, k, v, v, bias_mask)


def _merge_kernel(final, x_ref, y2_ref, o0_ref, o1_ref, o2_ref, l0_ref, l1_ref, l2_ref, qm_ref, km_ref, vm_ref,
                  g_ref, wzs_ref, wza_ref, wzm_ref, wg0_ref, wg1_ref, wg2_ref, wg3_ref, wg4_ref, wg5_ref,
                  bg_ref, wglu_ref, bglu_ref, wbs_ref, wba_ref, wbm_ref, wout_ref, fg_ref, out_ref,
                  y_scr, o_scr, l_scr):
    i = pl.program_id(1)
    n_blocks = pl.num_programs(1) - 1
    tm = x_ref.shape[1]
    tiles = D_GROUP // LANES
    dilated = ((1, o1_ref, l1_ref), (2, o2_ref, l2_ref))

    def prepare(slot):
        for j in range(SSM_TILES):
            groups = [y2_ref[0, :, j * SSM_TILE_W + gl * LANES:j * SSM_TILE_W + (gl + 1) * LANES]
                      for gl in range(SSM_TILE_GROUPS)]
            for t, blk in enumerate(_block_transpose(groups)):
                y_scr[slot, j, pl.ds(t, tm // SSM_CHUNK, stride=SSM_CHUNK), :] = blk
        for gi, o_ref, l_ref in dilated:
            r = ATTN_CONFIGS[gi][1]
            for s in range(r):
                rows = pl.ds(s, tm // r, stride=r)
                for c in range(tiles):
                    o_scr[slot, (gi - 1) * tiles + c, rows, :] = o_ref[0, s, :, c * LANES:(c + 1) * LANES].astype(F32)
                l_scr[slot, gi - 1, rows, :] = l_ref[0, s]

    def compute(slot):
        x = x_ref[0]
        h = _rms(x, g_ref[...]).astype(BF16)

        def hdot(w_ref):
            return jnp.dot(h, w_ref[...], preferred_element_type=F32)

        yg = jax.nn.gelu(jnp.concatenate([y_scr[slot, j] for j in range(SSM_TILES)], axis=-1))
        t = jnp.dot(yg.astype(BF16), wglu_ref[...], preferred_element_type=F32) + bglu_ref[...]
        o_ssm = yg * jax.nn.sigmoid(t) * jax.nn.silu(hdot(wzs_ref))
        p_ssm = jnp.dot(o_ssm.astype(BF16), wbs_ref[...], preferred_element_type=F32)

        ls = (l0_ref[0, 0], l_scr[slot, 0], l_scr[slot, 1])
        os_ = [o0_ref[0, 0].astype(F32)]
        os_ += [jnp.concatenate([o_scr[slot, g * tiles + c] for c in range(tiles)], axis=-1) for g in range(2)]
        mx = jnp.maximum(jnp.maximum(ls[0], ls[1]), ls[2])
        es = [jnp.exp(l - mx) for l in ls]
        inv = 1.0 / (es[0] + es[1] + es[2])
        head_of_lane = lax.broadcasted_iota(jnp.int32, (tm, D_GROUP), 1) // ATTN_HEAD_DIM
        parts = []
        for e, o_g in zip(es, os_):
            alpha = e * inv
            wide = jnp.zeros((tm, D_GROUP), F32)
            for j in range(ATTN_HEADS_PER_GROUP):
                wide = jnp.where(head_of_lane == j, alpha[:, j:j + 1], wide)
            parts.append(o_g * wide)
        o_attn = jnp.concatenate(parts, axis=-1) * jax.nn.silu(hdot(wza_ref))
        p_attn = jnp.dot(o_attn.astype(BF16), wba_ref[...], preferred_element_type=F32)

        dn = (((1,), (1,)), ((), ()))
        heads = []
        for hd in range(MEM_HEADS):
            cols = slice(hd * MEM_HEAD_DIM, (hd + 1) * MEM_HEAD_DIM)
            s = lax.dot_general(qm_ref[0, :, cols], km_ref[0, :, cols], dn, preferred_element_type=F32)
            s = s * (MEM_HEAD_DIM ** -0.5)
            m = jnp.max(s, axis=-1, keepdims=True)
            p = jnp.exp(s - m)
            l = jnp.sum(p, axis=-1, keepdims=True)
            heads.append(jnp.dot(p.astype(BF16), vm_ref[0, :, cols], preferred_element_type=F32) * (1.0 / l))
        o_mem = jnp.concatenate(heads, axis=-1) * jax.nn.silu(hdot(wzm_ref))
        p_mem = jnp.dot(o_mem.astype(BF16), wbm_ref[...], preferred_element_type=F32)

        gate_refs = (wg0_ref, wg1_ref, wg2_ref, wg3_ref, wg4_ref, wg5_ref)
        per_branch = D_MODEL // GATE_W
        halves = []
        for part in range(per_branch):
            acc = jnp.zeros((tm, GATE_W), F32)
            for br, p_br in enumerate((p_ssm, p_attn, p_mem)):
                k = br * per_branch + part
                gate = jax.nn.sigmoid(hdot(gate_refs[k]) + bg_ref[:, k * GATE_W:(k + 1) * GATE_W])
                acc = acc + gate * p_br[:, part * GATE_W:(part + 1) * GATE_W]
            halves.append(acc)
        merged = jnp.concatenate(halves, axis=-1)
        xn = x + jnp.dot(merged.astype(BF16), wout_ref[...], preferred_element_type=F32)
        if final:
            xn = _rms(xn, fg_ref[...])
        out_ref[0] = xn

    slot = i % 2

    @pl.when(i == 0)
    def _():
        prepare(slot)

    @pl.when(jnp.logical_and(i > 0, i < n_blocks))
    def _():
        compute(1 - slot)
        prepare(slot)

    @pl.when(i == n_blocks)
    def _():
        compute(1 - slot)


def _merge(final, layer, x, y2, o_groups, lse_groups, qm, k_mem, v_mem, g, w_in_bf, bg, wglu, bglu, wbs, wba,
           wbm, wout, fg):
    B, L, _ = x.shape
    tm = TM_MERGE
    n_blocks = L // tm

    def cur(i):
        return jnp.maximum(i - 1, 0)

    def nxt(i):
        return jnp.minimum(i, n_blocks - 1)

    def rows(w):
        return pl.BlockSpec((1, tm, w), lambda b, i: (b, cur(i), 0))

    def dec(r, w, which):
        return pl.BlockSpec((1, r, tm // r, w), lambda b, i: (b, 0, which(i), 0))

    mem_spec = pl.BlockSpec((1,) + k_mem.shape[1:], lambda b, i: (b, 0, 0))
    rs = [r for _, r in ATTN_CONFIGS]
    assert rs[0] == 1
    n_gate = N_BRANCHES * D_MODEL // GATE_W
    in_specs = ([rows(D_MODEL),
                 pl.BlockSpec((1, tm // SSM_CHUNK, D_SSM * SSM_CHUNK), lambda b, i: (b, nxt(i), 0))]
                + [dec(r, D_GROUP, cur if r == 1 else nxt) for r in rs]
                + [dec(r, LANES, cur if r == 1 else nxt) for r in rs]
                + [rows(D_MEM), mem_spec, mem_spec, _const_spec(g.shape)]
                + [_w_in_spec(n, layer) for n in ("z_ssm", "z_attn", "z_mem")]
                + [_w_in_spec("gates", layer, part) for part in range(n_gate)]
                + [_const_spec(a.shape) for a in (bg, wglu, bglu, wbs, wba, wbm, wout, fg)])
    n_dil = len(rs) - 1
    return pl.pallas_call(
        functools.partial(_merge_kernel, final),
        grid=(B, n_blocks + 1),
        in_specs=in_specs,
        out_specs=rows(D_MODEL),
        out_shape=jax.ShapeDtypeStruct((B, L, D_MODEL), F32),
        scratch_shapes=[pltpu.VMEM((2, D_SSM // LANES, tm, LANES), F32),
                        pltpu.VMEM((2, n_dil * D_GROUP // LANES, tm, LANES), F32),
                        pltpu.VMEM((2, n_dil, tm, LANES), F32)],
        compiler_params=pltpu.CompilerParams(vmem_limit_bytes=VMEM_LIMIT),
        name="merge",
    )(x, y2, *o_groups, *lse_groups, qm, k_mem, v_mem, g, *([w_in_bf] * (3 + n_gate)), bg, wglu, bglu, wbs, wba,
      wbm, wout, fg)


def kernel(x, mem, norm_g, mem_norm_g, w_in, b_gate, ssm_lambda_re, ssm_lambda_im, ssm_log_dt, ssm_b_re,
           ssm_b_im, ssm_c_re, ssm_c_im, ssm_d, w_glu, b_glu, w_mem_kv, w_br_ssm, w_br_attn, w_br_mem,
           w_out, rel_bias, final_norm_g):
    B, L, _ = x.shape
    assert L % (ATTN_CONFIGS[-1][1] * ATTN_BLOCK) == 0 and L % TM_INPROJ == 0 and L % TM_MERGE == 0
    bias_masks = [_bias_mask(rel_bias[:, gi * ATTN_HEADS_PER_GROUP:(gi + 1) * ATTN_HEADS_PER_GROUP], win, dil)
                  for gi, (win, dil) in enumerate(ATTN_CONFIGS)]
    fg = final_norm_g.reshape(1, D_MODEL)
    w_in_bf = w_in.astype(BF16)
    def groups_of_all_layers(a):
        return a.reshape((DEPTH * SSM_GROUPS,) + a.shape[2:])

    ssm_tables = _ssm_prep(*(groups_of_all_layers(a) for a in (
        ssm_lambda_re, ssm_lambda_im, ssm_log_dt, ssm_b_re, ssm_b_im, ssm_c_re, ssm_c_im,
        ssm_d.reshape(DEPTH, SSM_GROUPS, SSM_GROUP))))
    ssm_tables = [t.reshape((DEPTH, SSM_TILES) + t.shape[1:]) for t in ssm_tables]
    n_chunks = B * L // SSM_CHUNK
    for layer in range(DEPTH):
        g = norm_g[layer].reshape(1, D_MODEL)
        k_mem, v_mem = _mem_kv(mem, mem_norm_g[layer].reshape(1, D_MODEL), w_mem_kv[layer].astype(BF16))
        u2, *qkv, qm = _in_proj(x, g, w_in_bf, layer)

        y = _ssm(u2.reshape(n_chunks, D_SSM * SSM_CHUNK), ssm_tables, layer, batch=B)
        y = y.reshape(B, L // SSM_CHUNK, D_SSM * SSM_CHUNK)

        o_groups, lse_groups = [], []
        for gi, (_, r) in enumerate(ATTN_CONFIGS):
            m_len = L // r
            q_g, k_g, v_g = (qkv[3 * idx + gi].reshape(B * r, m_len, D_GROUP) for idx in range(3))
            o_g, lse_g = _attention(q_g, k_g, v_g, bias_masks[gi])
            o_groups.append(o_g.reshape(B, r, m_len, D_GROUP))
            lse_groups.append(lse_g.reshape(B, r, m_len, LANES))

        x = _merge(layer == DEPTH - 1, layer, x, y, o_groups, lse_groups, qm, k_mem, v_mem, g, w_in_bf,
                   b_gate[layer].reshape(1, -1), w_glu[layer].astype(BF16), b_glu[layer].reshape(1, -1),
                   w_br_ssm[layer].astype(BF16), w_br_attn[layer].astype(BF16), w_br_mem[layer].astype(BF16),
                   w_out[layer].astype(BF16), fg)
    return x
```

```python
import functools
import math

import jax
import jax.numpy as jnp
import numpy as np
from jax import lax
from jax.experimental import pallas as pl
from jax.experimental.pallas import tpu as pltpu

F32 = jnp.float32
BF16 = jnp.bfloat16

D_MODEL = 1024
DEPTH = 2
EPS = 1e-6
N_BRANCHES = 3
D_SSM = 768
SSM_GROUP = 16
SSM_GROUPS = 48
SSM_STATE = 64
ATTN_HEAD_DIM = 64
ATTN_HEADS_PER_GROUP = 4
ATTN_CONFIGS = ((128, 1), (512, 4), (2048, 16))
N_ATTN_HEADS = 12
D_ATTN = 768
ATTN_BLOCK = 128
NUM_BUCKETS = 32
REL_MAX_DISTANCE = 2048
NEG_INF = -1e30
MEM_HEADS = 4
MEM_HEAD_DIM = 128
D_MEM = 512
D_GROUP = ATTN_HEADS_PER_GROUP * ATTN_HEAD_DIM

LANES = 128
SUBLANES = 8
SSM_CHUNK = SUBLANES
SSM_TILE_GROUPS = LANES // SSM_GROUP
SSM_TILES = D_SSM // LANES
SSM_TILE_W = SSM_CHUNK * LANES
SSM_PAIRS_PER_TILE = SSM_TILE_GROUPS // 2
VMEM_LIMIT = 56 * 1024 * 1024

TM_INPROJ = 1024
TM_MERGE = 512
TQ_ATTN = 1024


def _rms(x, g):
    return x * lax.rsqrt(jnp.mean(x * x, axis=-1, keepdims=True) + EPS) * g


def _block_transpose(vs):
    n = len(vs)
    width = LANES // n
    block = lax.broadcasted_iota(jnp.int32, vs[0].shape, 1) // width
    d = n // 2
    while d >= 1:
        bit_set = (block & d) != 0
        new = list(vs)
        for i in range(n):
            if i & d == 0:
                a, b = vs[i], vs[i + d]
                new[i] = jnp.where(bit_set, pltpu.roll(b, d * width, 1), a)
                new[i + d] = jnp.where(bit_set, b, pltpu.roll(a, LANES - d * width, 1))
        vs = new
        d //= 2
    return vs


def _const_spec(shape):
    n = len(shape)
    return pl.BlockSpec(shape, lambda *_: (0,) * n, pipeline_mode=pl.Buffered(1))


def _memkv_kernel(mem_ref, g_ref, w_ref, k_ref, v_ref):
    h = _rms(mem_ref[0], g_ref[...]).astype(BF16)
    k_ref[0] = jnp.dot(h, w_ref[:, :D_MEM], preferred_element_type=F32).astype(BF16)
    v_ref[0] = jnp.dot(h, w_ref[:, D_MEM:], preferred_element_type=F32).astype(BF16)


def _mem_kv(mem, g, w_bf16):
    B, ML, _ = mem.shape
    return pl.pallas_call(
        _memkv_kernel,
        grid=(B,),
        in_specs=[pl.BlockSpec((1, ML, D_MODEL), lambda b: (b, 0, 0)),
                  _const_spec((1, D_MODEL)),
                  _const_spec((D_MODEL, 2 * D_MEM))],
        out_specs=[pl.BlockSpec((1, ML, D_MEM), lambda b: (b, 0, 0)),
                   pl.BlockSpec((1, ML, D_MEM), lambda b: (b, 0, 0))],
        out_shape=[jax.ShapeDtypeStruct((B, ML, D_MEM), BF16)] * 2,
        name="mem_kv",
    )(mem, g, w_bf16)


_IN_SIZES = (D_SSM, D_SSM, D_ATTN, D_ATTN, D_ATTN, D_ATTN, D_MEM, D_MEM, N_BRANCHES * D_MODEL)
_IN_OFFS = tuple(int(v) for v in np.concatenate([[0], np.cumsum(_IN_SIZES)]))
_IN_NAMES = ("u", "z_ssm", "q", "k", "v", "z_attn", "q_mem", "z_mem", "gates")
GATE_W = 512


def _w_in_spec(name, layer, part=0):
    idx = _IN_NAMES.index(name)
    width = GATE_W if name == "gates" else _IN_SIZES[idx]
    block, rem = divmod(_IN_OFFS[idx], width)
    assert rem == 0
    return pl.BlockSpec((None, D_MODEL, width), lambda *_: (layer, 0, block + part),
                        pipeline_mode=pl.Buffered(1))


def _inproj_kernel(x_ref, g_ref, wu_ref, wq_ref, wk_ref, wv_ref, wqm_ref, u2_ref, *rest):
    qkv_refs, qm_ref, scr = rest[:9], rest[9], rest[10]
    h = _rms(x_ref[0], g_ref[...]).astype(BF16)
    tm = h.shape[0]

    def to_scratch(p):
        for j in range(p.shape[1] // LANES):
            scr[j] = p[:, j * LANES:(j + 1) * LANES]

    to_scratch(jnp.dot(h, wu_ref[...], preferred_element_type=F32))
    for j in range(SSM_TILES):
        steps = [scr[j, pl.ds(s, tm // SSM_CHUNK, stride=SSM_CHUNK), :] for s in range(SSM_CHUNK)]
        for gl, blk in enumerate(_block_transpose(steps)):
            lo = j * SSM_TILE_W + gl * LANES
            u2_ref[0, :, lo:lo + LANES] = blk.astype(BF16)
    for idx, (w_ref, scale) in enumerate(((wq_ref, ATTN_HEAD_DIM ** -0.5), (wk_ref, None), (wv_ref, None))):
        p = jnp.dot(h, w_ref[...], preferred_element_type=F32)
        if scale is not None:
            p = p * scale
        to_scratch(p)
        tiles = D_GROUP // LANES
        for gi, (_, r) in enumerate(ATTN_CONFIGS):
            o_ref = qkv_refs[3 * idx + gi]
            for s in range(r):
                for c in range(tiles):
                    piece = scr[gi * tiles + c, pl.ds(s, tm // r, stride=r), :]
                    o_ref[0, s, :, c * LANES:(c + 1) * LANES] = piece.astype(BF16)
    qm_ref[0] = jnp.dot(h, wqm_ref[...], preferred_element_type=F32).astype(BF16)


def _in_proj(x, g, w_in_bf, layer):
    B, L, _ = x.shape
    tm = TM_INPROJ
    out_specs = [pl.BlockSpec((1, tm // SSM_CHUNK, D_SSM * SSM_CHUNK), lambda b, i: (b, i, 0))]
    out_shape = [jax.ShapeDtypeStruct((B, L // SSM_CHUNK, D_SSM * SSM_CHUNK), BF16)]
    for _ in range(3):
        for _, r in ATTN_CONFIGS:
            out_specs.append(pl.BlockSpec((1, r, tm // r, D_GROUP), lambda b, i: (b, 0, i, 0)))
            out_shape.append(jax.ShapeDtypeStruct((B, r, L // r, D_GROUP), BF16))
    out_specs.append(pl.BlockSpec((1, tm, D_MEM), lambda b, i: (b, i, 0)))
    out_shape.append(jax.ShapeDtypeStruct((B, L, D_MEM), BF16))
    return pl.pallas_call(
        _inproj_kernel,
        grid=(B, L // tm),
        in_specs=[pl.BlockSpec((1, tm, D_MODEL), lambda b, i: (b, i, 0)),
                  _const_spec((1, D_MODEL))]
                 + [_w_in_spec(n, layer) for n in ("u", "q", "k", "v", "q_mem")],
        out_specs=out_specs,
        out_shape=out_shape,
        scratch_shapes=[pltpu.VMEM((D_SSM // LANES, tm, LANES), F32)],
        compiler_params=pltpu.CompilerParams(vmem_limit_bytes=VMEM_LIMIT),
        name="in_proj",
    )(x, g, *([w_in_bf] * 5))


def _cmul(ar, ai, br, bi):
    return ar * br - ai * bi, ar * bi + ai * br


def _ssm_prep(lre, lim, log_dt, b_re, b_im, c_re, c_im, d):
    P, H, C = SSM_STATE, SSM_GROUP, SSM_CHUNK
    G = lre.shape[0]
    NT = G // SSM_TILE_GROUPS
    dt = jnp.exp(log_dt)[:, None]
    mag = jnp.exp(lre * dt)
    ar, ai = mag * jnp.cos(lim * dt), mag * jnp.sin(lim * dt)
    den = lre * lre + lim * lim
    nr, ni = ar - 1.0, ai
    fr = (nr * lre + ni * lim) / den
    fi = (ni * lre - nr * lim) / den
    prs, pis = [jnp.ones_like(ar)], [jnp.zeros_like(ai)]
    for _ in range(C):
        r_, i_ = _cmul(prs[-1], pis[-1], ar, ai)
        prs.append(r_)
        pis.append(i_)

    def halves(left, right):
        return jnp.stack([jnp.concatenate([l, r], axis=-1) for l, r in zip(left, right)], axis=1)

    ws = [_cmul(prs[C - 1 - s], pis[C - 1 - s], fr, fi) for s in range(C)]
    s_tab = halves([w[0] for w in ws] + [-w[1] for w in ws], [w[0] for w in ws] + [w[1] for w in ws])
    r_tab = halves(prs[1:] + [-p for p in pis[1:]], [-p for p in prs[1:]] + [-p for p in pis[1:]])
    bt_re, bt_im = b_re.transpose(0, 2, 1), b_im.transpose(0, 2, 1)
    b_tab = jnp.concatenate([jnp.concatenate([bt_re, bt_im], axis=-1),
                             jnp.concatenate([bt_im, bt_re], axis=-1)], axis=1)
    d_rows = jnp.zeros((G, SUBLANES, LANES), F32).at[:, 0, :].set(jnp.tile(d, (1, C)))
    c_tab = jnp.concatenate([jnp.concatenate([c_re, c_im], axis=-1), jnp.concatenate([c_im, c_re], axis=-1),
                             jnp.concatenate([c_re, -c_im], axis=-1), d_rows], axis=1)
    PT = SSM_PAIRS_PER_TILE
    alr, ali = prs[C], pis[C]
    qrs, qis = [jnp.ones_like(alr)], [jnp.zeros_like(ali)]
    for _ in range(SUBLANES):
        r_, i_ = _cmul(qrs[-1], qis[-1], alr, ali)
        qrs.append(r_)
        qis.append(i_)

    def lay(zr, zi):
        z = jnp.stack([zr.reshape(NT, PT, LANES), zi.reshape(NT, PT, LANES)], axis=2)
        return z.reshape(NT, SSM_TILE_W)

    rows = jnp.arange(SUBLANES)[:, None, None]
    tabs = []
    for dsh in (1, 2, 4):
        full = jnp.broadcast_to(lay(qrs[dsh], qis[dsh])[None], (SUBLANES, NT, SSM_TILE_W))
        tabs.append(jnp.where(rows >= dsh, full, 0.0))
    tabs.append(jnp.stack([lay(qrs[i], qis[i]) for i in range(SUBLANES)]))
    tabs.append(jnp.broadcast_to(lay(qrs[SUBLANES], qis[SUBLANES])[None], (SUBLANES, NT, SSM_TILE_W)))
    tab = jnp.stack(tabs).transpose(2, 0, 1, 3)

    def per_tile(a):
        return a.reshape((NT, SSM_TILE_GROUPS) + a.shape[1:])

    return per_tile(s_tab), per_tile(r_tab), per_tile(b_tab), per_tile(c_tab), tab.astype(F32)


def _ssm_expand(st_ref, rt_ref, bt_ref, ct_ref, m_scr, s_scr, r_scr):
    H, P, C = SSM_GROUP, SSM_STATE, SSM_CHUNK
    lane = lax.broadcasted_iota(jnp.int32, (LANES, LANES), 1)
    low = lane < P
    step_of_lane = lane // H
    diag = (lax.broadcasted_iota(jnp.int32, (H, LANES), 0) == lax.broadcasted_iota(jnp.int32, (H, LANES), 1) % H)
    nt_dims = (((1,), (1,)), ((), ()))
    m_scr[...] = jnp.zeros(m_scr.shape, m_scr.dtype)
    r_scr[...] = jnp.zeros(r_scr.shape, r_scr.dtype)
    for gl in range(SSM_TILE_GROUPS):
        q, gl2 = divmod(gl, 2)
        own = slice(gl2 * LANES, (gl2 + 1) * LANES)
        b_cat, b_swp = bt_ref[gl, 0:H], bt_ref[gl, H:2 * H]
        c_cat, c_swp, c_zero = ct_ref[gl, 0:H], ct_ref[gl, H:2 * H], ct_ref[gl, 2 * H:3 * H]
        d_lanes = ct_ref[gl, 3 * H:3 * H + 1]
        s_g = jnp.concatenate([st_ref[gl, s:s + 1] * b_cat + st_ref[gl, C + s:C + s + 1] * b_swp
                               for s in range(C)], axis=0)
        rt_g = jnp.concatenate([rt_ref[gl, t:t + 1] * c_cat + rt_ref[gl, C + t:C + t + 1] * c_swp
                                for t in range(C)], axis=0)
        lags = lax.dot_general(s_g, jnp.tile(c_zero, (C, 1)), nt_dims, precision=lax.Precision.HIGHEST,
                               preferred_element_type=F32)
        skip = jnp.where(diag, d_lanes, 0.0)
        lags = lags + jnp.concatenate([jnp.zeros((LANES - H, LANES), F32), skip], axis=0)
        m_g = jnp.zeros((LANES, LANES), F32)
        for t in range(C):
            up = (C - 1 - t) * H
            shifted = lags if up == 0 else jnp.concatenate([lags[up:], jnp.zeros((up, LANES), F32)], axis=0)
            m_g = jnp.where(step_of_lane == t, shifted, m_g)
        m_scr[q, own, own] = m_g.astype(BF16)
        swapped = pltpu.roll(s_g, P, 1)
        mine = low if gl2 == 0 else ~low
        s_scr[q, own, 0:LANES] = jnp.where(mine, s_g if gl2 == 0 else swapped, 0.0).astype(BF16)
        s_scr[q, own, LANES:2 * LANES] = jnp.where(mine, swapped if gl2 == 0 else s_g, 0.0).astype(BF16)
        r_g = rt_g.T
        for r in range(2):
            rows = slice(r * LANES + gl2 * P, r * LANES + (gl2 + 1) * P)
            r_scr[q, rows, own] = r_g[r * P:(r + 1) * P, :].astype(BF16)


def _ssm_kernel(u_ref, st_ref, rt_ref, bt_ref, ct_ref, tab_ref, y_ref, m_scr, s_scr, r_scr, upd_ref, xin_ref):
    @pl.when(pl.program_id(1) == 0)
    def _():
        _ssm_expand(st_ref, rt_ref, bt_ref, ct_ref, m_scr, s_scr, r_scr)

    pw = 2 * LANES
    for q in range(SSM_PAIRS_PER_TILE):
        cols = slice(q * pw, (q + 1) * pw)
        upd_ref[:, cols] = jnp.dot(u_ref[:, cols], s_scr[q], preferred_element_type=F32)
    n_blocks = u_ref.shape[0] // SUBLANES
    row = lax.broadcasted_iota(jnp.int32, (SUBLANES, LANES), 0)

    def scan(i, carry):
        r0 = pl.multiple_of(i * SUBLANES, SUBLANES)
        rows = pl.ds(r0, SUBLANES)
        out = []
        for q in range(SSM_PAIRS_PER_TILE):
            re, im = slice(q * pw, q * pw + LANES), slice(q * pw + LANES, (q + 1) * pw)
            er, ei = carry[2 * q], carry[2 * q + 1]
            vr, vi = upd_ref[rows, re], upd_ref[rows, im]
            for lvl, dsh in enumerate((1, 2, 4)):
                cr, ci = tab_ref[lvl, :, re], tab_ref[lvl, :, im]
                sr, si = pltpu.roll(vr, dsh, 0), pltpu.roll(vi, dsh, 0)
                vr, vi = vr + (cr * sr - ci * si), vi + (cr * si + ci * sr)
            sr = jnp.where(row == 0, 0.0, pltpu.roll(vr, 1, 0))
            si = jnp.where(row == 0, 0.0, pltpu.roll(vi, 1, 0))
            pr, pi_ = tab_ref[3, :, re], tab_ref[3, :, im]
            xin_ref[rows, re] = sr + (pr * er - pi_ * ei)
            xin_ref[rows, im] = si + (pr * ei + pi_ * er)
            a8r, a8i = tab_ref[4, :, re], tab_ref[4, :, im]
            lr = jnp.broadcast_to(vr[SUBLANES - 1:SUBLANES, :], (SUBLANES, LANES))
            li = jnp.broadcast_to(vi[SUBLANES - 1:SUBLANES, :], (SUBLANES, LANES))
            out += [a8r * er - a8i * ei + lr, a8r * ei + a8i * er + li]
        return tuple(out)

    zero = (jnp.zeros((SUBLANES, LANES), F32),) * (2 * SSM_PAIRS_PER_TILE)

    lax.fori_loop(0, n_blocks, scan, zero)
    for q in range(SSM_PAIRS_PER_TILE):
        cols = slice(q * pw, (q + 1) * pw)
        y = jnp.dot(u_ref[:, cols], m_scr[q], preferred_element_type=F32)
        y_ref[:, cols] = y + jnp.dot(xin_ref[:, cols].astype(BF16), r_scr[q], preferred_element_type=F32)


def _ssm(u2, tables, layer, batch):
    rows = u2.shape[0] // batch
    w = SSM_TILE_W
    pw = 2 * LANES

    def tab_spec(a):
        nd = a.ndim - 2
        return pl.BlockSpec((None, None) + a.shape[2:], lambda j, b: (layer, j) + (0,) * nd)

    return pl.pallas_call(
        _ssm_kernel,
        grid=(SSM_TILES, batch),
        in_specs=[pl.BlockSpec((rows, w), lambda j, b: (b, j))] + [tab_spec(a) for a in tables],
        out_specs=pl.BlockSpec((rows, w), lambda j, b: (b, j)),
        out_shape=jax.ShapeDtypeStruct(u2.shape, F32),
        scratch_shapes=[pltpu.VMEM((SSM_PAIRS_PER_TILE, pw, pw), BF16)] * 3 + [pltpu.VMEM((rows, w), F32)] * 2,
        compiler_params=pltpu.CompilerParams(vmem_limit_bytes=VMEM_LIMIT),
        name="ssm",
    )(u2, *tables)


def _rel_bucket(dist):
    n = jnp.maximum(dist, 0)
    max_exact = NUM_BUCKETS // 2
    n_f = jnp.maximum(n, 1).astype(F32)
    large = max_exact + (jnp.log(n_f / max_exact) / math.log(REL_MAX_DISTANCE / max_exact)
                         * (NUM_BUCKETS - max_exact)).astype(jnp.int32)
    large = jnp.minimum(large, NUM_BUCKETS - 1)
    return jnp.where(n < max_exact, n, large)


def _bias_mask(rel_bias_g, window, dilation):
    span = window // dilation
    blk, period = ATTN_BLOCK, 4 * ATTN_BLOCK
    delta = jnp.arange(period)
    bias = rel_bias_g[_rel_bucket(delta * dilation)].astype(F32)
    by_delta = jnp.where((delta <= span)[:, None], bias, NEG_INF).T
    heads = by_delta.shape[0]
    skew = jnp.tile(by_delta, (1, blk))[:, :blk * (period - 1)].reshape(heads, blk, period - 1)
    return skew[:, ::-1, 2 * blk - 1::-1]


def _attn_kernel(q_ref, kc_ref, kp_ref, vc_ref, vp_ref, bm_ref, o_ref, lse_ref):
    blk = ATTN_BLOCK
    first_valid_col = jnp.where(pl.program_id(1) == 0, blk, 0)
    lane = lax.broadcasted_iota(jnp.int32, (blk, LANES), 1)
    low = lane < ATTN_HEAD_DIM
    col = lax.broadcasted_iota(jnp.int32, (blk, 2 * blk), 1)
    n_sub = q_ref.shape[1] // blk
    dn = (((1,), (1,)), ((), ()))
    for sq in range(q_ref.shape[0]):
        for n in range(n_sub):
            rows = slice(n * blk, (n + 1) * blk)
            lse_w = jnp.zeros((blk, LANES), F32)
            for pair in range(ATTN_HEADS_PER_GROUP // 2):
                cols = slice(pair * LANES, (pair + 1) * LANES)
                q32 = q_ref[sq, rows, cols].astype(F32)
                if n == 0:
                    kk = jnp.concatenate([kp_ref[sq, :, cols], kc_ref[sq, rows, cols]], axis=0)
                    vv = jnp.concatenate([vp_ref[sq, :, cols], vc_ref[sq, rows, cols]], axis=0)
                else:
                    kk = kc_ref[sq, (n - 1) * blk:(n + 1) * blk, cols]
                    vv = vc_ref[sq, (n - 1) * blk:(n + 1) * blk, cols]
                outs = []
                for sub in range(2):
                    hh = 2 * pair + sub
                    qh = jnp.where(low if sub == 0 else ~low, q32, 0.0).astype(BF16)
                    s = lax.dot_general(qh, kk, dn, preferred_element_type=F32) + bm_ref[hh]
                    if n == 0:
                        s = jnp.where(col >= first_valid_col, s, NEG_INF)
                    m = jnp.max(s, axis=-1, keepdims=True)
                    p = jnp.exp(s - m)
                    l = jnp.sum(p, axis=-1, keepdims=True)
                    o = jnp.dot(p.astype(BF16), vv, preferred_element_type=F32) * (1.0 / l)
                    outs.append(o)
                    lse_w = jnp.where(lane == hh, m + jnp.log(l), lse_w)
                o_ref[sq, rows, cols] = jnp.where(low, outs[0], outs[1]).astype(BF16)
            lse_ref[sq, rows, :] = lse_w


def _attention(q, k, v, bias_mask):
    ns, m_len, _ = q.shape
    tq = min(TQ_ATTN, m_len)
    per_step = TQ_ATTN // tq
    per = tq // ATTN_BLOCK
    cur = pl.BlockSpec((per_step, tq, D_GROUP), lambda s, i: (s, i, 0))
    prev = pl.BlockSpec((per_step, ATTN_BLOCK, D_GROUP), lambda s, i: (s, jnp.maximum(i * per - 1, 0), 0))
    return pl.pallas_call(
        _attn_kernel,
        grid=(ns // per_step, m_len // tq),
        in_specs=[cur, cur, prev, cur, prev,
                  _const_spec((ATTN_HEADS_PER_GROUP, ATTN_BLOCK, 2 * ATTN_BLOCK))],
        out_specs=[cur, pl.BlockSpec((per_step, tq, LANES), lambda s, i: (s, i, 0))],
        out_shape=[jax.ShapeDtypeStruct((ns, m_len, D_GROUP), BF16),
                   jax.ShapeDtypeStruct((ns, m_len, LANES), F32)],
        name="attn",
    )(q, k, k, v, v, bias_mask)


def _merge_kernel(final, x_ref, y2_ref, o0_ref, o1_ref, o2_ref, l0_ref, l1_ref, l2_ref, qm_ref, km_ref, vm_ref,
                  g_ref, wzs_ref, wza_ref, wzm_ref, wg0_ref, wg1_ref, wg2_ref, wg3_ref, wg4_ref, wg5_ref,
                  bg_ref, wglu_ref, bglu_ref, wbs_ref, wba_ref, wbm_ref, wout_ref, fg_ref, out_ref,
                  y_scr, o_scr, l_scr):
    i = pl.program_id(1)
    n_blocks = pl.num_programs(1) - 1
    tm = x_ref.shape[1]
    tiles = D_GROUP // LANES
    dilated = ((1, o1_ref, l1_ref), (2, o2_ref, l2_ref))

    def prepare(slot):
        for j in range(SSM_TILES):
            groups = [y2_ref[0, :, j * SSM_TILE_W + gl * LANES:j * SSM_TILE_W + (gl + 1) * LANES]
                      for gl in range(SSM_TILE_GROUPS)]
            for t, blk in enumerate(_block_transpose(groups)):
                y_scr[slot, j, pl.ds(t, tm // SSM_CHUNK, stride=SSM_CHUNK), :] = blk
        for gi, o_ref, l_ref in dilated:
            r = ATTN_CONFIGS[gi][1]
            for s in range(r):
                rows = pl.ds(s, tm // r, stride=r)
                for c in range(tiles):
                    o_scr[slot, (gi - 1) * tiles + c, rows, :] = o_ref[0, s, :, c * LANES:(c + 1) * LANES].astype(F32)
                l_scr[slot, gi - 1, rows, :] = l_ref[0, s]

    def compute(slot):
        x = x_ref[0]
        h = _rms(x, g_ref[...]).astype(BF16)

        def hdot(w_ref):
            return jnp.dot(h, w_ref[...], preferred_element_type=F32)

        yg = jax.nn.gelu(jnp.concatenate([y_scr[slot, j] for j in range(SSM_TILES)], axis=-1))
        t = jnp.dot(yg.astype(BF16), wglu_ref[...], preferred_element_type=F32) + bglu_ref[...]
        o_ssm = yg * jax.nn.sigmoid(t) * jax.nn.silu(hdot(wzs_ref))
        p_ssm = jnp.dot(o_ssm.astype(BF16), wbs_ref[...], preferred_element_type=F32)

        ls = (l0_ref[0, 0], l_scr[slot, 0], l_scr[slot, 1])
        os_ = [o0_ref[0, 0].astype(F32)]
        os_ += [jnp.concatenate([o_scr[slot, g * tiles + c] for c in range(tiles)], axis=-1) for g in range(2)]
        mx = jnp.maximum(jnp.maximum(ls[0], ls[1]), ls[2])
        es = [jnp.exp(l - mx) for l in ls]
        inv = 1.0 / (es[0] + es[1] + es[2])
        head_of_lane = lax.broadcasted_iota(jnp.int32, (tm, D_GROUP), 1) // ATTN_HEAD_DIM
        parts = []
        for e, o_g in zip(es, os_):
            alpha = e * inv
            wide = jnp.zeros((tm, D_GROUP), F32)
            for j in range(ATTN_HEADS_PER_GROUP):
                wide = jnp.where(head_of_lane == j, alpha[:, j:j + 1], wide)
            parts.append(o_g * wide)
        o_attn = jnp.concatenate(parts, axis=-1) * jax.nn.silu(hdot(wza_ref))
        p_attn = jnp.dot(o_attn.astype(BF16), wba_ref[...], preferred_element_type=F32)

        dn = (((1,), (1,)), ((), ()))
        heads = []
        for hd in range(MEM_HEADS):
            cols = slice(hd * MEM_HEAD_DIM, (hd + 1) * MEM_HEAD_DIM)
            s = lax.dot_general(qm_ref[0, :, cols], km_ref[0, :, cols], dn, preferred_element_type=F32)
            s = s * (MEM_HEAD_DIM ** -0.5)
            m = jnp.max(s, axis=-1, keepdims=True)
            p = jnp.exp(s - m)
            l = jnp.sum(p, axis=-1, keepdims=True)
            heads.append(jnp.dot(p.astype(BF16), vm_ref[0, :, cols], preferred_element_type=F32) * (1.0 / l))
        o_mem = jnp.concatenate(heads, axis=-1) * jax.nn.silu(hdot(wzm_ref))
        p_mem = jnp.dot(o_mem.astype(BF16), wbm_ref[...], preferred_element_type=F32)

        gate_refs = (wg0_ref, wg1_ref, wg2_ref, wg3_ref, wg4_ref, wg5_ref)
        per_branch = D_MODEL // GATE_W
        halves = []
        for part in range(per_branch):
            acc = jnp.zeros((tm, GATE_W), F32)
            for br, p_br in enumerate((p_ssm, p_attn, p_mem)):
                k = br * per_branch + part
                gate = jax.nn.sigmoid(hdot(gate_refs[k]) + bg_ref[:, k * GATE_W:(k + 1) * GATE_W])
                acc = acc + gate * p_br[:, part * GATE_W:(part + 1) * GATE_W]
            halves.append(acc)
        merged = jnp.concatenate(halves, axis=-1)
        xn = x + jnp.dot(merged.astype(BF16), wout_ref[...], preferred_element_type=F32)
        if final:
            xn = _rms(xn, fg_ref[...])
        out_ref[0] = xn

    slot = i % 2

    @pl.when(i == 0)
    def _():
        prepare(slot)

    @pl.when(jnp.logical_and(i > 0, i < n_blocks))
    def _():
        compute(1 - slot)
        prepare(slot)

    @pl.when(i == n_blocks)
    def _():
        compute(1 - slot)


def _merge(final, layer, x, y2, o_groups, lse_groups, qm, k_mem, v_mem, g, w_in_bf, bg, wglu, bglu, wbs, wba,
           wbm, wout, fg):
    B, L, _ = x.shape
    tm = TM_MERGE
    n_blocks = L // tm

    def cur(i):
        return jnp.maximum(i - 1, 0)

    def nxt(i):
        return jnp.minimum(i, n_blocks - 1)

    def rows(w):
        return pl.BlockSpec((1, tm, w), lambda b, i: (b, cur(i), 0))

    def dec(r, w, which):
        return pl.BlockSpec((1, r, tm // r, w), lambda b, i: (b, 0, which(i), 0))

    mem_spec = pl.BlockSpec((1,) + k_mem.shape[1:], lambda b, i: (b, 0, 0))
    rs = [r for _, r in ATTN_CONFIGS]
    assert rs[0] == 1
    n_gate = N_BRANCHES * D_MODEL // GATE_W
    in_specs = ([rows(D_MODEL),
                 pl.BlockSpec((1, tm // SSM_CHUNK, D_SSM * SSM_CHUNK), lambda b, i: (b, nxt(i), 0))]
                + [dec(r, D_GROUP, cur if r == 1 else nxt) for r in rs]
                + [dec(r, LANES, cur if r == 1 else nxt) for r in rs]
                + [rows(D_MEM), mem_spec, mem_spec, _const_spec(g.shape)]
                + [_w_in_spec(n, layer) for n in ("z_ssm", "z_attn", "z_mem")]
                + [_w_in_spec("gates", layer, part) for part in range(n_gate)]
                + [_const_spec(a.shape) for a in (bg, wglu, bglu, wbs, wba, wbm, wout, fg)])
    n_dil = len(rs) - 1
    return pl.pallas_call(
        functools.partial(_merge_kernel, final),
        grid=(B, n_blocks + 1),
        in_specs=in_specs,
        out_specs=rows(D_MODEL),
        out_shape=jax.ShapeDtypeStruct((B, L, D_MODEL), F32),
        scratch_shapes=[pltpu.VMEM((2, D_SSM // LANES, tm, LANES), F32),
                        pltpu.VMEM((2, n_dil * D_GROUP // LANES, tm, LANES), F32),
                        pltpu.VMEM((2, n_dil, tm, LANES), F32)],
        compiler_params=pltpu.CompilerParams(vmem_limit_bytes=VMEM_LIMIT),
        name="merge",
    )(x, y2, *o_groups, *lse_groups, qm, k_mem, v_mem, g, *([w_in_bf] * (3 + n_gate)), bg, wglu, bglu, wbs, wba,
      wbm, wout, fg)


def kernel(x, mem, norm_g, mem_norm_g, w_in, b_gate, ssm_lambda_re, ssm_lambda_im, ssm_log_dt, ssm_b_re,
           ssm_b_im, ssm_c_re, ssm_c_im, ssm_d, w_glu, b_glu, w_mem_kv, w_br_ssm, w_br_attn, w_br_mem,
           w_out, rel_bias, final_norm_g):
    B, L, _ = x.shape
    assert L % (ATTN_CONFIGS[-1][1] * ATTN_BLOCK) == 0 and L % TM_INPROJ == 0 and L % TM_MERGE == 0
    bias_masks = [_bias_mask(rel_bias[:, gi * ATTN_HEADS_PER_GROUP:(gi + 1) * ATTN_HEADS_PER_GROUP], win, dil)
                  for gi, (win, dil) in enumerate(ATTN_CONFIGS)]
    fg = final_norm_g.reshape(1, D_MODEL)
    w_in_bf = w_in.astype(BF16)
    def groups_of_all_layers(a):
        return a.reshape((DEPTH * SSM_GROUPS,) + a.shape[2:])

    ssm_tables = _ssm_prep(*(groups_of_all_layers(a) for a in (
        ssm_lambda_re, ssm_lambda_im, ssm_log_dt, ssm_b_re, ssm_b_im, ssm_c_re, ssm_c_im,
        ssm_d.reshape(DEPTH, SSM_GROUPS, SSM_GROUP))))
    ssm_tables = [t.reshape((DEPTH, SSM_TILES) + t.shape[1:]) for t in ssm_tables]
    n_chunks = B * L // SSM_CHUNK
    for layer in range(DEPTH):
        g = norm_g[layer].reshape(1, D_MODEL)
        k_mem, v_mem = _mem_kv(mem, mem_norm_g[layer].reshape(1, D_MODEL), w_mem_kv[layer].astype(BF16))
        u2, *qkv, qm = _in_proj(x, g, w_in_bf, layer)

        y = _ssm(u2.reshape(n_chunks, D_SSM * SSM_CHUNK), ssm_tables, layer, batch=B)
        y = y.reshape(B, L // SSM_CHUNK, D_SSM * SSM_CHUNK)

        o_groups, lse_groups = [], []
        for gi, (_, r) in enumerate(ATTN_CONFIGS):
            m_len = L // r
            q_g, k_g, v_g = (qkv[3 * idx + gi].reshape(B * r, m_len, D_GROUP) for idx in range(3))
            o_g, lse_g = _attention(q_g, k_g, v_g, bias_masks[gi])
            o_groups.append(o_g.reshape(B, r, m_len, D_GROUP))
            lse_groups.append(lse_g.reshape(B, r, m_len, LANES))

        x = _merge(layer == DEPTH - 1, layer, x, y, o_groups, lse_groups, qm, k_mem, v_mem, g, w_in_bf,
                   b_gate[layer].reshape(1, -1), w_glu[layer].astype(BF16), b_glu[layer].reshape(1, -1),
                   w_br_ssm[layer].astype(BF16), w_br_attn[layer].astype(BF16), w_br_mem[layer].astype(BF16),
                   w_out[layer].astype(BF16), fg)
    return x
```

```python
import functools
import math

import jax
import jax.numpy as jnp
import numpy as np
from jax import lax
from jax.experimental import pallas as pl
from jax.experimental.pallas import tpu as pltpu

F32 = jnp.float32
BF16 = jnp.bfloat16

D_MODEL = 1024
DEPTH = 2
EPS = 1e-6
N_BRANCHES = 3
D_SSM = 768
SSM_GROUP = 16
SSM_GROUPS = 48
SSM_STATE = 64
ATTN_HEAD_DIM = 64
ATTN_HEADS_PER_GROUP = 4
ATTN_CONFIGS = ((128, 1), (512, 4), (2048, 16))
N_ATTN_HEADS = 12
D_ATTN = 768
ATTN_BLOCK = 128
NUM_BUCKETS = 32
REL_MAX_DISTANCE = 2048
NEG_INF = -1e30
MEM_HEADS = 4
MEM_HEAD_DIM = 128
D_MEM = 512
D_GROUP = ATTN_HEADS_PER_GROUP * ATTN_HEAD_DIM

LANES = 128
SUBLANES = 8
SSM_CHUNK = SUBLANES
SSM_TILE_GROUPS = LANES // SSM_GROUP
SSM_TILES = D_SSM // LANES
SSM_TILE_W = SSM_CHUNK * LANES
SSM_PAIRS_PER_TILE = SSM_TILE_GROUPS // 2
VMEM_LIMIT = 56 * 1024 * 1024

TM_INPROJ = 1024
TM_MERGE = 512
TQ_ATTN = 2048


def _rms(x, g):
    return x * lax.rsqrt(jnp.mean(x * x, axis=-1, keepdims=True) + EPS) * g


def _block_transpose(vs):
    n = len(vs)
    width = LANES // n
    block = lax.broadcasted_iota(jnp.int32, vs[0].shape, 1) // width
    d = n // 2
    while d >= 1:
        bit_set = (block & d) != 0
        new = list(vs)
        for i in range(n):
            if i & d == 0:
                a, b = vs[i], vs[i + d]
                new[i] = jnp.where(bit_set, pltpu.roll(b, d * width, 1), a)
                new[i + d] = jnp.where(bit_set, b, pltpu.roll(a, LANES - d * width, 1))
        vs = new
        d //= 2
    return vs


def _const_spec(shape):
    n = len(shape)
    return pl.BlockSpec(shape, lambda *_: (0,) * n, pipeline_mode=pl.Buffered(1))


def _memkv_kernel(mem_ref, g_ref, w_ref, k_ref, v_ref):
    h = _rms(mem_ref[0], g_ref[...]).astype(BF16)
    k_ref[0] = jnp.dot(h, w_ref[:, :D_MEM], preferred_element_type=F32).astype(BF16)
    v_ref[0] = jnp.dot(h, w_ref[:, D_MEM:], preferred_element_type=F32).astype(BF16)


def _mem_kv(mem, g, w_bf16):
    B, ML, _ = mem.shape
    return pl.pallas_call(
        _memkv_kernel,
        grid=(B,),
        in_specs=[pl.BlockSpec((1, ML, D_MODEL), lambda b: (b, 0, 0)),
                  _const_spec((1, D_MODEL)),
                  _const_spec((D_MODEL, 2 * D_MEM))],
        out_specs=[pl.BlockSpec((1, ML, D_MEM), lambda b: (b, 0, 0)),
                   pl.BlockSpec((1, ML, D_MEM), lambda b: (b, 0, 0))],
        out_shape=[jax.ShapeDtypeStruct((B, ML, D_MEM), BF16)] * 2,
        name="mem_kv",
    )(mem, g, w_bf16)


_IN_SIZES = (D_SSM, D_SSM, D_ATTN, D_ATTN, D_ATTN, D_ATTN, D_MEM, D_MEM, N_BRANCHES * D_MODEL)
_IN_OFFS = tuple(int(v) for v in np.concatenate([[0], np.cumsum(_IN_SIZES)]))
_IN_NAMES = ("u", "z_ssm", "q", "k", "v", "z_attn", "q_mem", "z_mem", "gates")
GATE_W = 512


def _w_in_spec(name, layer, part=0):
    idx = _IN_NAMES.index(name)
    width = GATE_W if name == "gates" else _IN_SIZES[idx]
    block, rem = divmod(_IN_OFFS[idx], width)
    assert rem == 0
    return pl.BlockSpec((None, D_MODEL, width), lambda *_: (layer, 0, block + part),
                        pipeline_mode=pl.Buffered(1))


def _inproj_kernel(x_ref, g_ref, wu_ref, wq_ref, wk_ref, wv_ref, wqm_ref, u2_ref, *rest):
    qkv_refs, qm_ref, scr = rest[:9], rest[9], rest[10]
    h = _rms(x_ref[0], g_ref[...]).astype(BF16)
    tm = h.shape[0]

    def to_scratch(p):
        for j in range(p.shape[1] // LANES):
            scr[j] = p[:, j * LANES:(j + 1) * LANES]

    to_scratch(jnp.dot(h, wu_ref[...], preferred_element_type=F32))
    for j in range(SSM_TILES):
        steps = [scr[j, pl.ds(s, tm // SSM_CHUNK, stride=SSM_CHUNK), :] for s in range(SSM_CHUNK)]
        for gl, blk in enumerate(_block_transpose(steps)):
            lo = j * SSM_TILE_W + gl * LANES
            u2_ref[0, :, lo:lo + LANES] = blk.astype(BF16)
    for idx, (w_ref, scale) in enumerate(((wq_ref, ATTN_HEAD_DIM ** -0.5), (wk_ref, None), (wv_ref, None))):
        p = jnp.dot(h, w_ref[...], preferred_element_type=F32)
        if scale is not None:
            p = p * scale
        to_scratch(p)
        tiles = D_GROUP // LANES
        for gi, (_, r) in enumerate(ATTN_CONFIGS):
            o_ref = qkv_refs[3 * idx + gi]
            for s in range(r):
                for c in range(tiles):
                    piece = scr[gi * tiles + c, pl.ds(s, tm // r, stride=r), :]
                    o_ref[0, s, :, c * LANES:(c + 1) * LANES] = piece.astype(BF16)
    qm_ref[0] = jnp.dot(h, wqm_ref[...], preferred_element_type=F32).astype(BF16)


def _in_proj(x, g, w_in_bf, layer):
    B, L, _ = x.shape
    tm = TM_INPROJ
    out_specs = [pl.BlockSpec((1, tm // SSM_CHUNK, D_SSM * SSM_CHUNK), lambda b, i: (b, i, 0))]
    out_shape = [jax.ShapeDtypeStruct((B, L // SSM_CHUNK, D_SSM * SSM_CHUNK), BF16)]
    for _ in range(3):
        for _, r in ATTN_CONFIGS:
            out_specs.append(pl.BlockSpec((1, r, tm // r, D_GROUP), lambda b, i: (b, 0, i, 0)))
            out_shape.append(jax.ShapeDtypeStruct((B, r, L // r, D_GROUP), BF16))
    out_specs.append(pl.BlockSpec((1, tm, D_MEM), lambda b, i: (b, i, 0)))
    out_shape.append(jax.ShapeDtypeStruct((B, L, D_MEM), BF16))
    return pl.pallas_call(
        _inproj_kernel,
        grid=(B, L // tm),
        in_specs=[pl.BlockSpec((1, tm, D_MODEL), lambda b, i: (b, i, 0)),
                  _const_spec((1, D_MODEL))]
                 + [_w_in_spec(n, layer) for n in ("u", "q", "k", "v", "q_mem")],
        out_specs=out_specs,
        out_shape=out_shape,
        scratch_shapes=[pltpu.VMEM((D_SSM // LANES, tm, LANES), F32)],
        compiler_params=pltpu.CompilerParams(vmem_limit_bytes=VMEM_LIMIT),
        name="in_proj",
    )(x, g, *([w_in_bf] * 5))


def _cmul(ar, ai, br, bi):
    return ar * br - ai * bi, ar * bi + ai * br


def _ssm_prep(lre, lim, log_dt, b_re, b_im, c_re, c_im, d):
    P, H, C = SSM_STATE, SSM_GROUP, SSM_CHUNK
    G = lre.shape[0]
    NT = G // SSM_TILE_GROUPS
    dt = jnp.exp(log_dt)[:, None]
    mag = jnp.exp(lre * dt)
    ar, ai = mag * jnp.cos(lim * dt), mag * jnp.sin(lim * dt)
    den = lre * lre + lim * lim
    nr, ni = ar - 1.0, ai
    fr = (nr * lre + ni * lim) / den
    fi = (ni * lre - nr * lim) / den
    prs, pis = [jnp.ones_like(ar)], [jnp.zeros_like(ai)]
    for _ in range(C):
        r_, i_ = _cmul(prs[-1], pis[-1], ar, ai)
        prs.append(r_)
        pis.append(i_)

    def halves(left, right):
        return jnp.stack([jnp.concatenate([l, r], axis=-1) for l, r in zip(left, right)], axis=1)

    ws = [_cmul(prs[C - 1 - s], pis[C - 1 - s], fr, fi) for s in range(C)]
    s_tab = halves([w[0] for w in ws] + [-w[1] for w in ws], [w[0] for w in ws] + [w[1] for w in ws])
    r_tab = halves(prs[1:] + [-p for p in pis[1:]], [-p for p in prs[1:]] + [-p for p in pis[1:]])
    bt_re, bt_im = b_re.transpose(0, 2, 1), b_im.transpose(0, 2, 1)
    b_tab = jnp.concatenate([jnp.concatenate([bt_re, bt_im], axis=-1),
                             jnp.concatenate([bt_im, bt_re], axis=-1)], axis=1)
    d_rows = jnp.zeros((G, SUBLANES, LANES), F32).at[:, 0, :].set(jnp.tile(d, (1, C)))
    c_tab = jnp.concatenate([jnp.concatenate([c_re, c_im], axis=-1), jnp.concatenate([c_im, c_re], axis=-1),
                             jnp.concatenate([c_re, -c_im], axis=-1), d_rows], axis=1)
    PT = SSM_PAIRS_PER_TILE
    alr, ali = prs[C], pis[C]
    qrs, qis = [jnp.ones_like(alr)], [jnp.zeros_like(ali)]
    for _ in range(SUBLANES // 2):
        r_, i_ = _cmul(qrs[-1], qis[-1], alr, ali)
        qrs.append(r_)
        qis.append(i_)

    def lay(zr, zi):
        z = jnp.stack([zr.reshape(NT, PT, LANES), zi.reshape(NT, PT, LANES)], axis=2)
        return z.reshape(NT, SSM_TILE_W)

    rows = jnp.arange(SUBLANES)[:, None, None]
    tabs = []
    for dsh in (1, 2, 4):
        full = jnp.broadcast_to(lay(qrs[dsh], qis[dsh])[None], (SUBLANES, NT, SSM_TILE_W))
        tabs.append(jnp.where(rows >= dsh, full, 0.0))
    tabs.append(jnp.broadcast_to(lay(qrs[1], qis[1])[None], (SUBLANES, NT, SSM_TILE_W)))
    tab = jnp.stack(tabs).transpose(2, 0, 1, 3)

    def per_tile(a):
        return a.reshape((NT, SSM_TILE_GROUPS) + a.shape[1:])

    return per_tile(s_tab), per_tile(r_tab), per_tile(b_tab), per_tile(c_tab), tab.astype(F32)


def _ssm_expand(st_ref, rt_ref, bt_ref, ct_ref, m_scr, s_scr, r_scr):
    H, P, C = SSM_GROUP, SSM_STATE, SSM_CHUNK
    lane = lax.broadcasted_iota(jnp.int32, (LANES, LANES), 1)
    low = lane < P
    step_of_lane = lane // H
    diag = (lax.broadcasted_iota(jnp.int32, (H, LANES), 0) == lax.broadcasted_iota(jnp.int32, (H, LANES), 1) % H)
    nt_dims = (((1,), (1,)), ((), ()))
    m_scr[...] = jnp.zeros(m_scr.shape, m_scr.dtype)
    r_scr[...] = jnp.zeros(r_scr.shape, r_scr.dtype)
    for gl in range(SSM_TILE_GROUPS):
        q, gl2 = divmod(gl, 2)
        own = slice(gl2 * LANES, (gl2 + 1) * LANES)
        b_cat, b_swp = bt_ref[gl, 0:H], bt_ref[gl, H:2 * H]
        c_cat, c_swp, c_zero = ct_ref[gl, 0:H], ct_ref[gl, H:2 * H], ct_ref[gl, 2 * H:3 * H]
        d_lanes = ct_ref[gl, 3 * H:3 * H + 1]
        s_g = jnp.concatenate([st_ref[gl, s:s + 1] * b_cat + st_ref[gl, C + s:C + s + 1] * b_swp
                               for s in range(C)], axis=0)
        rt_g = jnp.concatenate([rt_ref[gl, t:t + 1] * c_cat + rt_ref[gl, C + t:C + t + 1] * c_swp
                                for t in range(C)], axis=0)
        lags = lax.dot_general(s_g, jnp.tile(c_zero, (C, 1)), nt_dims, precision=lax.Precision.HIGHEST,
                               preferred_element_type=F32)
        skip = jnp.where(diag, d_lanes, 0.0)
        lags = lags + jnp.concatenate([jnp.zeros((LANES - H, LANES), F32), skip], axis=0)
        m_g = jnp.zeros((LANES, LANES), F32)
        for t in range(C):
            up = (C - 1 - t) * H
            shifted = lags if up == 0 else jnp.concatenate([lags[up:], jnp.zeros((up, LANES), F32)], axis=0)
            m_g = jnp.where(step_of_lane == t, shifted, m_g)
        m_scr[q, own, own] = m_g.astype(BF16)
        swapped = pltpu.roll(s_g, P, 1)
        mine = low if gl2 == 0 else ~low
        s_scr[q, own, 0:LANES] = jnp.where(mine, s_g if gl2 == 0 else swapped, 0.0).astype(BF16)
        s_scr[q, own, LANES:2 * LANES] = jnp.where(mine, swapped if gl2 == 0 else s_g, 0.0).astype(BF16)
        r_g = rt_g.T
        for r in range(2):
            rows = slice(r * LANES + gl2 * P, r * LANES + (gl2 + 1) * P)
            r_scr[q, rows, own] = r_g[r * P:(r + 1) * P, :].astype(BF16)


def _ssm_kernel(u_ref, st_ref, rt_ref, bt_ref, ct_ref, tab_ref, y_ref, m_scr, s_scr, r_scr, upd_ref, xin_ref):
    @pl.when(pl.program_id(1) == 0)
    def _():
        _ssm_expand(st_ref, rt_ref, bt_ref, ct_ref, m_scr, s_scr, r_scr)

    pw = 2 * LANES
    for q in range(SSM_PAIRS_PER_TILE):
        cols = slice(q * pw, (q + 1) * pw)
        upd_ref[:, cols] = jnp.dot(u_ref[:, cols], s_scr[q], preferred_element_type=F32)
    n_blocks = u_ref.shape[0] // SUBLANES
    row = lax.broadcasted_iota(jnp.int32, (SUBLANES, LANES), 0)

    def scan(i, carry):
        r0 = pl.multiple_of(i * SUBLANES, SUBLANES)
        rows = pl.ds(r0, SUBLANES)
        out = []
        for q in range(SSM_PAIRS_PER_TILE):
            re, im = slice(q * pw, q * pw + LANES), slice(q * pw + LANES, (q + 1) * pw)
            ur, ui = upd_ref[rows, re], upd_ref[rows, im]
            vr = pltpu.roll(jnp.where(row == SUBLANES - 1, carry[2 * q], ur), 1, 0)
            vi = pltpu.roll(jnp.where(row == SUBLANES - 1, carry[2 * q + 1], ui), 1, 0)
            for lvl, dsh in enumerate((1, 2, 4)):
                cr, ci = tab_ref[lvl, :, re], tab_ref[lvl, :, im]
                sr, si = pltpu.roll(vr, dsh, 0), pltpu.roll(vi, dsh, 0)
                vr, vi = vr + (cr * sr - ci * si), vi + (cr * si + ci * sr)
            xin_ref[rows, re] = vr
            xin_ref[rows, im] = vi
            ar, ai = tab_ref[3, :, re], tab_ref[3, :, im]
            out += [ar * vr - ai * vi + ur, ar * vi + ai * vr + ui]
        return tuple(out)

    zero = (jnp.zeros((SUBLANES, LANES), F32),) * (2 * SSM_PAIRS_PER_TILE)

    lax.fori_loop(0, n_blocks, scan, zero)
    for q in range(SSM_PAIRS_PER_TILE):
        cols = slice(q * pw, (q + 1) * pw)
        y = jnp.dot(u_ref[:, cols], m_scr[q], preferred_element_type=F32)
        y_ref[:, cols] = y + jnp.dot(xin_ref[:, cols].astype(BF16), r_scr[q], preferred_element_type=F32)


def _ssm(u2, tables, layer, batch):
    rows = u2.shape[0] // batch
    w = SSM_TILE_W
    pw = 2 * LANES

    def tab_spec(a):
        nd = a.ndim - 2
        return pl.BlockSpec((None, None) + a.shape[2:], lambda j, b: (layer, j) + (0,) * nd)

    return pl.pallas_call(
        _ssm_kernel,
        grid=(SSM_TILES, batch),
        in_specs=[pl.BlockSpec((rows, w), lambda j, b: (b, j))] + [tab_spec(a) for a in tables],
        out_specs=pl.BlockSpec((rows, w), lambda j, b: (b, j)),
        out_shape=jax.ShapeDtypeStruct(u2.shape, F32),
        scratch_shapes=[pltpu.VMEM((SSM_PAIRS_PER_TILE, pw, pw), BF16)] * 3 + [pltpu.VMEM((rows, w), F32)] * 2,
        compiler_params=pltpu.CompilerParams(vmem_limit_bytes=VMEM_LIMIT),
        name="ssm",
    )(u2, *tables)


def _rel_bucket(dist):
    n = jnp.maximum(dist, 0)
    max_exact = NUM_BUCKETS // 2
    n_f = jnp.maximum(n, 1).astype(F32)
    large = max_exact + (jnp.log(n_f / max_exact) / math.log(REL_MAX_DISTANCE / max_exact)
                         * (NUM_BUCKETS - max_exact)).astype(jnp.int32)
    large = jnp.minimum(large, NUM_BUCKETS - 1)
    return jnp.where(n < max_exact, n, large)


def _bias_mask(rel_bias_g, window, dilation):
    span = window // dilation
    blk, period = ATTN_BLOCK, 4 * ATTN_BLOCK
    delta = jnp.arange(period)
    bias = rel_bias_g[_rel_bucket(delta * dilation)].astype(F32)
    by_delta = jnp.where((delta <= span)[:, None], bias, NEG_INF).T
    heads = by_delta.shape[0]
    skew = jnp.tile(by_delta, (1, blk))[:, :blk * (period - 1)].reshape(heads, blk, period - 1)
    return skew[:, ::-1, 2 * blk - 1::-1]


def _attn_kernel(q_ref, kc_ref, kp_ref, vc_ref, vp_ref, bm_ref, o_ref, lse_ref):
    blk = ATTN_BLOCK
    first_valid_col = jnp.where(pl.program_id(1) == 0, blk, 0)
    lane = lax.broadcasted_iota(jnp.int32, (blk, LANES), 1)
    low = lane < ATTN_HEAD_DIM
    col = lax.broadcasted_iota(jnp.int32, (blk, 2 * blk), 1)
    n_sub = q_ref.shape[1] // blk
    dn = (((1,), (1,)), ((), ()))
    for sq in range(q_ref.shape[0]):
        for n in range(n_sub):
            rows = slice(n * blk, (n + 1) * blk)
            lse_w = jnp.zeros((blk, LANES), F32)
            for pair in range(ATTN_HEADS_PER_GROUP // 2):
                cols = slice(pair * LANES, (pair + 1) * LANES)
                q32 = q_ref[sq, rows, cols].astype(F32)
                if n == 0:
                    kk = jnp.concatenate([kp_ref[sq, :, cols], kc_ref[sq, rows, cols]], axis=0)
                    vv = jnp.concatenate([vp_ref[sq, :, cols], vc_ref[sq, rows, cols]], axis=0)
                else:
                    kk = kc_ref[sq, (n - 1) * blk:(n + 1) * blk, cols]
                    vv = vc_ref[sq, (n - 1) * blk:(n + 1) * blk, cols]
                outs = []
                for sub in range(2):
                    hh = 2 * pair + sub
                    qh = jnp.where(low if sub == 0 else ~low, q32, 0.0).astype(BF16)
                    s = lax.dot_general(qh, kk, dn, preferred_element_type=F32) + bm_ref[hh]
                    if n == 0:
                        s = jnp.where(col >= first_valid_col, s, NEG_INF)
                    m = jnp.max(s, axis=-1, keepdims=True)
                    p = jnp.exp(s - m)
                    l = jnp.sum(p, axis=-1, keepdims=True)
                    o = jnp.dot(p.astype(BF16), vv, preferred_element_type=F32) * (1.0 / l)
                    outs.append(o)
                    lse_w = jnp.where(lane == hh, m + jnp.log(l), lse_w)
                o_ref[sq, rows, cols] = jnp.where(low, outs[0], outs[1]).astype(BF16)
            lse_ref[sq, rows, :] = lse_w


def _attention(q, k, v, bias_mask):
    ns, m_len, _ = q.shape
    tq = min(TQ_ATTN, m_len)
    per_step = TQ_ATTN // tq
    per = tq // ATTN_BLOCK
    cur = pl.BlockSpec((per_step, tq, D_GROUP), lambda s, i: (s, i, 0))
    prev = pl.BlockSpec((per_step, ATTN_BLOCK, D_GROUP), lambda s, i: (s, jnp.maximum(i * per - 1, 0), 0))
    return pl.pallas_call(
        _attn_kernel,
        grid=(ns // per_step, m_len // tq),
        in_specs=[cur, cur, prev, cur, prev,
                  _const_spec((ATTN_HEADS_PER_GROUP, ATTN_BLOCK, 2 * ATTN_BLOCK))],
        out_specs=[cur, pl.BlockSpec((per_step, tq, LANES), lambda s, i: (s, i, 0))],
        out_shape=[jax.ShapeDtypeStruct((ns, m_len, D_GROUP), BF16),
                   jax.ShapeDtypeStruct((ns, m_len, LANES), F32)],
        name="attn",
    )(q, k, k, v, v, bias_mask)


def _merge_kernel(final, x_ref, y2_ref, o0_ref, o1_ref, o2_ref, l0_ref, l1_ref, l2_ref, qm_ref, km_ref, vm_ref,
                  g_ref, wzs_ref, wza_ref, wzm_ref, wg0_ref, wg1_ref, wg2_ref, wg3_ref, wg4_ref, wg5_ref,
                  bg_ref, wglu_ref, bglu_ref, wbs_ref, wba_ref, wbm_ref, wout_ref, fg_ref, out_ref,
                  y_scr, o_scr, l_scr):
    i = pl.program_id(1)
    n_blocks = pl.num_programs(1) - 1
    tm = x_ref.shape[1]
    tiles = D_GROUP // LANES
    dilated = ((1, o1_ref, l1_ref), (2, o2_ref, l2_ref))

    def prepare(slot):
        for j in range(SSM_TILES):
            groups = [y2_ref[0, :, j * SSM_TILE_W + gl * LANES:j * SSM_TILE_W + (gl + 1) * LANES]
                      for gl in range(SSM_TILE_GROUPS)]
            for t, blk in enumerate(_block_transpose(groups)):
                y_scr[slot, j, pl.ds(t, tm // SSM_CHUNK, stride=SSM_CHUNK), :] = blk
        for gi, o_ref, l_ref in dilated:
            r = ATTN_CONFIGS[gi][1]
            for s in range(r):
                rows = pl.ds(s, tm // r, stride=r)
                for c in range(tiles):
                    o_scr[slot, (gi - 1) * tiles + c, rows, :] = o_ref[0, s, :, c * LANES:(c + 1) * LANES].astype(F32)
                l_scr[slot, gi - 1, rows, :] = l_ref[0, s]

    def compute(slot):
        x = x_ref[0]
        h = _rms(x, g_ref[...]).astype(BF16)

        def hdot(w_ref):
            return jnp.dot(h, w_ref[...], preferred_element_type=F32)

        yg = jax.nn.gelu(jnp.concatenate([y_scr[slot, j] for j in range(SSM_TILES)], axis=-1))
        t = jnp.dot(yg.astype(BF16), wglu_ref[...], preferred_element_type=F32) + bglu_ref[...]
        o_ssm = yg * jax.nn.sigmoid(t) * jax.nn.silu(hdot(wzs_ref))
        p_ssm = jnp.dot(o_ssm.astype(BF16), wbs_ref[...], preferred_element_type=F32)

        ls = (l0_ref[0, 0], l_scr[slot, 0], l_scr[slot, 1])
        os_ = [o0_ref[0, 0].astype(F32)]
        os_ += [jnp.concatenate([o_scr[slot, g * tiles + c] for c in range(tiles)], axis=-1) for g in range(2)]
        mx = jnp.maximum(jnp.maximum(ls[0], ls[1]), ls[2])
        es = [jnp.exp(l - mx) for l in ls]
        inv = 1.0 / (es[0] + es[1] + es[2])
        head_of_lane = lax.broadcasted_iota(jnp.int32, (tm, D_GROUP), 1) // ATTN_HEAD_DIM
        parts = []
        for e, o_g in zip(es, os_):
            alpha = e * inv
            wide = jnp.zeros((tm, D_GROUP), F32)
            for j in range(ATTN_HEADS_PER_GROUP):
                wide = jnp.where(head_of_lane == j, alpha[:, j:j + 1], wide)
            parts.append(o_g * wide)
        o_attn = jnp.concatenate(parts, axis=-1) * jax.nn.silu(hdot(wza_ref))
        p_attn = jnp.dot(o_attn.astype(BF16), wba_ref[...], preferred_element_type=F32)

        dn = (((1,), (1,)), ((), ()))
        heads = []
        for hd in range(MEM_HEADS):
            cols = slice(hd * MEM_HEAD_DIM, (hd + 1) * MEM_HEAD_DIM)
            s = lax.dot_general(qm_ref[0, :, cols], km_ref[0, :, cols], dn, preferred_element_type=F32)
            s = s * (MEM_HEAD_DIM ** -0.5)
            m = jnp.max(s, axis=-1, keepdims=True)
            p = jnp.exp(s - m)
            l = jnp.sum(p, axis=-1, keepdims=True)
            heads.append(jnp.dot(p.astype(BF16), vm_ref[0, :, cols], preferred_element_type=F32) * (1.0 / l))
        o_mem = jnp.concatenate(heads, axis=-1) * jax.nn.silu(hdot(wzm_ref))
        p_mem = jnp.dot(o_mem.astype(BF16), wbm_ref[...], preferred_element_type=F32)

        gate_refs = (wg0_ref, wg1_ref, wg2_ref, wg3_ref, wg4_ref, wg5_ref)
        per_branch = D_MODEL // GATE_W
        halves = []
        for part in range(per_branch):
            acc = jnp.zeros((tm, GATE_W), F32)
            for br, p_br in enumerate((p_ssm, p_attn, p_mem)):
                k = br * per_branch + part
                gate = jax.nn.sigmoid(hdot(gate_refs[k]) + bg_ref[:, k * GATE_W:(k + 1) * GATE_W])
                acc = acc + gate * p_br[:, part * GATE_W:(part + 1) * GATE_W]
            halves.append(acc)
        merged = jnp.concatenate(halves, axis=-1)
        xn = x + jnp.dot(merged.astype(BF16), wout_ref[...], preferred_element_type=F32)
        if final:
            xn = _rms(xn, fg_ref[...])
        out_ref[0] = xn

    slot = i % 2

    @pl.when(i == 0)
    def _():
        prepare(slot)

    @pl.when(jnp.logical_and(i > 0, i < n_blocks))
    def _():
        compute(1 - slot)
        prepare(slot)

    @pl.when(i == n_blocks)
    def _():
        compute(1 - slot)


def _merge(final, layer, x, y2, o_groups, lse_groups, qm, k_mem, v_mem, g, w_in_bf, bg, wglu, bglu, wbs, wba,
           wbm, wout, fg):
    B, L, _ = x.shape
    tm = TM_MERGE
    n_blocks = L // tm

    def cur(i):
        return jnp.maximum(i - 1, 0)

    def nxt(i):
        return jnp.minimum(i, n_blocks - 1)

    def rows(w):
        return pl.BlockSpec((1, tm, w), lambda b, i: (b, cur(i), 0))

    def dec(r, w, which):
        return pl.BlockSpec((1, r, tm // r, w), lambda b, i: (b, 0, which(i), 0))

    mem_spec = pl.BlockSpec((1,) + k_mem.shape[1:], lambda b, i: (b, 0, 0))
    rs = [r for _, r in ATTN_CONFIGS]
    assert rs[0] == 1
    n_gate = N_BRANCHES * D_MODEL // GATE_W
    in_specs = ([rows(D_MODEL),
                 pl.BlockSpec((1, tm // SSM_CHUNK, D_SSM * SSM_CHUNK), lambda b, i: (b, nxt(i), 0))]
                + [dec(r, D_GROUP, cur if r == 1 else nxt) for r in rs]
                + [dec(r, LANES, cur if r == 1 else nxt) for r in rs]
                + [rows(D_MEM), mem_spec, mem_spec, _const_spec(g.shape)]
                + [_w_in_spec(n, layer) for n in ("z_ssm", "z_attn", "z_mem")]
                + [_w_in_spec("gates", layer, part) for part in range(n_gate)]
                + [_const_spec(a.shape) for a in (bg, wglu, bglu, wbs, wba, wbm, wout, fg)])
    n_dil = len(rs) - 1
    return pl.pallas_call(
        functools.partial(_merge_kernel, final),
        grid=(B, n_blocks + 1),
        in_specs=in_specs,
        out_specs=rows(D_MODEL),
        out_shape=jax.ShapeDtypeStruct((B, L, D_MODEL), F32),
        scratch_shapes=[pltpu.VMEM((2, D_SSM // LANES, tm, LANES), F32),
                        pltpu.VMEM((2, n_dil * D_GROUP // LANES, tm, LANES), F32),
                        pltpu.VMEM((2, n_dil, tm, LANES), F32)],
        compiler_params=pltpu.CompilerParams(vmem_limit_bytes=VMEM_LIMIT),
        name="merge",
    )(x, y2, *o_groups, *lse_groups, qm, k_mem, v_mem, g, *([w_in_bf] * (3 + n_gate)), bg, wglu, bglu, wbs, wba,
      wbm, wout, fg)


def kernel(x, mem, norm_g, mem_norm_g, w_in, b_gate, ssm_lambda_re, ssm_lambda_im, ssm_log_dt, ssm_b_re,
           ssm_b_im, ssm_c_re, ssm_c_im, ssm_d, w_glu, b_glu, w_mem_kv, w_br_ssm, w_br_attn, w_br_mem,
           w_out, rel_bias, final_norm_g):
    B, L, _ = x.shape
    assert L % (ATTN_CONFIGS[-1][1] * ATTN_BLOCK) == 0 and L % TM_INPROJ == 0 and L % TM_MERGE == 0
    bias_masks = [_bias_mask(rel_bias[:, gi * ATTN_HEADS_PER_GROUP:(gi + 1) * ATTN_HEADS_PER_GROUP], win, dil)
                  for gi, (win, dil) in enumerate(ATTN_CONFIGS)]
    fg = final_norm_g.reshape(1, D_MODEL)
    w_in_bf = w_in.astype(BF16)
    def groups_of_all_layers(a):
        return a.reshape((DEPTH * SSM_GROUPS,) + a.shape[2:])

    ssm_tables = _ssm_prep(*(groups_of_all_layers(a) for a in (
        ssm_lambda_re, ssm_lambda_im, ssm_log_dt, ssm_b_re, ssm_b_im, ssm_c_re, ssm_c_im,
        ssm_d.reshape(DEPTH, SSM_GROUPS, SSM_GROUP))))
    ssm_tables = [t.reshape((DEPTH, SSM_TILES) + t.shape[1:]) for t in ssm_tables]
    n_chunks = B * L // SSM_CHUNK
    for layer in range(DEPTH):
        g = norm_g[layer].reshape(1, D_MODEL)
        k_mem, v_mem = _mem_kv(mem, mem_norm_g[layer].reshape(1, D_MODEL), w_mem_kv[layer].astype(BF16))
        u2, *qkv, qm = _in_proj(x, g, w_in_bf, layer)

        y = _ssm(u2.reshape(n_chunks, D_SSM * SSM_CHUNK), ssm_tables, layer, batch=B)
        y = y.reshape(B, L // SSM_CHUNK, D_SSM * SSM_CHUNK)

        o_groups, lse_groups = [], []
        for gi, (_, r) in enumerate(ATTN_CONFIGS):
            m_len = L // r
            q_g, k_g, v_g = (qkv[3 * idx + gi].reshape(B * r, m_len, D_GROUP) for idx in range(3))
            o_g, lse_g = _attention(q_g, k_g, v_g, bias_masks[gi])
            o_groups.append(o_g.reshape(B, r, m_len, D_GROUP))
            lse_groups.append(lse_g.reshape(B, r, m_len, LANES))

        x = _merge(layer == DEPTH - 1, layer, x, y, o_groups, lse_groups, qm, k_mem, v_mem, g, w_in_bf,
                   b_gate[layer].reshape(1, -1), w_glu[layer].astype(BF16), b_glu[layer].reshape(1, -1),
                   w_br_ssm[layer].astype(BF16), w_br_attn[layer].astype(BF16), w_br_mem[layer].astype(BF16),
                   w_out[layer].astype(BF16), fg)
    return x
```

```python
import functools
import math

import jax
import jax.numpy as jnp
import numpy as np
from jax import lax
from jax.experimental import pallas as pl
from jax.experimental.pallas import tpu as pltpu

F32 = jnp.float32
BF16 = jnp.bfloat16

D_MODEL = 1024
DEPTH = 2
EPS = 1e-6
N_BRANCHES = 3
D_SSM = 768
SSM_GROUP = 16
SSM_GROUPS = 48
SSM_STATE = 64
ATTN_HEAD_DIM = 64
ATTN_HEADS_PER_GROUP = 4
ATTN_CONFIGS = ((128, 1), (512, 4), (2048, 16))
N_ATTN_HEADS = 12
D_ATTN = 768
ATTN_BLOCK = 128
NUM_BUCKETS = 32
REL_MAX_DISTANCE = 2048
NEG_INF = -1e30
MEM_HEADS = 4
MEM_HEAD_DIM = 128
D_MEM = 512
D_GROUP = ATTN_HEADS_PER_GROUP * ATTN_HEAD_DIM

LANES = 128
SUBLANES = 8
SSM_CHUNK = SUBLANES
SSM_TILE_GROUPS = LANES // SSM_GROUP
SSM_TILES = D_SSM // LANES
SSM_TILE_W = SSM_CHUNK * LANES
SSM_PAIRS_PER_TILE = SSM_TILE_GROUPS // 2
VMEM_LIMIT = 56 * 1024 * 1024

TM_INPROJ = 1024
TM_MERGE = 512
TQ_ATTN = 2048


def _rms(x, g):
    return x * lax.rsqrt(jnp.mean(x * x, axis=-1, keepdims=True) + EPS) * g


def _block_transpose(vs):
    n = len(vs)
    width = LANES // n
    block = lax.broadcasted_iota(jnp.int32, vs[0].shape, 1) // width
    d = n // 2
    while d >= 1:
        bit_set = (block & d) != 0
        new = list(vs)
        for i in range(n):
            if i & d == 0:
                a, b = vs[i], vs[i + d]
                new[i] = jnp.where(bit_set, pltpu.roll(b, d * width, 1), a)
                new[i + d] = jnp.where(bit_set, b, pltpu.roll(a, LANES - d * width, 1))
        vs = new
        d //= 2
    return vs


def _const_spec(shape):
    n = len(shape)
    return pl.BlockSpec(shape, lambda *_: (0,) * n, pipeline_mode=pl.Buffered(1))


def _memkv_kernel(mem_ref, g_ref, w_ref, k_ref, v_ref):
    h = _rms(mem_ref[0], g_ref[...]).astype(BF16)
    k_ref[0] = jnp.dot(h, w_ref[:, :D_MEM], preferred_element_type=F32).astype(BF16)
    v_ref[0] = jnp.dot(h, w_ref[:, D_MEM:], preferred_element_type=F32).astype(BF16)


def _mem_kv(mem, g, w_bf16):
    B, ML, _ = mem.shape
    return pl.pallas_call(
        _memkv_kernel,
        grid=(B,),
        in_specs=[pl.BlockSpec((1, ML, D_MODEL), lambda b: (b, 0, 0)),
                  _const_spec((1, D_MODEL)),
                  _const_spec((D_MODEL, 2 * D_MEM))],
        out_specs=[pl.BlockSpec((1, ML, D_MEM), lambda b: (b, 0, 0)),
                   pl.BlockSpec((1, ML, D_MEM), lambda b: (b, 0, 0))],
        out_shape=[jax.ShapeDtypeStruct((B, ML, D_MEM), BF16)] * 2,
        name="mem_kv",
    )(mem, g, w_bf16)


_IN_SIZES = (D_SSM, D_SSM, D_ATTN, D_ATTN, D_ATTN, D_ATTN, D_MEM, D_MEM, N_BRANCHES * D_MODEL)
_IN_OFFS = tuple(int(v) for v in np.concatenate([[0], np.cumsum(_IN_SIZES)]))
_IN_NAMES = ("u", "z_ssm", "q", "k", "v", "z_attn", "q_mem", "z_mem", "gates")
GATE_W = 512


def _w_in_spec(name, layer, part=0):
    idx = _IN_NAMES.index(name)
    width = GATE_W if name == "gates" else _IN_SIZES[idx]
    block, rem = divmod(_IN_OFFS[idx], width)
    assert rem == 0
    return pl.BlockSpec((None, D_MODEL, width), lambda *_: (layer, 0, block + part),
                        pipeline_mode=pl.Buffered(1))


def _inproj_kernel(x_ref, g_ref, wu_ref, wq_ref, wk_ref, wv_ref, wqm_ref, u2_ref, *rest):
    qkv_refs, qm_ref, scr = rest[:3], rest[3], rest[4]
    h = _rms(x_ref[0], g_ref[...]).astype(BF16)
    tm = h.shape[0]

    def to_scratch(p):
        for j in range(p.shape[1] // LANES):
            scr[j] = p[:, j * LANES:(j + 1) * LANES]

    to_scratch(jnp.dot(h, wu_ref[...], preferred_element_type=F32))
    for j in range(SSM_TILES):
        steps = [scr[j, pl.ds(s, tm // SSM_CHUNK, stride=SSM_CHUNK), :] for s in range(SSM_CHUNK)]
        for gl, blk in enumerate(_block_transpose(steps)):
            lo = j * SSM_TILE_W + gl * LANES
            u2_ref[0, :, lo:lo + LANES] = blk.astype(BF16)
    for idx, (w_ref, scale) in enumerate(((wq_ref, ATTN_HEAD_DIM ** -0.5), (wk_ref, None), (wv_ref, None))):
        p = jnp.dot(h, w_ref[...], preferred_element_type=F32)
        if scale is not None:
            p = p * scale
        to_scratch(p)
        tiles = D_GROUP // LANES
        for gi, (_, r) in enumerate(ATTN_CONFIGS):
            o_ref = qkv_refs[gi]
            for s in range(r):
                for c in range(tiles):
                    piece = scr[gi * tiles + c, pl.ds(s, tm // r, stride=r), :]
                    lo = idx * D_GROUP + c * LANES
                    o_ref[0, s, :, lo:lo + LANES] = piece.astype(BF16)
    qm_ref[0] = jnp.dot(h, wqm_ref[...], preferred_element_type=F32).astype(BF16)


def _in_proj(x, g, w_in_bf, layer):
    B, L, _ = x.shape
    tm = TM_INPROJ
    out_specs = [pl.BlockSpec((1, tm // SSM_CHUNK, D_SSM * SSM_CHUNK), lambda b, i: (b, i, 0))]
    out_shape = [jax.ShapeDtypeStruct((B, L // SSM_CHUNK, D_SSM * SSM_CHUNK), BF16)]
    for _, r in ATTN_CONFIGS:
        out_specs.append(pl.BlockSpec((1, r, tm // r, 3 * D_GROUP), lambda b, i: (b, 0, i, 0)))
        out_shape.append(jax.ShapeDtypeStruct((B, r, L // r, 3 * D_GROUP), BF16))
    out_specs.append(pl.BlockSpec((1, tm, D_MEM), lambda b, i: (b, i, 0)))
    out_shape.append(jax.ShapeDtypeStruct((B, L, D_MEM), BF16))
    return pl.pallas_call(
        _inproj_kernel,
        grid=(B, L // tm),
        in_specs=[pl.BlockSpec((1, tm, D_MODEL), lambda b, i: (b, i, 0)),
                  _const_spec((1, D_MODEL))]
                 + [_w_in_spec(n, layer) for n in ("u", "q", "k", "v", "q_mem")],
        out_specs=out_specs,
        out_shape=out_shape,
        scratch_shapes=[pltpu.VMEM((D_SSM // LANES, tm, LANES), F32)],
        compiler_params=pltpu.CompilerParams(vmem_limit_bytes=VMEM_LIMIT),
        name="in_proj",
    )(x, g, *([w_in_bf] * 5))


def _cmul(ar, ai, br, bi):
    return ar * br - ai * bi, ar * bi + ai * br


def _ssm_prep(lre, lim, log_dt, b_re, b_im, c_re, c_im, d):
    P, H, C = SSM_STATE, SSM_GROUP, SSM_CHUNK
    G = lre.shape[0]
    NT = G // SSM_TILE_GROUPS
    dt = jnp.exp(log_dt)[:, None]
    mag = jnp.exp(lre * dt)
    ar, ai = mag * jnp.cos(lim * dt), mag * jnp.sin(lim * dt)
    den = lre * lre + lim * lim
    nr, ni = ar - 1.0, ai
    fr = (nr * lre + ni * lim) / den
    fi = (ni * lre - nr * lim) / den
    prs, pis = [jnp.ones_like(ar)], [jnp.zeros_like(ai)]
    for _ in range(C):
        r_, i_ = _cmul(prs[-1], pis[-1], ar, ai)
        prs.append(r_)
        pis.append(i_)

    def halves(left, right):
        return jnp.stack([jnp.concatenate([l, r], axis=-1) for l, r in zip(left, right)], axis=1)

    ws = [_cmul(prs[C - 1 - s], pis[C - 1 - s], fr, fi) for s in range(C)]
    s_tab = halves([w[0] for w in ws] + [-w[1] for w in ws], [w[0] for w in ws] + [w[1] for w in ws])
    r_tab = halves(prs[1:] + [-p for p in pis[1:]], [-p for p in prs[1:]] + [-p for p in pis[1:]])
    bt_re, bt_im = b_re.transpose(0, 2, 1), b_im.transpose(0, 2, 1)
    b_tab = jnp.concatenate([jnp.concatenate([bt_re, bt_im], axis=-1),
                             jnp.concatenate([bt_im, bt_re], axis=-1)], axis=1)
    d_rows = jnp.zeros((G, SUBLANES, LANES), F32).at[:, 0, :].set(jnp.tile(d, (1, C)))
    c_tab = jnp.concatenate([jnp.concatenate([c_re, c_im], axis=-1), jnp.concatenate([c_im, c_re], axis=-1),
                             jnp.concatenate([c_re, -c_im], axis=-1), d_rows], axis=1)
    PT = SSM_PAIRS_PER_TILE
    alr, ali = prs[C], pis[C]
    qrs, qis = [jnp.ones_like(alr)], [jnp.zeros_like(ali)]
    for _ in range(SUBLANES // 2):
        r_, i_ = _cmul(qrs[-1], qis[-1], alr, ali)
        qrs.append(r_)
        qis.append(i_)

    def lay(zr, zi):
        z = jnp.stack([zr.reshape(NT, PT, LANES), zi.reshape(NT, PT, LANES)], axis=2)
        return z.reshape(NT, SSM_TILE_W)

    rows = jnp.arange(SUBLANES)[:, None, None]
    tabs = []
    for dsh in (1, 2, 4):
        full = jnp.broadcast_to(lay(qrs[dsh], qis[dsh])[None], (SUBLANES, NT, SSM_TILE_W))
        tabs.append(jnp.where(rows >= dsh, full, 0.0))
    tabs.append(jnp.broadcast_to(lay(qrs[1], qis[1])[None], (SUBLANES, NT, SSM_TILE_W)))
    tab = jnp.stack(tabs).transpose(2, 0, 1, 3)

    def per_tile(a):
        return a.reshape((NT, SSM_TILE_GROUPS) + a.shape[1:])

    return per_tile(s_tab), per_tile(r_tab), per_tile(b_tab), per_tile(c_tab), tab.astype(F32)


def _ssm_expand(st_ref, rt_ref, bt_ref, ct_ref, m_scr, s_scr, r_scr):
    H, P, C = SSM_GROUP, SSM_STATE, SSM_CHUNK
    lane = lax.broadcasted_iota(jnp.int32, (LANES, LANES), 1)
    low = lane < P
    step_of_lane = lane // H
    diag = (lax.broadcasted_iota(jnp.int32, (H, LANES), 0) == lax.broadcasted_iota(jnp.int32, (H, LANES), 1) % H)
    nt_dims = (((1,), (1,)), ((), ()))
    m_scr[...] = jnp.zeros(m_scr.shape, m_scr.dtype)
    r_scr[...] = jnp.zeros(r_scr.shape, r_scr.dtype)
    for gl in range(SSM_TILE_GROUPS):
        q, gl2 = divmod(gl, 2)
        own = slice(gl2 * LANES, (gl2 + 1) * LANES)
        b_cat, b_swp = bt_ref[gl, 0:H], bt_ref[gl, H:2 * H]
        c_cat, c_swp, c_zero = ct_ref[gl, 0:H], ct_ref[gl, H:2 * H], ct_ref[gl, 2 * H:3 * H]
        d_lanes = ct_ref[gl, 3 * H:3 * H + 1]
        s_g = jnp.concatenate([st_ref[gl, s:s + 1] * b_cat + st_ref[gl, C + s:C + s + 1] * b_swp
                               for s in range(C)], axis=0)
        rt_g = jnp.concatenate([rt_ref[gl, t:t + 1] * c_cat + rt_ref[gl, C + t:C + t + 1] * c_swp
                                for t in range(C)], axis=0)
        lags = lax.dot_general(s_g, jnp.tile(c_zero, (C, 1)), nt_dims, precision=lax.Precision.HIGHEST,
                               preferred_element_type=F32)
        skip = jnp.where(diag, d_lanes, 0.0)
        lags = lags + jnp.concatenate([jnp.zeros((LANES - H, LANES), F32), skip], axis=0)
        m_g = jnp.zeros((LANES, LANES), F32)
        for t in range(C):
            up = (C - 1 - t) * H
            shifted = lags if up == 0 else jnp.concatenate([lags[up:], jnp.zeros((up, LANES), F32)], axis=0)
            m_g = jnp.where(step_of_lane == t, shifted, m_g)
        m_scr[q, own, own] = m_g.astype(BF16)
        swapped = pltpu.roll(s_g, P, 1)
        mine = low if gl2 == 0 else ~low
        s_scr[q, own, 0:LANES] = jnp.where(mine, s_g if gl2 == 0 else swapped, 0.0).astype(BF16)
        s_scr[q, own, LANES:2 * LANES] = jnp.where(mine, swapped if gl2 == 0 else s_g, 0.0).astype(BF16)
        r_g = rt_g.T
        for r in range(2):
            rows = slice(r * LANES + gl2 * P, r * LANES + (gl2 + 1) * P)
            r_scr[q, rows, own] = r_g[r * P:(r + 1) * P, :].astype(BF16)


def _ssm_kernel(u_ref, st_ref, rt_ref, bt_ref, ct_ref, tab_ref, y_ref, m_scr, s_scr, r_scr, upd_ref, xin_ref):
    @pl.when(pl.program_id(1) == 0)
    def _():
        _ssm_expand(st_ref, rt_ref, bt_ref, ct_ref, m_scr, s_scr, r_scr)

    pw = 2 * LANES
    for q in range(SSM_PAIRS_PER_TILE):
        cols = slice(q * pw, (q + 1) * pw)
        upd_ref[:, cols] = jnp.dot(u_ref[:, cols], s_scr[q], preferred_element_type=F32)
    n_blocks = u_ref.shape[0] // SUBLANES
    row = lax.broadcasted_iota(jnp.int32, (SUBLANES, LANES), 0)

    def scan(i, carry):
        r0 = pl.multiple_of(i * SUBLANES, SUBLANES)
        rows = pl.ds(r0, SUBLANES)
        out = []
        for q in range(SSM_PAIRS_PER_TILE):
            re, im = slice(q * pw, q * pw + LANES), slice(q * pw + LANES, (q + 1) * pw)
            ur, ui = upd_ref[rows, re], upd_ref[rows, im]
            vr = pltpu.roll(jnp.where(row == SUBLANES - 1, carry[2 * q], ur), 1, 0)
            vi = pltpu.roll(jnp.where(row == SUBLANES - 1, carry[2 * q + 1], ui), 1, 0)
            for lvl, dsh in enumerate((1, 2, 4)):
                cr, ci = tab_ref[lvl, :, re], tab_ref[lvl, :, im]
                sr, si = pltpu.roll(vr, dsh, 0), pltpu.roll(vi, dsh, 0)
                vr, vi = vr + (cr * sr - ci * si), vi + (cr * si + ci * sr)
            xin_ref[rows, re] = vr
            xin_ref[rows, im] = vi
            ar, ai = tab_ref[3, :, re], tab_ref[3, :, im]
            out += [ar * vr - ai * vi + ur, ar * vi + ai * vr + ui]
        return tuple(out)

    zero = (jnp.zeros((SUBLANES, LANES), F32),) * (2 * SSM_PAIRS_PER_TILE)

    lax.fori_loop(0, n_blocks, scan, zero, unroll=2)
    for q in range(SSM_PAIRS_PER_TILE):
        cols = slice(q * pw, (q + 1) * pw)
        y = jnp.dot(u_ref[:, cols], m_scr[q], preferred_element_type=F32)
        y_ref[:, cols] = y + jnp.dot(xin_ref[:, cols].astype(BF16), r_scr[q], preferred_element_type=F32)


def _ssm(u2, tables, layer, batch):
    rows = u2.shape[0] // batch
    w = SSM_TILE_W
    pw = 2 * LANES

    def tab_spec(a):
        nd = a.ndim - 2
        return pl.BlockSpec((None, None) + a.shape[2:], lambda j, b: (layer, j) + (0,) * nd)

    return pl.pallas_call(
        _ssm_kernel,
        grid=(SSM_TILES, batch),
        in_specs=[pl.BlockSpec((rows, w), lambda j, b: (b, j))] + [tab_spec(a) for a in tables],
        out_specs=pl.BlockSpec((rows, w), lambda j, b: (b, j)),
        out_shape=jax.ShapeDtypeStruct(u2.shape, F32),
        scratch_shapes=[pltpu.VMEM((SSM_PAIRS_PER_TILE, pw, pw), BF16)] * 3 + [pltpu.VMEM((rows, w), F32)] * 2,
        compiler_params=pltpu.CompilerParams(vmem_limit_bytes=VMEM_LIMIT),
        name="ssm",
    )(u2, *tables)


def _rel_bucket(dist):
    n = jnp.maximum(dist, 0)
    max_exact = NUM_BUCKETS // 2
    n_f = jnp.maximum(n, 1).astype(F32)
    large = max_exact + (jnp.log(n_f / max_exact) / math.log(REL_MAX_DISTANCE / max_exact)
                         * (NUM_BUCKETS - max_exact)).astype(jnp.int32)
    large = jnp.minimum(large, NUM_BUCKETS - 1)
    return jnp.where(n < max_exact, n, large)


def _bias_mask(rel_bias_g, window, dilation):
    span = window // dilation
    blk, period = ATTN_BLOCK, 4 * ATTN_BLOCK
    delta = jnp.arange(period)
    bias = rel_bias_g[_rel_bucket(delta * dilation)].astype(F32)
    by_delta = jnp.where((delta <= span)[:, None], bias, NEG_INF).T
    heads = by_delta.shape[0]
    skew = jnp.tile(by_delta, (1, blk))[:, :blk * (period - 1)].reshape(heads, blk, period - 1)
    return skew[:, ::-1, 2 * blk - 1::-1]


def _attn_kernel(q_ref, kc_ref, kp_ref, vc_ref, vp_ref, bm_ref, o_ref, lse_ref):
    blk = ATTN_BLOCK
    first_valid_col = jnp.where(pl.program_id(1) == 0, blk, 0)
    lane = lax.broadcasted_iota(jnp.int32, (blk, LANES), 1)
    low = lane < ATTN_HEAD_DIM
    col = lax.broadcasted_iota(jnp.int32, (blk, 2 * blk), 1)
    n_sub = q_ref.shape[1] // blk
    dn = (((1,), (1,)), ((), ()))
    for sq in range(q_ref.shape[0]):
        for n in range(n_sub):
            rows = slice(n * blk, (n + 1) * blk)
            lse_w = jnp.zeros((blk, LANES), F32)
            for pair in range(ATTN_HEADS_PER_GROUP // 2):
                cols = slice(pair * LANES, (pair + 1) * LANES)
                q32 = q_ref[sq, rows, cols].astype(F32)
                if n == 0:
                    kk = jnp.concatenate([kp_ref[sq, :, cols], kc_ref[sq, rows, cols]], axis=0)
                    vv = jnp.concatenate([vp_ref[sq, :, cols], vc_ref[sq, rows, cols]], axis=0)
                else:
                    kk = kc_ref[sq, (n - 1) * blk:(n + 1) * blk, cols]
                    vv = vc_ref[sq, (n - 1) * blk:(n + 1) * blk, cols]
                outs = []
                for sub in range(2):
                    hh = 2 * pair + sub
                    qh = jnp.where(low if sub == 0 else ~low, q32, 0.0).astype(BF16)
                    s = lax.dot_general(qh, kk, dn, preferred_element_type=F32) + bm_ref[hh]
                    if n == 0:
                        s = jnp.where(col >= first_valid_col, s, NEG_INF)
                    m = jnp.max(s, axis=-1, keepdims=True)
                    p = jnp.exp(s - m)
                    l = jnp.sum(p, axis=-1, keepdims=True)
                    o = jnp.dot(p.astype(BF16), vv, preferred_element_type=F32) * (1.0 / l)
                    outs.append(o)
                    lse_w = jnp.where(lane == hh, m + jnp.log(l), lse_w)
                o_ref[sq, rows, cols] = jnp.where(low, outs[0], outs[1]).astype(BF16)
            lse_ref[sq, rows, :] = lse_w


def _attention(qkv, bias_mask):
    ns, m_len, _ = qkv.shape
    tq = min(TQ_ATTN, m_len)
    per_step = TQ_ATTN // tq
    per = tq // ATTN_BLOCK

    def cur(part):
        return pl.BlockSpec((per_step, tq, D_GROUP), lambda s, i: (s, i, part))

    def prev(part):
        return pl.BlockSpec((per_step, ATTN_BLOCK, D_GROUP), lambda s, i: (s, jnp.maximum(i * per - 1, 0), part))

    return pl.pallas_call(
        _attn_kernel,
        grid=(ns // per_step, m_len // tq),
        in_specs=[cur(0), cur(1), prev(1), cur(2), prev(2),
                  _const_spec((ATTN_HEADS_PER_GROUP, ATTN_BLOCK, 2 * ATTN_BLOCK))],
        out_specs=[cur(0), pl.BlockSpec((per_step, tq, LANES), lambda s, i: (s, i, 0))],
        out_shape=[jax.ShapeDtypeStruct((ns, m_len, D_GROUP), BF16),
                   jax.ShapeDtypeStruct((ns, m_len, LANES), F32)],
        name="attn",
    )(qkv, qkv, qkv, qkv, qkv, bias_mask)


def _merge_kernel(final, x_ref, y2_ref, o0_ref, o1_ref, o2_ref, l0_ref, l1_ref, l2_ref, qm_ref, km_ref, vm_ref,
                  g_ref, wzs_ref, wza_ref, wzm_ref, wg0_ref, wg1_ref, wg2_ref, wg3_ref, wg4_ref, wg5_ref,
                  bg_ref, wglu_ref, bglu_ref, wbs_ref, wba_ref, wbm_ref, wout_ref, fg_ref, out_ref,
                  y_scr, o_scr, l_scr):
    i = pl.program_id(1)
    n_blocks = pl.num_programs(1) - 1
    tm = x_ref.shape[1]
    tiles = D_GROUP // LANES
    dilated = ((1, o1_ref, l1_ref), (2, o2_ref, l2_ref))

    def prepare(slot):
        for j in range(SSM_TILES):
            groups = [y2_ref[0, :, j * SSM_TILE_W + gl * LANES:j * SSM_TILE_W + (gl + 1) * LANES]
                      for gl in range(SSM_TILE_GROUPS)]
            for t, blk in enumerate(_block_transpose(groups)):
                y_scr[slot, j, pl.ds(t, tm // SSM_CHUNK, stride=SSM_CHUNK), :] = blk
        for gi, o_ref, l_ref in dilated:
            r = ATTN_CONFIGS[gi][1]
            for s in range(r):
                rows = pl.ds(s, tm // r, stride=r)
                for c in range(tiles):
                    o_scr[slot, (gi - 1) * tiles + c, rows, :] = o_ref[0, s, :, c * LANES:(c + 1) * LANES].astype(F32)
                l_scr[slot, gi - 1, rows, :] = l_ref[0, s]

    def compute(slot):
        x = x_ref[0]
        h = _rms(x, g_ref[...]).astype(BF16)

        def hdot(w_ref):
            return jnp.dot(h, w_ref[...], preferred_element_type=F32)

        yg = jax.nn.gelu(jnp.concatenate([y_scr[slot, j] for j in range(SSM_TILES)], axis=-1))
        t = jnp.dot(yg.astype(BF16), wglu_ref[...], preferred_element_type=F32) + bglu_ref[...]
        o_ssm = yg * jax.nn.sigmoid(t) * jax.nn.silu(hdot(wzs_ref))
        p_ssm = jnp.dot(o_ssm.astype(BF16), wbs_ref[...], preferred_element_type=F32)

        ls = (l0_ref[0, 0], l_scr[slot, 0], l_scr[slot, 1])
        os_ = [o0_ref[0, 0].astype(F32)]
        os_ += [jnp.concatenate([o_scr[slot, g * tiles + c] for c in range(tiles)], axis=-1) for g in range(2)]
        mx = jnp.maximum(jnp.maximum(ls[0], ls[1]), ls[2])
        es = [jnp.exp(l - mx) for l in ls]
        inv = 1.0 / (es[0] + es[1] + es[2])
        head_of_lane = lax.broadcasted_iota(jnp.int32, (tm, D_GROUP), 1) // ATTN_HEAD_DIM
        parts = []
        for e, o_g in zip(es, os_):
            alpha = e * inv
            wide = jnp.zeros((tm, D_GROUP), F32)
            for j in range(ATTN_HEADS_PER_GROUP):
                wide = jnp.where(head_of_lane == j, alpha[:, j:j + 1], wide)
            parts.append(o_g * wide)
        o_attn = jnp.concatenate(parts, axis=-1) * jax.nn.silu(hdot(wza_ref))
        p_attn = jnp.dot(o_attn.astype(BF16), wba_ref[...], preferred_element_type=F32)

        dn = (((1,), (1,)), ((), ()))
        heads = []
        for hd in range(MEM_HEADS):
            cols = slice(hd * MEM_HEAD_DIM, (hd + 1) * MEM_HEAD_DIM)
            s = lax.dot_general(qm_ref[0, :, cols], km_ref[0, :, cols], dn, preferred_element_type=F32)
            s = s * (MEM_HEAD_DIM ** -0.5)
            m = jnp.max(s, axis=-1, keepdims=True)
            p = jnp.exp(s - m)
            l = jnp.sum(p, axis=-1, keepdims=True)
            heads.append(jnp.dot(p.astype(BF16), vm_ref[0, :, cols], preferred_element_type=F32) * (1.0 / l))
        o_mem = jnp.concatenate(heads, axis=-1) * jax.nn.silu(hdot(wzm_ref))
        p_mem = jnp.dot(o_mem.astype(BF16), wbm_ref[...], preferred_element_type=F32)

        gate_refs = (wg0_ref, wg1_ref, wg2_ref, wg3_ref, wg4_ref, wg5_ref)
        per_branch = D_MODEL // GATE_W
        halves = []
        for part in range(per_branch):
            acc = jnp.zeros((tm, GATE_W), F32)
            for br, p_br in enumerate((p_ssm, p_attn, p_mem)):
                k = br * per_branch + part
                gate = jax.nn.sigmoid(hdot(gate_refs[k]) + bg_ref[:, k * GATE_W:(k + 1) * GATE_W])
                acc = acc + gate * p_br[:, part * GATE_W:(part + 1) * GATE_W]
            halves.append(acc)
        merged = jnp.concatenate(halves, axis=-1)
        xn = x + jnp.dot(merged.astype(BF16), wout_ref[...], preferred_element_type=F32)
        if final:
            xn = _rms(xn, fg_ref[...])
        out_ref[0] = xn

    slot = i % 2

    @pl.when(i == 0)
    def _():
        prepare(slot)

    @pl.when(jnp.logical_and(i > 0, i < n_blocks))
    def _():
        compute(1 - slot)
        prepare(slot)

    @pl.when(i == n_blocks)
    def _():
        compute(1 - slot)


def _merge(final, layer, x, y2, o_groups, lse_groups, qm, k_mem, v_mem, g, w_in_bf, bg, wglu, bglu, wbs, wba,
           wbm, wout, fg):
    B, L, _ = x.shape
    tm = TM_MERGE
    n_blocks = L // tm

    def cur(i):
        return jnp.maximum(i - 1, 0)

    def nxt(i):
        return jnp.minimum(i, n_blocks - 1)

    def rows(w):
        return pl.BlockSpec((1, tm, w), lambda b, i: (b, cur(i), 0))

    def dec(r, w, which):
        return pl.BlockSpec((1, r, tm // r, w), lambda b, i: (b, 0, which(i), 0))

    mem_spec = pl.BlockSpec((1,) + k_mem.shape[1:], lambda b, i: (b, 0, 0))
    rs = [r for _, r in ATTN_CONFIGS]
    assert rs[0] == 1
    n_gate = N_BRANCHES * D_MODEL // GATE_W
    in_specs = ([rows(D_MODEL),
                 pl.BlockSpec((1, tm // SSM_CHUNK, D_SSM * SSM_CHUNK), lambda b, i: (b, nxt(i), 0))]
                + [dec(r, D_GROUP, cur if r == 1 else nxt) for r in rs]
                + [dec(r, LANES, cur if r == 1 else nxt) for r in rs]
                + [rows(D_MEM), mem_spec, mem_spec, _const_spec(g.shape)]
                + [_w_in_spec(n, layer) for n in ("z_ssm", "z_attn", "z_mem")]
                + [_w_in_spec("gates", layer, part) for part in range(n_gate)]
                + [_const_spec(a.shape) for a in (bg, wglu, bglu, wbs, wba, wbm, wout, fg)])
    n_dil = len(rs) - 1
    return pl.pallas_call(
        functools.partial(_merge_kernel, final),
        grid=(B, n_blocks + 1),
        in_specs=in_specs,
        out_specs=rows(D_MODEL),
        out_shape=jax.ShapeDtypeStruct((B, L, D_MODEL), F32),
        scratch_shapes=[pltpu.VMEM((2, D_SSM // LANES, tm, LANES), F32),
                        pltpu.VMEM((2, n_dil * D_GROUP // LANES, tm, LANES), F32),
                        pltpu.VMEM((2, n_dil, tm, LANES), F32)],
        compiler_params=pltpu.CompilerParams(vmem_limit_bytes=VMEM_LIMIT),
        name="merge",
    )(x, y2, *o_groups, *lse_groups, qm, k_mem, v_mem, g, *([w_in_bf] * (3 + n_gate)), bg, wglu, bglu, wbs, wba,
      wbm, wout, fg)


def kernel(x, mem, norm_g, mem_norm_g, w_in, b_gate, ssm_lambda_re, ssm_lambda_im, ssm_log_dt, ssm_b_re,
           ssm_b_im, ssm_c_re, ssm_c_im, ssm_d, w_glu, b_glu, w_mem_kv, w_br_ssm, w_br_attn, w_br_mem,
           w_out, rel_bias, final_norm_g):
    B, L, _ = x.shape
    assert L % (ATTN_CONFIGS[-1][1] * ATTN_BLOCK) == 0 and L % TM_INPROJ == 0 and L % TM_MERGE == 0
    bias_masks = [_bias_mask(rel_bias[:, gi * ATTN_HEADS_PER_GROUP:(gi + 1) * ATTN_HEADS_PER_GROUP], win, dil)
                  for gi, (win, dil) in enumerate(ATTN_CONFIGS)]
    fg = final_norm_g.reshape(1, D_MODEL)
    w_in_bf = w_in.astype(BF16)
    def groups_of_all_layers(a):
        return a.reshape((DEPTH * SSM_GROUPS,) + a.shape[2:])

    ssm_tables = _ssm_prep(*(groups_of_all_layers(a) for a in (
        ssm_lambda_re, ssm_lambda_im, ssm_log_dt, ssm_b_re, ssm_b_im, ssm_c_re, ssm_c_im,
        ssm_d.reshape(DEPTH, SSM_GROUPS, SSM_GROUP))))
    ssm_tables = [t.reshape((DEPTH, SSM_TILES) + t.shape[1:]) for t in ssm_tables]
    n_chunks = B * L // SSM_CHUNK
    for layer in range(DEPTH):
        g = norm_g[layer].reshape(1, D_MODEL)
        k_mem, v_mem = _mem_kv(mem, mem_norm_g[layer].reshape(1, D_MODEL), w_mem_kv[layer].astype(BF16))
        u2, *qkv, qm = _in_proj(x, g, w_in_bf, layer)

        y = _ssm(u2.reshape(n_chunks, D_SSM * SSM_CHUNK), ssm_tables, layer, batch=B)
        y = y.reshape(B, L // SSM_CHUNK, D_SSM * SSM_CHUNK)

        o_groups, lse_groups = [], []
        for gi, (_, r) in enumerate(ATTN_CONFIGS):
            m_len = L // r
            o_g, lse_g = _attention(qkv[gi].reshape(B * r, m_len, 3 * D_GROUP), bias_masks[gi])
            o_groups.append(o_g.reshape(B, r, m_len, D_GROUP))
            lse_groups.append(lse_g.reshape(B, r, m_len, LANES))

        x = _merge(layer == DEPTH - 1, layer, x, y, o_groups, lse_groups, qm, k_mem, v_mem, g, w_in_bf,
                   b_gate[layer].reshape(1, -1), w_glu[layer].astype(BF16), b_glu[layer].reshape(1, -1),
                   w_br_ssm[layer].astype(BF16), w_br_attn[layer].astype(BF16), w_br_mem[layer].astype(BF16),
                   w_out[layer].astype(BF16), fg)
    return x
```

```python
import functools
import math

import jax
import jax.numpy as jnp
import numpy as np
from jax import lax
from jax.experimental import pallas as pl
from jax.experimental.pallas import tpu as pltpu

F32 = jnp.float32
BF16 = jnp.bfloat16

D_MODEL = 1024
DEPTH = 2
EPS = 1e-6
N_BRANCHES = 3
D_SSM = 768
SSM_GROUP = 16
SSM_GROUPS = 48
SSM_STATE = 64
ATTN_HEAD_DIM = 64
ATTN_HEADS_PER_GROUP = 4
ATTN_CONFIGS = ((128, 1), (512, 4), (2048, 16))
N_ATTN_HEADS = 12
D_ATTN = 768
ATTN_BLOCK = 128
NUM_BUCKETS = 32
REL_MAX_DISTANCE = 2048
NEG_INF = -1e30
MEM_HEADS = 4
MEM_HEAD_DIM = 128
D_MEM = 512
D_GROUP = ATTN_HEADS_PER_GROUP * ATTN_HEAD_DIM

LANES = 128
SUBLANES = 8
SSM_CHUNK = SUBLANES
SSM_TILE_GROUPS = LANES // SSM_GROUP
SSM_TILES = D_SSM // LANES
SSM_TILE_W = SSM_CHUNK * LANES
SSM_PAIRS_PER_TILE = SSM_TILE_GROUPS // 2
VMEM_LIMIT = 56 * 1024 * 1024

SSM_SCAN_UNROLL = 8
TM_INPROJ = 1024
TM_MERGE = 512
TQ_ATTN = 2048


def _rms(x, g):
    return x * lax.rsqrt(jnp.mean(x * x, axis=-1, keepdims=True) + EPS) * g


def _block_transpose(vs):
    n = len(vs)
    width = LANES // n
    block = lax.broadcasted_iota(jnp.int32, vs[0].shape, 1) // width
    d = n // 2
    while d >= 1:
        bit_set = (block & d) != 0
        new = list(vs)
        for i in range(n):
            if i & d == 0:
                a, b = vs[i], vs[i + d]
                new[i] = jnp.where(bit_set, pltpu.roll(b, d * width, 1), a)
                new[i + d] = jnp.where(bit_set, b, pltpu.roll(a, LANES - d * width, 1))
        vs = new
        d //= 2
    return vs


def _layer_spec(a, layer):
    nd = a.ndim - 1
    return pl.BlockSpec((None,) + a.shape[1:], lambda *_: (layer,) + (0,) * nd, pipeline_mode=pl.Buffered(1))


def _const_spec(shape):
    n = len(shape)
    return pl.BlockSpec(shape, lambda *_: (0,) * n, pipeline_mode=pl.Buffered(1))


def _memkv_kernel(mem_ref, g_ref, w_ref, k_ref, v_ref):
    h = _rms(mem_ref[0], g_ref[...]).astype(BF16)
    k_ref[0] = jnp.dot(h, w_ref[:, :D_MEM], preferred_element_type=F32).astype(BF16)
    v_ref[0] = jnp.dot(h, w_ref[:, D_MEM:], preferred_element_type=F32).astype(BF16)


def _mem_kv(mem, g, w_bf16):
    B, ML, _ = mem.shape
    depth = g.shape[0]
    kv_spec = pl.BlockSpec((1, ML, D_MEM), lambda l, b: (l * B + b, 0, 0))
    return pl.pallas_call(
        _memkv_kernel,
        grid=(depth, B),
        in_specs=[pl.BlockSpec((1, ML, D_MODEL), lambda l, b: (b, 0, 0)),
                  pl.BlockSpec((None, 1, D_MODEL), lambda l, b: (l, 0, 0)),
                  pl.BlockSpec((None, D_MODEL, 2 * D_MEM), lambda l, b: (l, 0, 0))],
        out_specs=[kv_spec, kv_spec],
        out_shape=[jax.ShapeDtypeStruct((depth * B, ML, D_MEM), BF16)] * 2,
        name="mem_kv",
    )(mem, g, w_bf16)


_IN_SIZES = (D_SSM, D_SSM, D_ATTN, D_ATTN, D_ATTN, D_ATTN, D_MEM, D_MEM, N_BRANCHES * D_MODEL)
_IN_OFFS = tuple(int(v) for v in np.concatenate([[0], np.cumsum(_IN_SIZES)]))
_IN_NAMES = ("u", "z_ssm", "q", "k", "v", "z_attn", "q_mem", "z_mem", "gates")
GATE_W = 512


def _w_in_spec(name, layer, part=0):
    idx = _IN_NAMES.index(name)
    width = GATE_W if name == "gates" else _IN_SIZES[idx]
    block, rem = divmod(_IN_OFFS[idx], width)
    assert rem == 0
    return pl.BlockSpec((None, D_MODEL, width), lambda *_: (layer, 0, block + part),
                        pipeline_mode=pl.Buffered(1))


def _inproj_kernel(x_ref, g_ref, wu_ref, wq_ref, wk_ref, wv_ref, wqm_ref, u2_ref, *rest):
    qkv_refs, qm_ref, scr = rest[:3], rest[3], rest[4]
    h = _rms(x_ref[0], g_ref[...]).astype(BF16)
    tm = h.shape[0]

    def to_scratch(p):
        for j in range(p.shape[1] // LANES):
            scr[j] = p[:, j * LANES:(j + 1) * LANES]

    to_scratch(jnp.dot(h, wu_ref[...], preferred_element_type=F32))
    for j in range(SSM_TILES):
        steps = [scr[j, pl.ds(s, tm // SSM_CHUNK, stride=SSM_CHUNK), :] for s in range(SSM_CHUNK)]
        for gl, blk in enumerate(_block_transpose(steps)):
            lo = j * SSM_TILE_W + gl * LANES
            u2_ref[0, :, lo:lo + LANES] = blk.astype(BF16)
    for idx, (w_ref, scale) in enumerate(((wq_ref, ATTN_HEAD_DIM ** -0.5), (wk_ref, None), (wv_ref, None))):
        p = jnp.dot(h, w_ref[...], preferred_element_type=F32)
        if scale is not None:
            p = p * scale
        to_scratch(p)
        tiles = D_GROUP // LANES
        for gi, (_, r) in enumerate(ATTN_CONFIGS):
            o_ref = qkv_refs[gi]
            for s in range(r):
                for c in range(tiles):
                    piece = scr[gi * tiles + c, pl.ds(s, tm // r, stride=r), :]
                    lo = idx * D_GROUP + c * LANES
                    o_ref[0, s, :, lo:lo + LANES] = piece.astype(BF16)
    qm_ref[0] = jnp.dot(h, wqm_ref[...], preferred_element_type=F32).astype(BF16)


def _in_proj(x, g, w_in_bf, layer):
    B, L, _ = x.shape
    tm = TM_INPROJ
    out_specs = [pl.BlockSpec((1, tm // SSM_CHUNK, D_SSM * SSM_CHUNK), lambda b, i: (b, i, 0))]
    out_shape = [jax.ShapeDtypeStruct((B, L // SSM_CHUNK, D_SSM * SSM_CHUNK), BF16)]
    for _, r in ATTN_CONFIGS:
        out_specs.append(pl.BlockSpec((1, r, tm // r, 3 * D_GROUP), lambda b, i: (b, 0, i, 0)))
        out_shape.append(jax.ShapeDtypeStruct((B, r, L // r, 3 * D_GROUP), BF16))
    out_specs.append(pl.BlockSpec((1, tm, D_MEM), lambda b, i: (b, i, 0)))
    out_shape.append(jax.ShapeDtypeStruct((B, L, D_MEM), BF16))
    return pl.pallas_call(
        _inproj_kernel,
        grid=(B, L // tm),
        in_specs=[pl.BlockSpec((1, tm, D_MODEL), lambda b, i: (b, i, 0)),
                  _const_spec((1, D_MODEL))]
                 + [_w_in_spec(n, layer) for n in ("u", "q", "k", "v", "q_mem")],
        out_specs=out_specs,
        out_shape=out_shape,
        scratch_shapes=[pltpu.VMEM((D_SSM // LANES, tm, LANES), F32)],
        compiler_params=pltpu.CompilerParams(vmem_limit_bytes=VMEM_LIMIT),
        name="in_proj",
    )(x, g, *([w_in_bf] * 5))


def _cmul(ar, ai, br, bi):
    return ar * br - ai * bi, ar * bi + ai * br


def _ssm_prep(lre, lim, log_dt, b_re, b_im, c_re, c_im, d):
    P, H, C = SSM_STATE, SSM_GROUP, SSM_CHUNK
    G = lre.shape[0]
    NT = G // SSM_TILE_GROUPS
    dt = jnp.exp(log_dt)[:, None]
    mag = jnp.exp(lre * dt)
    ar, ai = mag * jnp.cos(lim * dt), mag * jnp.sin(lim * dt)
    den = lre * lre + lim * lim
    nr, ni = ar - 1.0, ai
    fr = (nr * lre + ni * lim) / den
    fi = (ni * lre - nr * lim) / den
    prs, pis = [jnp.ones_like(ar)], [jnp.zeros_like(ai)]
    for _ in range(C):
        r_, i_ = _cmul(prs[-1], pis[-1], ar, ai)
        prs.append(r_)
        pis.append(i_)

    def halves(left, right):
        return jnp.stack([jnp.concatenate([l, r], axis=-1) for l, r in zip(left, right)], axis=1)

    ws = [_cmul(prs[C - 1 - s], pis[C - 1 - s], fr, fi) for s in range(C)]
    s_tab = halves([w[0] for w in ws] + [-w[1] for w in ws], [w[0] for w in ws] + [w[1] for w in ws])
    r_tab = halves(prs[1:] + [-p for p in pis[1:]], [-p for p in prs[1:]] + [-p for p in pis[1:]])
    bt_re, bt_im = b_re.transpose(0, 2, 1), b_im.transpose(0, 2, 1)
    b_tab = jnp.concatenate([jnp.concatenate([bt_re, bt_im], axis=-1),
                             jnp.concatenate([bt_im, bt_re], axis=-1)], axis=1)
    d_rows = jnp.zeros((G, SUBLANES, LANES), F32).at[:, 0, :].set(jnp.tile(d, (1, C)))
    c_tab = jnp.concatenate([jnp.concatenate([c_re, c_im], axis=-1), jnp.concatenate([c_im, c_re], axis=-1),
                             jnp.concatenate([c_re, -c_im], axis=-1), d_rows], axis=1)
    PT = SSM_PAIRS_PER_TILE
    alr, ali = prs[C], pis[C]
    qrs, qis = [jnp.ones_like(alr)], [jnp.zeros_like(ali)]
    for _ in range(SUBLANES // 2):
        r_, i_ = _cmul(qrs[-1], qis[-1], alr, ali)
        qrs.append(r_)
        qis.append(i_)

    def lay(zr, zi):
        z = jnp.stack([zr.reshape(NT, PT, LANES), zi.reshape(NT, PT, LANES)], axis=2)
        return z.reshape(NT, SSM_TILE_W)

    rows = jnp.arange(SUBLANES)[:, None, None]
    tabs = []
    for dsh in (1, 2, 4):
        full = jnp.broadcast_to(lay(qrs[dsh], qis[dsh])[None], (SUBLANES, NT, SSM_TILE_W))
        tabs.append(jnp.where(rows >= dsh, full, 0.0))
    tabs.append(jnp.broadcast_to(lay(qrs[1], qis[1])[None], (SUBLANES, NT, SSM_TILE_W)))
    tab = jnp.stack(tabs).transpose(2, 0, 1, 3)

    def per_tile(a):
        return a.reshape((NT, SSM_TILE_GROUPS) + a.shape[1:])

    return per_tile(s_tab), per_tile(r_tab), per_tile(b_tab), per_tile(c_tab), tab.astype(F32)


def _ssm_expand(st_ref, rt_ref, bt_ref, ct_ref, m_scr, s_scr, r_scr):
    H, P, C = SSM_GROUP, SSM_STATE, SSM_CHUNK
    lane = lax.broadcasted_iota(jnp.int32, (LANES, LANES), 1)
    low = lane < P
    step_of_lane = lane // H
    diag = (lax.broadcasted_iota(jnp.int32, (H, LANES), 0) == lax.broadcasted_iota(jnp.int32, (H, LANES), 1) % H)
    nt_dims = (((1,), (1,)), ((), ()))
    m_scr[...] = jnp.zeros(m_scr.shape, m_scr.dtype)
    r_scr[...] = jnp.zeros(r_scr.shape, r_scr.dtype)
    for gl in range(SSM_TILE_GROUPS):
        q, gl2 = divmod(gl, 2)
        own = slice(gl2 * LANES, (gl2 + 1) * LANES)
        b_cat, b_swp = bt_ref[gl, 0:H], bt_ref[gl, H:2 * H]
        c_cat, c_swp, c_zero = ct_ref[gl, 0:H], ct_ref[gl, H:2 * H], ct_ref[gl, 2 * H:3 * H]
        d_lanes = ct_ref[gl, 3 * H:3 * H + 1]
        s_g = jnp.concatenate([st_ref[gl, s:s + 1] * b_cat + st_ref[gl, C + s:C + s + 1] * b_swp
                               for s in range(C)], axis=0)
        rt_g = jnp.concatenate([rt_ref[gl, t:t + 1] * c_cat + rt_ref[gl, C + t:C + t + 1] * c_swp
                                for t in range(C)], axis=0)
        lags = lax.dot_general(s_g, jnp.tile(c_zero, (C, 1)), nt_dims, precision=lax.Precision.HIGHEST,
                               preferred_element_type=F32)
        skip = jnp.where(diag, d_lanes, 0.0)
        lags = lags + jnp.concatenate([jnp.zeros((LANES - H, LANES), F32), skip], axis=0)
        m_g = jnp.zeros((LANES, LANES), F32)
        for t in range(C):
            up = (C - 1 - t) * H
            shifted = lags if up == 0 else jnp.concatenate([lags[up:], jnp.zeros((up, LANES), F32)], axis=0)
            m_g = jnp.where(step_of_lane == t, shifted, m_g)
        m_scr[q, own, own] = m_g.astype(BF16)
        swapped = pltpu.roll(s_g, P, 1)
        mine = low if gl2 == 0 else ~low
        s_scr[q, own, 0:LANES] = jnp.where(mine, s_g if gl2 == 0 else swapped, 0.0).astype(BF16)
        s_scr[q, own, LANES:2 * LANES] = jnp.where(mine, swapped if gl2 == 0 else s_g, 0.0).astype(BF16)
        r_g = rt_g.T
        for r in range(2):
            rows = slice(r * LANES + gl2 * P, r * LANES + (gl2 + 1) * P)
            r_scr[q, rows, own] = r_g[r * P:(r + 1) * P, :].astype(BF16)


def _ssm_kernel(u_ref, st_ref, rt_ref, bt_ref, ct_ref, tab_ref, y_ref, m_scr, s_scr, r_scr, upd_ref, xin_ref):
    @pl.when(pl.program_id(1) == 0)
    def _():
        _ssm_expand(st_ref, rt_ref, bt_ref, ct_ref, m_scr, s_scr, r_scr)

    pw = 2 * LANES
    for q in range(SSM_PAIRS_PER_TILE):
        cols = slice(q * pw, (q + 1) * pw)
        upd_ref[:, cols] = jnp.dot(u_ref[:, cols], s_scr[q], preferred_element_type=F32)
    n_blocks = u_ref.shape[0] // SUBLANES
    row = lax.broadcasted_iota(jnp.int32, (SUBLANES, LANES), 0)

    def scan(i, carry):
        r0 = pl.multiple_of(i * SUBLANES, SUBLANES)
        rows = pl.ds(r0, SUBLANES)
        out = []
        for q in range(SSM_PAIRS_PER_TILE):
            re, im = slice(q * pw, q * pw + LANES), slice(q * pw + LANES, (q + 1) * pw)
            ur, ui = upd_ref[rows, re], upd_ref[rows, im]
            vr = pltpu.roll(jnp.where(row == SUBLANES - 1, carry[2 * q], ur), 1, 0)
            vi = pltpu.roll(jnp.where(row == SUBLANES - 1, carry[2 * q + 1], ui), 1, 0)
            for lvl, dsh in enumerate((1, 2, 4)):
                cr, ci = tab_ref[lvl, :, re], tab_ref[lvl, :, im]
                sr, si = pltpu.roll(vr, dsh, 0), pltpu.roll(vi, dsh, 0)
                vr, vi = vr + (cr * sr - ci * si), vi + (cr * si + ci * sr)
            xin_ref[rows, re] = vr
            xin_ref[rows, im] = vi
            ar, ai = tab_ref[3, :, re], tab_ref[3, :, im]
            out += [ar * vr - ai * vi + ur, ar * vi + ai * vr + ui]
        return tuple(out)

    zero = (jnp.zeros((SUBLANES, LANES), F32),) * (2 * SSM_PAIRS_PER_TILE)

    lax.fori_loop(0, n_blocks, scan, zero, unroll=SSM_SCAN_UNROLL)
    for q in range(SSM_PAIRS_PER_TILE):
        cols = slice(q * pw, (q + 1) * pw)
        y = jnp.dot(u_ref[:, cols], m_scr[q], preferred_element_type=F32)
        y_ref[:, cols] = y + jnp.dot(xin_ref[:, cols].astype(BF16), r_scr[q], preferred_element_type=F32)


def _ssm(u2, tables, layer, batch):
    rows = u2.shape[0] // batch
    w = SSM_TILE_W
    pw = 2 * LANES

    def tab_spec(a):
        nd = a.ndim - 2
        return pl.BlockSpec((None, None) + a.shape[2:], lambda j, b: (layer, j) + (0,) * nd)

    return pl.pallas_call(
        _ssm_kernel,
        grid=(SSM_TILES, batch),
        in_specs=[pl.BlockSpec((rows, w), lambda j, b: (b, j))] + [tab_spec(a) for a in tables],
        out_specs=pl.BlockSpec((rows, w), lambda j, b: (b, j)),
        out_shape=jax.ShapeDtypeStruct(u2.shape, F32),
        scratch_shapes=[pltpu.VMEM((SSM_PAIRS_PER_TILE, pw, pw), BF16)] * 3 + [pltpu.VMEM((rows, w), F32)] * 2,
        compiler_params=pltpu.CompilerParams(vmem_limit_bytes=VMEM_LIMIT),
        name="ssm",
    )(u2, *tables)


def _rel_bucket(dist):
    n = jnp.maximum(dist, 0)
    max_exact = NUM_BUCKETS // 2
    n_f = jnp.maximum(n, 1).astype(F32)
    large = max_exact + (jnp.log(n_f / max_exact) / math.log(REL_MAX_DISTANCE / max_exact)
                         * (NUM_BUCKETS - max_exact)).astype(jnp.int32)
    large = jnp.minimum(large, NUM_BUCKETS - 1)
    return jnp.where(n < max_exact, n, large)


def _bias_mask(rel_bias_g, window, dilation):
    span = window // dilation
    blk, period = ATTN_BLOCK, 4 * ATTN_BLOCK
    delta = jnp.arange(period)
    bias = rel_bias_g[_rel_bucket(delta * dilation)].astype(F32)
    by_delta = jnp.where((delta <= span)[:, None], bias, NEG_INF).T
    heads = by_delta.shape[0]
    skew = jnp.tile(by_delta, (1, blk))[:, :blk * (period - 1)].reshape(heads, blk, period - 1)
    return skew[:, ::-1, 2 * blk - 1::-1]


def _attn_kernel(q_ref, kc_ref, kp_ref, vc_ref, vp_ref, bm_ref, o_ref, lse_ref):
    blk = ATTN_BLOCK
    first_valid_col = jnp.where(pl.program_id(1) == 0, blk, 0)
    lane = lax.broadcasted_iota(jnp.int32, (blk, LANES), 1)
    low = lane < ATTN_HEAD_DIM
    col = lax.broadcasted_iota(jnp.int32, (blk, 2 * blk), 1)
    n_sub = q_ref.shape[1] // blk
    dn = (((1,), (1,)), ((), ()))
    for sq in range(q_ref.shape[0]):
        for n in range(n_sub):
            rows = slice(n * blk, (n + 1) * blk)
            lse_w = jnp.zeros((blk, LANES), F32)
            for pair in range(ATTN_HEADS_PER_GROUP // 2):
                cols = slice(pair * LANES, (pair + 1) * LANES)
                q32 = q_ref[sq, rows, cols].astype(F32)
                if n == 0:
                    kk = jnp.concatenate([kp_ref[sq, :, cols], kc_ref[sq, rows, cols]], axis=0)
                    vv = jnp.concatenate([vp_ref[sq, :, cols], vc_ref[sq, rows, cols]], axis=0)
                else:
                    kk = kc_ref[sq, (n - 1) * blk:(n + 1) * blk, cols]
                    vv = vc_ref[sq, (n - 1) * blk:(n + 1) * blk, cols]
                outs = []
                for sub in range(2):
                    hh = 2 * pair + sub
                    qh = jnp.where(low if sub == 0 else ~low, q32, 0.0).astype(BF16)
                    s = lax.dot_general(qh, kk, dn, preferred_element_type=F32) + bm_ref[hh]
                    if n == 0:
                        s = jnp.where(col >= first_valid_col, s, NEG_INF)
                    m = jnp.max(s, axis=-1, keepdims=True)
                    p = jnp.exp(s - m)
                    l = jnp.sum(p, axis=-1, keepdims=True)
                    o = jnp.dot(p.astype(BF16), vv, preferred_element_type=F32) * (1.0 / l)
                    outs.append(o)
                    lse_w = jnp.where(lane == hh, m + jnp.log(l), lse_w)
                o_ref[sq, rows, cols] = jnp.where(low, outs[0], outs[1]).astype(BF16)
            lse_ref[sq, rows, :] = lse_w


def _attention(qkv, bias_mask):
    ns, m_len, _ = qkv.shape
    tq = min(TQ_ATTN, m_len)
    per_step = TQ_ATTN // tq
    per = tq // ATTN_BLOCK

    def cur(part):
        return pl.BlockSpec((per_step, tq, D_GROUP), lambda s, i: (s, i, part))

    def prev(part):
        return pl.BlockSpec((per_step, ATTN_BLOCK, D_GROUP), lambda s, i: (s, jnp.maximum(i * per - 1, 0), part))

    return pl.pallas_call(
        _attn_kernel,
        grid=(ns // per_step, m_len // tq),
        in_specs=[cur(0), cur(1), prev(1), cur(2), prev(2),
                  _const_spec((ATTN_HEADS_PER_GROUP, ATTN_BLOCK, 2 * ATTN_BLOCK))],
        out_specs=[cur(0), pl.BlockSpec((per_step, tq, LANES), lambda s, i: (s, i, 0))],
        out_shape=[jax.ShapeDtypeStruct((ns, m_len, D_GROUP), BF16),
                   jax.ShapeDtypeStruct((ns, m_len, LANES), F32)],
        name="attn",
    )(qkv, qkv, qkv, qkv, qkv, bias_mask)


def _merge_kernel(final, x_ref, y2_ref, o0_ref, o1_ref, o2_ref, l0_ref, l1_ref, l2_ref, qm_ref, km_ref, vm_ref,
                  g_ref, wzs_ref, wza_ref, wzm_ref, wg0_ref, wg1_ref, wg2_ref, wg3_ref, wg4_ref, wg5_ref,
                  bg_ref, wglu_ref, bglu_ref, wbs_ref, wba_ref, wbm_ref, wout_ref, fg_ref, out_ref,
                  y_scr, o_scr, l_scr):
    i = pl.program_id(1)
    n_blocks = pl.num_programs(1) - 1
    tm = x_ref.shape[1]
    tiles = D_GROUP // LANES
    dilated = ((1, o1_ref, l1_ref), (2, o2_ref, l2_ref))

    def prepare(slot):
        for j in range(SSM_TILES):
            groups = [y2_ref[0, :, j * SSM_TILE_W + gl * LANES:j * SSM_TILE_W + (gl + 1) * LANES]
                      for gl in range(SSM_TILE_GROUPS)]
            for t, blk in enumerate(_block_transpose(groups)):
                y_scr[slot, j, pl.ds(t, tm // SSM_CHUNK, stride=SSM_CHUNK), :] = blk
        for gi, o_ref, l_ref in dilated:
            r = ATTN_CONFIGS[gi][1]
            for s in range(r):
                rows = pl.ds(s, tm // r, stride=r)
                for c in range(tiles):
                    o_scr[slot, (gi - 1) * tiles + c, rows, :] = o_ref[0, s, :, c * LANES:(c + 1) * LANES].astype(F32)
                l_scr[slot, gi - 1, rows, :] = l_ref[0, s]

    def compute(slot):
        x = x_ref[0]
        h = _rms(x, g_ref[...]).astype(BF16)

        def hdot(w_ref):
            return jnp.dot(h, w_ref[...], preferred_element_type=F32)

        yg = jax.nn.gelu(jnp.concatenate([y_scr[slot, j] for j in range(SSM_TILES)], axis=-1))
        t = jnp.dot(yg.astype(BF16), wglu_ref[...], preferred_element_type=F32) + bglu_ref[...]
        o_ssm = yg * jax.nn.sigmoid(t) * jax.nn.silu(hdot(wzs_ref))
        p_ssm = jnp.dot(o_ssm.astype(BF16), wbs_ref[...], preferred_element_type=F32)

        ls = (l0_ref[0, 0], l_scr[slot, 0], l_scr[slot, 1])
        os_ = [o0_ref[0, 0].astype(F32)]
        os_ += [jnp.concatenate([o_scr[slot, g * tiles + c] for c in range(tiles)], axis=-1) for g in range(2)]
        mx = jnp.maximum(jnp.maximum(ls[0], ls[1]), ls[2])
        es = [jnp.exp(l - mx) for l in ls]
        inv = 1.0 / (es[0] + es[1] + es[2])
        head_of_lane = lax.broadcasted_iota(jnp.int32, (tm, D_GROUP), 1) // ATTN_HEAD_DIM
        parts = []
        for e, o_g in zip(es, os_):
            alpha = e * inv
            wide = jnp.zeros((tm, D_GROUP), F32)
            for j in range(ATTN_HEADS_PER_GROUP):
                wide = jnp.where(head_of_lane == j, alpha[:, j:j + 1], wide)
            parts.append(o_g * wide)
        o_attn = jnp.concatenate(parts, axis=-1) * jax.nn.silu(hdot(wza_ref))
        p_attn = jnp.dot(o_attn.astype(BF16), wba_ref[...], preferred_element_type=F32)

        dn = (((1,), (1,)), ((), ()))
        heads = []
        for hd in range(MEM_HEADS):
            cols = slice(hd * MEM_HEAD_DIM, (hd + 1) * MEM_HEAD_DIM)
            s = lax.dot_general(qm_ref[0, :, cols], km_ref[0, :, cols], dn, preferred_element_type=F32)
            s = s * (MEM_HEAD_DIM ** -0.5)
            m = jnp.max(s, axis=-1, keepdims=True)
            p = jnp.exp(s - m)
            l = jnp.sum(p, axis=-1, keepdims=True)
            heads.append(jnp.dot(p.astype(BF16), vm_ref[0, :, cols], preferred_element_type=F32) * (1.0 / l))
        o_mem = jnp.concatenate(heads, axis=-1) * jax.nn.silu(hdot(wzm_ref))
        p_mem = jnp.dot(o_mem.astype(BF16), wbm_ref[...], preferred_element_type=F32)

        gate_refs = (wg0_ref, wg1_ref, wg2_ref, wg3_ref, wg4_ref, wg5_ref)
        per_branch = D_MODEL // GATE_W
        halves = []
        for part in range(per_branch):
            acc = jnp.zeros((tm, GATE_W), F32)
            for br, p_br in enumerate((p_ssm, p_attn, p_mem)):
                k = br * per_branch + part
                gate = jax.nn.sigmoid(hdot(gate_refs[k]) + bg_ref[:, k * GATE_W:(k + 1) * GATE_W])
                acc = acc + gate * p_br[:, part * GATE_W:(part + 1) * GATE_W]
            halves.append(acc)
        merged = jnp.concatenate(halves, axis=-1)
        xn = x + jnp.dot(merged.astype(BF16), wout_ref[...], preferred_element_type=F32)
        if final:
            xn = _rms(xn, fg_ref[...])
        out_ref[0] = xn

    slot = i % 2

    @pl.when(i == 0)
    def _():
        prepare(slot)

    @pl.when(jnp.logical_and(i > 0, i < n_blocks))
    def _():
        compute(1 - slot)
        prepare(slot)

    @pl.when(i == n_blocks)
    def _():
        compute(1 - slot)


def _merge(final, layer, x, y2, o_groups, lse_groups, qm, k_mem, v_mem, g, w_in_bf, bg, wglu, bglu, wbs, wba,
           wbm, wout, fg):
    B, L, _ = x.shape
    tm = TM_MERGE
    n_blocks = L // tm

    def cur(i):
        return jnp.maximum(i - 1, 0)

    def nxt(i):
        return jnp.minimum(i, n_blocks - 1)

    def rows(w):
        return pl.BlockSpec((1, tm, w), lambda b, i: (b, cur(i), 0))

    def dec(r, w, which):
        return pl.BlockSpec((1, r, tm // r, w), lambda b, i: (b, 0, which(i), 0))

    mem_spec = pl.BlockSpec((1,) + k_mem.shape[1:], lambda b, i: (layer * B + b, 0, 0))
    rs = [r for _, r in ATTN_CONFIGS]
    assert rs[0] == 1
    n_gate = N_BRANCHES * D_MODEL // GATE_W
    in_specs = ([rows(D_MODEL),
                 pl.BlockSpec((1, tm // SSM_CHUNK, D_SSM * SSM_CHUNK), lambda b, i: (b, nxt(i), 0))]
                + [dec(r, D_GROUP, cur if r == 1 else nxt) for r in rs]
                + [dec(r, LANES, cur if r == 1 else nxt) for r in rs]
                + [rows(D_MEM), mem_spec, mem_spec, _const_spec(g.shape)]
                + [_w_in_spec(n, layer) for n in ("z_ssm", "z_attn", "z_mem")]
                + [_w_in_spec("gates", layer, part) for part in range(n_gate)]
                + [_layer_spec(a, layer) for a in (bg, wglu, bglu, wbs, wba, wbm, wout)] + [_const_spec(fg.shape)])
    n_dil = len(rs) - 1
    return pl.pallas_call(
        functools.partial(_merge_kernel, final),
        grid=(B, n_blocks + 1),
        in_specs=in_specs,
        out_specs=rows(D_MODEL),
        out_shape=jax.ShapeDtypeStruct((B, L, D_MODEL), F32),
        scratch_shapes=[pltpu.VMEM((2, D_SSM // LANES, tm, LANES), F32),
                        pltpu.VMEM((2, n_dil * D_GROUP // LANES, tm, LANES), F32),
                        pltpu.VMEM((2, n_dil, tm, LANES), F32)],
        compiler_params=pltpu.CompilerParams(vmem_limit_bytes=VMEM_LIMIT),
        name="merge",
    )(x, y2, *o_groups, *lse_groups, qm, k_mem, v_mem, g, *([w_in_bf] * (3 + n_gate)), bg, wglu, bglu, wbs, wba,
      wbm, wout, fg)


def kernel(x, mem, norm_g, mem_norm_g, w_in, b_gate, ssm_lambda_re, ssm_lambda_im, ssm_log_dt, ssm_b_re,
           ssm_b_im, ssm_c_re, ssm_c_im, ssm_d, w_glu, b_glu, w_mem_kv, w_br_ssm, w_br_attn, w_br_mem,
           w_out, rel_bias, final_norm_g):
    B, L, _ = x.shape
    assert L % (ATTN_CONFIGS[-1][1] * ATTN_BLOCK) == 0 and L % TM_INPROJ == 0 and L % TM_MERGE == 0
    bias_masks = [_bias_mask(rel_bias[:, gi * ATTN_HEADS_PER_GROUP:(gi + 1) * ATTN_HEADS_PER_GROUP], win, dil)
                  for gi, (win, dil) in enumerate(ATTN_CONFIGS)]
    fg = final_norm_g.reshape(1, D_MODEL)
    w_in_bf = w_in.astype(BF16)
    k_mem, v_mem = _mem_kv(mem, mem_norm_g.reshape(DEPTH, 1, D_MODEL), w_mem_kv.astype(BF16))
    layer_params = (b_gate.reshape(DEPTH, 1, -1), w_glu.astype(BF16), b_glu.reshape(DEPTH, 1, -1),
                    w_br_ssm.astype(BF16), w_br_attn.astype(BF16), w_br_mem.astype(BF16), w_out.astype(BF16))
    def groups_of_all_layers(a):
        return a.reshape((DEPTH * SSM_GROUPS,) + a.shape[2:])

    ssm_tables = _ssm_prep(*(groups_of_all_layers(a) for a in (
        ssm_lambda_re, ssm_lambda_im, ssm_log_dt, ssm_b_re, ssm_b_im, ssm_c_re, ssm_c_im,
        ssm_d.reshape(DEPTH, SSM_GROUPS, SSM_GROUP))))
    ssm_tables = [t.reshape((DEPTH, SSM_TILES) + t.shape[1:]) for t in ssm_tables]
    n_chunks = B * L // SSM_CHUNK
    for layer in range(DEPTH):
        g = norm_g[layer].reshape(1, D_MODEL)
        u2, *qkv, qm = _in_proj(x, g, w_in_bf, layer)

        y = _ssm(u2.reshape(n_chunks, D_SSM * SSM_CHUNK), ssm_tables, layer, batch=B)
        y = y.reshape(B, L // SSM_CHUNK, D_SSM * SSM_CHUNK)

        o_groups, lse_groups = [], []
        for gi, (_, r) in enumerate(ATTN_CONFIGS):
            m_len = L // r
            o_g, lse_g = _attention(qkv[gi].reshape(B * r, m_len, 3 * D_GROUP), bias_masks[gi])
            o_groups.append(o_g.reshape(B, r, m_len, D_GROUP))
            lse_groups.append(lse_g.reshape(B, r, m_len, LANES))

        x = _merge(layer == DEPTH - 1, layer, x, y, o_groups, lse_groups, qm, k_mem, v_mem, g, w_in_bf,
                   *layer_params, fg)
    return x
```

```python
import functools
import math

import jax
import jax.numpy as jnp
import numpy as np
from jax import lax
from jax.experimental import pallas as pl
from jax.experimental.pallas import tpu as pltpu

F32 = jnp.float32
BF16 = jnp.bfloat16

D_MODEL = 1024
DEPTH = 2
EPS = 1e-6
N_BRANCHES = 3
D_SSM = 768
SSM_GROUP = 16
SSM_GROUPS = 48
SSM_STATE = 64
ATTN_HEAD_DIM = 64
ATTN_HEADS_PER_GROUP = 4
ATTN_CONFIGS = ((128, 1), (512, 4), (2048, 16))
N_ATTN_HEADS = 12
D_ATTN = 768
ATTN_BLOCK = 128
NUM_BUCKETS = 32
REL_MAX_DISTANCE = 2048
NEG_INF = -1e30
MEM_HEADS = 4
MEM_HEAD_DIM = 128
D_MEM = 512
D_GROUP = ATTN_HEADS_PER_GROUP * ATTN_HEAD_DIM

LANES = 128
SUBLANES = 8
SSM_CHUNK = SUBLANES
SSM_TILE_GROUPS = LANES // SSM_GROUP
SSM_TILES = D_SSM // LANES
SSM_TILE_W = SSM_CHUNK * LANES
SSM_PAIRS_PER_TILE = SSM_TILE_GROUPS // 2
VMEM_LIMIT = 56 * 1024 * 1024

SSM_SEQS_PER_STEP = 2
TM_INPROJ = 1024
TM_MERGE = 512
TQ_ATTN = 2048


def _rms(x, g):
    return x * lax.rsqrt(jnp.mean(x * x, axis=-1, keepdims=True) + EPS) * g


def _block_transpose(vs):
    n = len(vs)
    width = LANES // n
    block = lax.broadcasted_iota(jnp.int32, vs[0].shape, 1) // width
    d = n // 2
    while d >= 1:
        bit_set = (block & d) != 0
        new = list(vs)
        for i in range(n):
            if i & d == 0:
                a, b = vs[i], vs[i + d]
                new[i] = jnp.where(bit_set, pltpu.roll(b, d * width, 1), a)
                new[i + d] = jnp.where(bit_set, b, pltpu.roll(a, LANES - d * width, 1))
        vs = new
        d //= 2
    return vs


def _const_spec(shape):
    n = len(shape)
    return pl.BlockSpec(shape, lambda *_: (0,) * n, pipeline_mode=pl.Buffered(1))


def _memkv_kernel(mem_ref, g_ref, w_ref, k_ref, v_ref):
    h = _rms(mem_ref[0], g_ref[...]).astype(BF16)
    k_ref[0] = jnp.dot(h, w_ref[:, :D_MEM], preferred_element_type=F32).astype(BF16)
    v_ref[0] = jnp.dot(h, w_ref[:, D_MEM:], preferred_element_type=F32).astype(BF16)


def _mem_kv(mem, g, w_bf16):
    B, ML, _ = mem.shape
    return pl.pallas_call(
        _memkv_kernel,
        grid=(B,),
        in_specs=[pl.BlockSpec((1, ML, D_MODEL), lambda b: (b, 0, 0)),
                  _const_spec((1, D_MODEL)),
                  _const_spec((D_MODEL, 2 * D_MEM))],
        out_specs=[pl.BlockSpec((1, ML, D_MEM), lambda b: (b, 0, 0)),
                   pl.BlockSpec((1, ML, D_MEM), lambda b: (b, 0, 0))],
        out_shape=[jax.ShapeDtypeStruct((B, ML, D_MEM), BF16)] * 2,
        name="mem_kv",
    )(mem, g, w_bf16)


_IN_SIZES = (D_SSM, D_SSM, D_ATTN, D_ATTN, D_ATTN, D_ATTN, D_MEM, D_MEM, N_BRANCHES * D_MODEL)
_IN_OFFS = tuple(int(v) for v in np.concatenate([[0], np.cumsum(_IN_SIZES)]))
_IN_NAMES = ("u", "z_ssm", "q", "k", "v", "z_attn", "q_mem", "z_mem", "gates")
GATE_W = 512


def _w_in_spec(name, layer, part=0):
    idx = _IN_NAMES.index(name)
    width = GATE_W if name == "gates" else _IN_SIZES[idx]
    block, rem = divmod(_IN_OFFS[idx], width)
    assert rem == 0
    return pl.BlockSpec((None, D_MODEL, width), lambda *_: (layer, 0, block + part),
                        pipeline_mode=pl.Buffered(1))


def _inproj_kernel(x_ref, g_ref, wu_ref, wq_ref, wk_ref, wv_ref, wqm_ref, u2_ref, *rest):
    qkv_refs, qm_ref, scr = rest[:3], rest[3], rest[4]
    h = _rms(x_ref[0], g_ref[...]).astype(BF16)
    tm = h.shape[0]

    def to_scratch(p):
        for j in range(p.shape[1] // LANES):
            scr[j] = p[:, j * LANES:(j + 1) * LANES]

    to_scratch(jnp.dot(h, wu_ref[...], preferred_element_type=F32))
    for j in range(SSM_TILES):
        steps = [scr[j, pl.ds(s, tm // SSM_CHUNK, stride=SSM_CHUNK), :] for s in range(SSM_CHUNK)]
        for gl, blk in enumerate(_block_transpose(steps)):
            lo = j * SSM_TILE_W + gl * LANES
            u2_ref[0, :, lo:lo + LANES] = blk.astype(BF16)
    for idx, (w_ref, scale) in enumerate(((wq_ref, ATTN_HEAD_DIM ** -0.5), (wk_ref, None), (wv_ref, None))):
        p = jnp.dot(h, w_ref[...], preferred_element_type=F32)
        if scale is not None:
            p = p * scale
        to_scratch(p)
        tiles = D_GROUP // LANES
        for gi, (_, r) in enumerate(ATTN_CONFIGS):
            o_ref = qkv_refs[gi]
            for s in range(r):
                for c in range(tiles):
                    piece = scr[gi * tiles + c, pl.ds(s, tm // r, stride=r), :]
                    lo = idx * D_GROUP + c * LANES
                    o_ref[0, s, :, lo:lo + LANES] = piece.astype(BF16)
    qm_ref[0] = jnp.dot(h, wqm_ref[...], preferred_element_type=F32).astype(BF16)


def _in_proj(x, g, w_in_bf, layer):
    B, L, _ = x.shape
    tm = TM_INPROJ
    out_specs = [pl.BlockSpec((1, tm // SSM_CHUNK, D_SSM * SSM_CHUNK), lambda b, i: (b, i, 0))]
    out_shape = [jax.ShapeDtypeStruct((B, L // SSM_CHUNK, D_SSM * SSM_CHUNK), BF16)]
    for _, r in ATTN_CONFIGS:
        out_specs.append(pl.BlockSpec((1, r, tm // r, 3 * D_GROUP), lambda b, i: (b, 0, i, 0)))
        out_shape.append(jax.ShapeDtypeStruct((B, r, L // r, 3 * D_GROUP), BF16))
    out_specs.append(pl.BlockSpec((1, tm, D_MEM), lambda b, i: (b, i, 0)))
    out_shape.append(jax.ShapeDtypeStruct((B, L, D_MEM), BF16))
    return pl.pallas_call(
        _inproj_kernel,
        grid=(B, L // tm),
        in_specs=[pl.BlockSpec((1, tm, D_MODEL), lambda b, i: (b, i, 0)),
                  _const_spec((1, D_MODEL))]
                 + [_w_in_spec(n, layer) for n in ("u", "q", "k", "v", "q_mem")],
        out_specs=out_specs,
        out_shape=out_shape,
        scratch_shapes=[pltpu.VMEM((D_SSM // LANES, tm, LANES), F32)],
        compiler_params=pltpu.CompilerParams(vmem_limit_bytes=VMEM_LIMIT),
        name="in_proj",
    )(x, g, *([w_in_bf] * 5))


def _cmul(ar, ai, br, bi):
    return ar * br - ai * bi, ar * bi + ai * br


def _ssm_prep(lre, lim, log_dt, b_re, b_im, c_re, c_im, d):
    P, H, C = SSM_STATE, SSM_GROUP, SSM_CHUNK
    G = lre.shape[0]
    NT = G // SSM_TILE_GROUPS
    dt = jnp.exp(log_dt)[:, None]
    mag = jnp.exp(lre * dt)
    ar, ai = mag * jnp.cos(lim * dt), mag * jnp.sin(lim * dt)
    den = lre * lre + lim * lim
    nr, ni = ar - 1.0, ai
    fr = (nr * lre + ni * lim) / den
    fi = (ni * lre - nr * lim) / den
    prs, pis = [jnp.ones_like(ar)], [jnp.zeros_like(ai)]
    for _ in range(C):
        r_, i_ = _cmul(prs[-1], pis[-1], ar, ai)
        prs.append(r_)
        pis.append(i_)

    def halves(left, right):
        return jnp.stack([jnp.concatenate([l, r], axis=-1) for l, r in zip(left, right)], axis=1)

    ws = [_cmul(prs[C - 1 - s], pis[C - 1 - s], fr, fi) for s in range(C)]
    s_tab = halves([w[0] for w in ws] + [-w[1] for w in ws], [w[0] for w in ws] + [w[1] for w in ws])
    r_tab = halves(prs[1:] + [-p for p in pis[1:]], [-p for p in prs[1:]] + [-p for p in pis[1:]])
    bt_re, bt_im = b_re.transpose(0, 2, 1), b_im.transpose(0, 2, 1)
    b_tab = jnp.concatenate([jnp.concatenate([bt_re, bt_im], axis=-1),
                             jnp.concatenate([bt_im, bt_re], axis=-1)], axis=1)
    d_rows = jnp.zeros((G, SUBLANES, LANES), F32).at[:, 0, :].set(jnp.tile(d, (1, C)))
    c_tab = jnp.concatenate([jnp.concatenate([c_re, c_im], axis=-1), jnp.concatenate([c_im, c_re], axis=-1),
                             jnp.concatenate([c_re, -c_im], axis=-1), d_rows], axis=1)
    PT = SSM_PAIRS_PER_TILE
    alr, ali = prs[C], pis[C]
    qrs, qis = [jnp.ones_like(alr)], [jnp.zeros_like(ali)]
    for _ in range(SUBLANES // 2):
        r_, i_ = _cmul(qrs[-1], qis[-1], alr, ali)
        qrs.append(r_)
        qis.append(i_)

    def lay(zr, zi):
        z = jnp.stack([zr.reshape(NT, PT, LANES), zi.reshape(NT, PT, LANES)], axis=2)
        return z.reshape(NT, SSM_TILE_W)

    rows = jnp.arange(SUBLANES)[:, None, None]
    tabs = []
    for dsh in (1, 2, 4):
        full = jnp.broadcast_to(lay(qrs[dsh], qis[dsh])[None], (SUBLANES, NT, SSM_TILE_W))
        tabs.append(jnp.where(rows >= dsh, full, 0.0))
    tabs.append(jnp.broadcast_to(lay(qrs[1], qis[1])[None], (SUBLANES, NT, SSM_TILE_W)))
    tab = jnp.stack(tabs).transpose(2, 0, 1, 3)

    def per_tile(a):
        return a.reshape((NT, SSM_TILE_GROUPS) + a.shape[1:])

    return per_tile(s_tab), per_tile(r_tab), per_tile(b_tab), per_tile(c_tab), tab.astype(F32)


def _ssm_expand(st_ref, rt_ref, bt_ref, ct_ref, m_scr, s_scr, r_scr):
    H, P, C = SSM_GROUP, SSM_STATE, SSM_CHUNK
    lane = lax.broadcasted_iota(jnp.int32, (LANES, LANES), 1)
    low = lane < P
    step_of_lane = lane // H
    diag = (lax.broadcasted_iota(jnp.int32, (H, LANES), 0) == lax.broadcasted_iota(jnp.int32, (H, LANES), 1) % H)
    nt_dims = (((1,), (1,)), ((), ()))
    m_scr[...] = jnp.zeros(m_scr.shape, m_scr.dtype)
    r_scr[...] = jnp.zeros(r_scr.shape, r_scr.dtype)
    for gl in range(SSM_TILE_GROUPS):
        q, gl2 = divmod(gl, 2)
        own = slice(gl2 * LANES, (gl2 + 1) * LANES)
        b_cat, b_swp = bt_ref[gl, 0:H], bt_ref[gl, H:2 * H]
        c_cat, c_swp, c_zero = ct_ref[gl, 0:H], ct_ref[gl, H:2 * H], ct_ref[gl, 2 * H:3 * H]
        d_lanes = ct_ref[gl, 3 * H:3 * H + 1]
        s_g = jnp.concatenate([st_ref[gl, s:s + 1] * b_cat + st_ref[gl, C + s:C + s + 1] * b_swp
                               for s in range(C)], axis=0)
        rt_g = jnp.concatenate([rt_ref[gl, t:t + 1] * c_cat + rt_ref[gl, C + t:C + t + 1] * c_swp
                                for t in range(C)], axis=0)
        lags = lax.dot_general(s_g, jnp.tile(c_zero, (C, 1)), nt_dims, precision=lax.Precision.HIGHEST,
                               preferred_element_type=F32)
        skip = jnp.where(diag, d_lanes, 0.0)
        lags = lags + jnp.concatenate([jnp.zeros((LANES - H, LANES), F32), skip], axis=0)
        m_g = jnp.zeros((LANES, LANES), F32)
        for t in range(C):
            up = (C - 1 - t) * H
            shifted = lags if up == 0 else jnp.concatenate([lags[up:], jnp.zeros((up, LANES), F32)], axis=0)
            m_g = jnp.where(step_of_lane == t, shifted, m_g)
        m_scr[q, own, own] = m_g.astype(BF16)
        swapped = pltpu.roll(s_g, P, 1)
        mine = low if gl2 == 0 else ~low
        s_scr[q, own, 0:LANES] = jnp.where(mine, s_g if gl2 == 0 else swapped, 0.0).astype(BF16)
        s_scr[q, own, LANES:2 * LANES] = jnp.where(mine, swapped if gl2 == 0 else s_g, 0.0).astype(BF16)
        r_g = rt_g.T
        for r in range(2):
            rows = slice(r * LANES + gl2 * P, r * LANES + (gl2 + 1) * P)
            r_scr[q, rows, own] = r_g[r * P:(r + 1) * P, :].astype(BF16)


def _ssm_kernel(u_ref, st_ref, rt_ref, bt_ref, ct_ref, tab_ref, y_ref, m_scr, s_scr, r_scr, upd_ref, xin_ref):
    @pl.when(pl.program_id(1) == 0)
    def _():
        _ssm_expand(st_ref, rt_ref, bt_ref, ct_ref, m_scr, s_scr, r_scr)

    pw = 2 * LANES
    for q in range(SSM_PAIRS_PER_TILE):
        cols = slice(q * pw, (q + 1) * pw)
        upd_ref[:, cols] = jnp.dot(u_ref[:, cols], s_scr[q], preferred_element_type=F32)
    seq_rows = u_ref.shape[0] // SSM_SEQS_PER_STEP
    n_blocks = seq_rows // SUBLANES
    row = lax.broadcasted_iota(jnp.int32, (SUBLANES, LANES), 0)

    def scan(i, carry):
        out = []
        for sq in range(SSM_SEQS_PER_STEP):
            rows = pl.ds(pl.multiple_of(sq * seq_rows + i * SUBLANES, SUBLANES), SUBLANES)
            for q in range(SSM_PAIRS_PER_TILE):
                re, im = slice(q * pw, q * pw + LANES), slice(q * pw + LANES, (q + 1) * pw)
                c = 2 * (sq * SSM_PAIRS_PER_TILE + q)
                ur, ui = upd_ref[rows, re], upd_ref[rows, im]
                vr = pltpu.roll(jnp.where(row == SUBLANES - 1, carry[c], ur), 1, 0)
                vi = pltpu.roll(jnp.where(row == SUBLANES - 1, carry[c + 1], ui), 1, 0)
                for lvl, dsh in enumerate((1, 2, 4)):
                    cr, ci = tab_ref[lvl, :, re], tab_ref[lvl, :, im]
                    sr, si = pltpu.roll(vr, dsh, 0), pltpu.roll(vi, dsh, 0)
                    vr, vi = vr + (cr * sr - ci * si), vi + (cr * si + ci * sr)
                xin_ref[rows, re] = vr
                xin_ref[rows, im] = vi
                ar, ai = tab_ref[3, :, re], tab_ref[3, :, im]
                out += [ar * vr - ai * vi + ur, ar * vi + ai * vr + ui]
        return tuple(out)

    zero = (jnp.zeros((SUBLANES, LANES), F32),) * (2 * SSM_PAIRS_PER_TILE * SSM_SEQS_PER_STEP)

    lax.fori_loop(0, n_blocks, scan, zero, unroll=2)
    for q in range(SSM_PAIRS_PER_TILE):
        cols = slice(q * pw, (q + 1) * pw)
        y = jnp.dot(u_ref[:, cols], m_scr[q], preferred_element_type=F32)
        y_ref[:, cols] = y + jnp.dot(xin_ref[:, cols].astype(BF16), r_scr[q], preferred_element_type=F32)


def _ssm(u2, tables, layer, batch):
    assert batch % SSM_SEQS_PER_STEP == 0
    rows = u2.shape[0] // batch * SSM_SEQS_PER_STEP
    w = SSM_TILE_W
    pw = 2 * LANES

    def tab_spec(a):
        nd = a.ndim - 2
        return pl.BlockSpec((None, None) + a.shape[2:], lambda j, b: (layer, j) + (0,) * nd)

    return pl.pallas_call(
        _ssm_kernel,
        grid=(SSM_TILES, batch // SSM_SEQS_PER_STEP),
        in_specs=[pl.BlockSpec((rows, w), lambda j, b: (b, j))] + [tab_spec(a) for a in tables],
        out_specs=pl.BlockSpec((rows, w), lambda j, b: (b, j)),
        out_shape=jax.ShapeDtypeStruct(u2.shape, F32),
        scratch_shapes=[pltpu.VMEM((SSM_PAIRS_PER_TILE, pw, pw), BF16)] * 3 + [pltpu.VMEM((rows, w), F32)] * 2,
        compiler_params=pltpu.CompilerParams(vmem_limit_bytes=VMEM_LIMIT),
        name="ssm",
    )(u2, *tables)


def _rel_bucket(dist):
    n = jnp.maximum(dist, 0)
    max_exact = NUM_BUCKETS // 2
    n_f = jnp.maximum(n, 1).astype(F32)
    large = max_exact + (jnp.log(n_f / max_exact) / math.log(REL_MAX_DISTANCE / max_exact)
                         * (NUM_BUCKETS - max_exact)).astype(jnp.int32)
    large = jnp.minimum(large, NUM_BUCKETS - 1)
    return jnp.where(n < max_exact, n, large)


def _bias_mask(rel_bias_g, window, dilation):
    span = window // dilation
    blk, period = ATTN_BLOCK, 4 * ATTN_BLOCK
    delta = jnp.arange(period)
    bias = rel_bias_g[_rel_bucket(delta * dilation)].astype(F32)
    by_delta = jnp.where((delta <= span)[:, None], bias, NEG_INF).T
    heads = by_delta.shape[0]
    skew = jnp.tile(by_delta, (1, blk))[:, :blk * (period - 1)].reshape(heads, blk, period - 1)
    return skew[:, ::-1, 2 * blk - 1::-1]


def _attn_kernel(q_ref, kc_ref, kp_ref, vc_ref, vp_ref, bm_ref, o_ref, lse_ref):
    blk = ATTN_BLOCK
    first_valid_col = jnp.where(pl.program_id(1) == 0, blk, 0)
    lane = lax.broadcasted_iota(jnp.int32, (blk, LANES), 1)
    low = lane < ATTN_HEAD_DIM
    col = lax.broadcasted_iota(jnp.int32, (blk, 2 * blk), 1)
    n_sub = q_ref.shape[1] // blk
    dn = (((1,), (1,)), ((), ()))
    for sq in range(q_ref.shape[0]):
        for n in range(n_sub):
            rows = slice(n * blk, (n + 1) * blk)
            lse_w = jnp.zeros((blk, LANES), F32)
            for pair in range(ATTN_HEADS_PER_GROUP // 2):
                cols = slice(pair * LANES, (pair + 1) * LANES)
                q32 = q_ref[sq, rows, cols].astype(F32)
                if n == 0:
                    kk = jnp.concatenate([kp_ref[sq, :, cols], kc_ref[sq, rows, cols]], axis=0)
                    vv = jnp.concatenate([vp_ref[sq, :, cols], vc_ref[sq, rows, cols]], axis=0)
                else:
                    kk = kc_ref[sq, (n - 1) * blk:(n + 1) * blk, cols]
                    vv = vc_ref[sq, (n - 1) * blk:(n + 1) * blk, cols]
                outs = []
                for sub in range(2):
                    hh = 2 * pair + sub
                    qh = jnp.where(low if sub == 0 else ~low, q32, 0.0).astype(BF16)
                    s = lax.dot_general(qh, kk, dn, preferred_element_type=F32) + bm_ref[hh]
                    if n == 0:
                        s = jnp.where(col >= first_valid_col, s, NEG_INF)
                    m = jnp.max(s, axis=-1, keepdims=True)
                    p = jnp.exp(s - m)
                    l = jnp.sum(p, axis=-1, keepdims=True)
                    o = jnp.dot(p.astype(BF16), vv, preferred_element_type=F32) * (1.0 / l)
                    outs.append(o)
                    lse_w = jnp.where(lane == hh, m + jnp.log(l), lse_w)
                o_ref[sq, rows, cols] = jnp.where(low, outs[0], outs[1]).astype(BF16)
            lse_ref[sq, rows, :] = lse_w


def _attention(qkv, bias_mask):
    ns, m_len, _ = qkv.shape
    tq = min(TQ_ATTN, m_len)
    per_step = TQ_ATTN // tq
    per = tq // ATTN_BLOCK

    def cur(part):
        return pl.BlockSpec((per_step, tq, D_GROUP), lambda s, i: (s, i, part))

    def prev(part):
        return pl.BlockSpec((per_step, ATTN_BLOCK, D_GROUP), lambda s, i: (s, jnp.maximum(i * per - 1, 0), part))

    return pl.pallas_call(
        _attn_kernel,
        grid=(ns // per_step, m_len // tq),
        in_specs=[cur(0), cur(1), prev(1), cur(2), prev(2),
                  _const_spec((ATTN_HEADS_PER_GROUP, ATTN_BLOCK, 2 * ATTN_BLOCK))],
        out_specs=[cur(0), pl.BlockSpec((per_step, tq, LANES), lambda s, i: (s, i, 0))],
        out_shape=[jax.ShapeDtypeStruct((ns, m_len, D_GROUP), BF16),
                   jax.ShapeDtypeStruct((ns, m_len, LANES), F32)],
        name="attn",
    )(qkv, qkv, qkv, qkv, qkv, bias_mask)


def _merge_kernel(final, x_ref, y2_ref, o0_ref, o1_ref, o2_ref, l0_ref, l1_ref, l2_ref, qm_ref, km_ref, vm_ref,
                  g_ref, wzs_ref, wza_ref, wzm_ref, wg0_ref, wg1_ref, wg2_ref, wg3_ref, wg4_ref, wg5_ref,
                  bg_ref, wglu_ref, bglu_ref, wbs_ref, wba_ref, wbm_ref, wout_ref, fg_ref, out_ref,
                  y_scr, o_scr, l_scr):
    i = pl.program_id(1)
    n_blocks = pl.num_programs(1) - 1
    tm = x_ref.shape[1]
    tiles = D_GROUP // LANES
    dilated = ((1, o1_ref, l1_ref), (2, o2_ref, l2_ref))

    def prepare(slot):
        for j in range(SSM_TILES):
            groups = [y2_ref[0, :, j * SSM_TILE_W + gl * LANES:j * SSM_TILE_W + (gl + 1) * LANES]
                      for gl in range(SSM_TILE_GROUPS)]
            for t, blk in enumerate(_block_transpose(groups)):
                y_scr[slot, j, pl.ds(t, tm // SSM_CHUNK, stride=SSM_CHUNK), :] = blk
        for gi, o_ref, l_ref in dilated:
            r = ATTN_CONFIGS[gi][1]
            for s in range(r):
                rows = pl.ds(s, tm // r, stride=r)
                for c in range(tiles):
                    o_scr[slot, (gi - 1) * tiles + c, rows, :] = o_ref[0, s, :, c * LANES:(c + 1) * LANES].astype(F32)
                l_scr[slot, gi - 1, rows, :] = l_ref[0, s]

    def compute(slot):
        x = x_ref[0]
        h = _rms(x, g_ref[...]).astype(BF16)

        def hdot(w_ref):
            return jnp.dot(h, w_ref[...], preferred_element_type=F32)

        yg = jax.nn.gelu(jnp.concatenate([y_scr[slot, j] for j in range(SSM_TILES)], axis=-1))
        t = jnp.dot(yg.astype(BF16), wglu_ref[...], preferred_element_type=F32) + bglu_ref[...]
        o_ssm = yg * jax.nn.sigmoid(t) * jax.nn.silu(hdot(wzs_ref))
        p_ssm = jnp.dot(o_ssm.astype(BF16), wbs_ref[...], preferred_element_type=F32)

        ls = (l0_ref[0, 0], l_scr[slot, 0], l_scr[slot, 1])
        os_ = [o0_ref[0, 0].astype(F32)]
        os_ += [jnp.concatenate([o_scr[slot, g * tiles + c] for c in range(tiles)], axis=-1) for g in range(2)]
        mx = jnp.maximum(jnp.maximum(ls[0], ls[1]), ls[2])
        es = [jnp.exp(l - mx) for l in ls]
        inv = 1.0 / (es[0] + es[1] + es[2])
        head_of_lane = lax.broadcasted_iota(jnp.int32, (tm, D_GROUP), 1) // ATTN_HEAD_DIM
        parts = []
        for e, o_g in zip(es, os_):
            alpha = e * inv
            wide = jnp.zeros((tm, D_GROUP), F32)
            for j in range(ATTN_HEADS_PER_GROUP):
                wide = jnp.where(head_of_lane == j, alpha[:, j:j + 1], wide)
            parts.append(o_g * wide)
        o_attn = jnp.concatenate(parts, axis=-1) * jax.nn.silu(hdot(wza_ref))
        p_attn = jnp.dot(o_attn.astype(BF16), wba_ref[...], preferred_element_type=F32)

        dn = (((1,), (1,)), ((), ()))
        heads = []
        for hd in range(MEM_HEADS):
            cols = slice(hd * MEM_HEAD_DIM, (hd + 1) * MEM_HEAD_DIM)
            s = lax.dot_general(qm_ref[0, :, cols], km_ref[0, :, cols], dn, preferred_element_type=F32)
            s = s * (MEM_HEAD_DIM ** -0.5)
            m = jnp.max(s, axis=-1, keepdims=True)
            p = jnp.exp(s - m)
            l = jnp.sum(p, axis=-1, keepdims=True)
            heads.append(jnp.dot(p.astype(BF16), vm_ref[0, :, cols], preferred_element_type=F32) * (1.0 / l))
        o_mem = jnp.concatenate(heads, axis=-1) * jax.nn.silu(hdot(wzm_ref))
        p_mem = jnp.dot(o_mem.astype(BF16), wbm_ref[...], preferred_element_type=F32)

        gate_refs = (wg0_ref, wg1_ref, wg2_ref, wg3_ref, wg4_ref, wg5_ref)
        per_branch = D_MODEL // GATE_W
        halves = []
        for part in range(per_branch):
            acc = jnp.zeros((tm, GATE_W), F32)
            for br, p_br in enumerate((p_ssm, p_attn, p_mem)):
                k = br * per_branch + part
                gate = jax.nn.sigmoid(hdot(gate_refs[k]) + bg_ref[:, k * GATE_W:(k + 1) * GATE_W])
                acc = acc + gate * p_br[:, part * GATE_W:(part + 1) * GATE_W]
            halves.append(acc)
        merged = jnp.concatenate(halves, axis=-1)
        xn = x + jnp.dot(merged.astype(BF16), wout_ref[...], preferred_element_type=F32)
        if final:
            xn = _rms(xn, fg_ref[...])
        out_ref[0] = xn

    slot = i % 2

    @pl.when(i == 0)
    def _():
        prepare(slot)

    @pl.when(jnp.logical_and(i > 0, i < n_blocks))
    def _():
        compute(1 - slot)
        prepare(slot)

    @pl.when(i == n_blocks)
    def _():
        compute(1 - slot)


def _merge(final, layer, x, y2, o_groups, lse_groups, qm, k_mem, v_mem, g, w_in_bf, bg, wglu, bglu, wbs, wba,
           wbm, wout, fg):
    B, L, _ = x.shape
    tm = TM_MERGE
    n_blocks = L // tm

    def cur(i):
        return jnp.maximum(i - 1, 0)

    def nxt(i):
        return jnp.minimum(i, n_blocks - 1)

    def rows(w):
        return pl.BlockSpec((1, tm, w), lambda b, i: (b, cur(i), 0))

    def dec(r, w, which):
        return pl.BlockSpec((1, r, tm // r, w), lambda b, i: (b, 0, which(i), 0))

    mem_spec = pl.BlockSpec((1,) + k_mem.shape[1:], lambda b, i: (b, 0, 0))
    rs = [r for _, r in ATTN_CONFIGS]
    assert rs[0] == 1
    n_gate = N_BRANCHES * D_MODEL // GATE_W
    in_specs = ([rows(D_MODEL),
                 pl.BlockSpec((1, tm // SSM_CHUNK, D_SSM * SSM_CHUNK), lambda b, i: (b, nxt(i), 0))]
                + [dec(r, D_GROUP, cur if r == 1 else nxt) for r in rs]
                + [dec(r, LANES, cur if r == 1 else nxt) for r in rs]
                + [rows(D_MEM), mem_spec, mem_spec, _const_spec(g.shape)]
                + [_w_in_spec(n, layer) for n in ("z_ssm", "z_attn", "z_mem")]
                + [_w_in_spec("gates", layer, part) for part in range(n_gate)]
                + [_const_spec(a.shape) for a in (bg, wglu, bglu, wbs, wba, wbm, wout, fg)])
    n_dil = len(rs) - 1
    return pl.pallas_call(
        functools.partial(_merge_kernel, final),
        grid=(B, n_blocks + 1),
        in_specs=in_specs,
        out_specs=rows(D_MODEL),
        out_shape=jax.ShapeDtypeStruct((B, L, D_MODEL), F32),
        scratch_shapes=[pltpu.VMEM((2, D_SSM // LANES, tm, LANES), F32),
                        pltpu.VMEM((2, n_dil * D_GROUP // LANES, tm, LANES), F32),
                        pltpu.VMEM((2, n_dil, tm, LANES), F32)],
        compiler_params=pltpu.CompilerParams(vmem_limit_bytes=VMEM_LIMIT),
        name="merge",
    )(x, y2, *o_groups, *lse_groups, qm, k_mem, v_mem, g, *([w_in_bf] * (3 + n_gate)), bg, wglu, bglu, wbs, wba,
      wbm, wout, fg)


def kernel(x, mem, norm_g, mem_norm_g, w_in, b_gate, ssm_lambda_re, ssm_lambda_im, ssm_log_dt, ssm_b_re,
           ssm_b_im, ssm_c_re, ssm_c_im, ssm_d, w_glu, b_glu, w_mem_kv, w_br_ssm, w_br_attn, w_br_mem,
           w_out, rel_bias, final_norm_g):
    B, L, _ = x.shape
    assert L % (ATTN_CONFIGS[-1][1] * ATTN_BLOCK) == 0 and L % TM_INPROJ == 0 and L % TM_MERGE == 0
    bias_masks = [_bias_mask(rel_bias[:, gi * ATTN_HEADS_PER_GROUP:(gi + 1) * ATTN_HEADS_PER_GROUP], win, dil)
                  for gi, (win, dil) in enumerate(ATTN_CONFIGS)]
    fg = final_norm_g.reshape(1, D_MODEL)
    w_in_bf = w_in.astype(BF16)
    def groups_of_all_layers(a):
        return a.reshape((DEPTH * SSM_GROUPS,) + a.shape[2:])

    ssm_tables = _ssm_prep(*(groups_of_all_layers(a) for a in (
        ssm_lambda_re, ssm_lambda_im, ssm_log_dt, ssm_b_re, ssm_b_im, ssm_c_re, ssm_c_im,
        ssm_d.reshape(DEPTH, SSM_GROUPS, SSM_GROUP))))
    ssm_tables = [t.reshape((DEPTH, SSM_TILES) + t.shape[1:]) for t in ssm_tables]
    n_chunks = B * L // SSM_CHUNK
    for layer in range(DEPTH):
        g = norm_g[layer].reshape(1, D_MODEL)
        k_mem, v_mem = _mem_kv(mem, mem_norm_g[layer].reshape(1, D_MODEL), w_mem_kv[layer].astype(BF16))
        u2, *qkv, qm = _in_proj(x, g, w_in_bf, layer)

        y = _ssm(u2.reshape(n_chunks, D_SSM * SSM_CHUNK), ssm_tables, layer, batch=B)
        y = y.reshape(B, L // SSM_CHUNK, D_SSM * SSM_CHUNK)

        o_groups, lse_groups = [], []
        for gi, (_, r) in enumerate(ATTN_CONFIGS):
            m_len = L // r
            o_g, lse_g = _attention(qkv[gi].reshape(B * r, m_len, 3 * D_GROUP), bias_masks[gi])
            o_groups.append(o_g.reshape(B, r, m_len, D_GROUP))
            lse_groups.append(lse_g.reshape(B, r, m_len, LANES))

        x = _merge(layer == DEPTH - 1, layer, x, y, o_groups, lse_groups, qm, k_mem, v_mem, g, w_in_bf,
                   b_gate[layer].reshape(1, -1), w_glu[layer].astype(BF16), b_glu[layer].reshape(1, -1),
                   w_br_ssm[layer].astype(BF16), w_br_attn[layer].astype(BF16), w_br_mem[layer].astype(BF16),
                   w_out[layer].astype(BF16), fg)
    return x
```
